```python
import math
import jax, jax.numpy as jnp
from jax import lax
import numpy as np

D_MODEL = 2048
BATCH = 4
SEQ = 8192
DEPTH = 4

HEAD_DIM = 128
FOX_HEADS = 8
DIL_PATTERNS = ((128, 1), (512, 4), (2048, 16))
DIL_HEADS_PER_GROUP = 4
DIL_HEADS = DIL_HEADS_PER_GROUP * len(DIL_PATTERNS)
D_FF = 5632
PLE_DIM = 256
ROPE_THETA = 10000.0
RMS_EPS = 1e-6
Q_BLOCK = 128
N_NORMS = 8
FOX_W = FOX_HEADS * HEAD_DIM
DIL_W = DIL_HEADS * HEAD_DIM
DIL_OUT_W = DIL_HEADS_PER_GROUP * HEAD_DIM
N_COLS = 3 * FOX_W + FOX_HEADS + 3 * DIL_W + 2 * D_MODEL

kernel_name = "fox_dilated_gated_hybrid_trunk"


def rms_norm(x, g):
    xf = x.astype(jnp.float32)
    y = xf * lax.rsqrt(jnp.mean(xf * xf, axis=-1, keepdims=True) + RMS_EPS)
    return (y * g.astype(jnp.float32)).astype(x.dtype)


def swiglu(u, w_in, w_out):
    a, b = jnp.split(u @ w_in, 2, axis=-1)
    return (jax.nn.silu(a) * b) @ w_out


def split_heads(t, n_heads):
    B, S, _ = t.shape
    return t.reshape(B, S, n_heads, HEAD_DIM).transpose(0, 2, 1, 3)


def merge_heads(t):
    B, H, S, hd = t.shape
    return t.transpose(0, 2, 1, 3).reshape(B, S, H * hd)


def rope(x, pos):
    half = HEAD_DIM // 2
    inv = ROPE_THETA ** (-jnp.arange(half, dtype=jnp.float32) * 2.0 / HEAD_DIM)
    ang = pos[:, None] * inv[None, :]
    cos = jnp.cos(ang).astype(x.dtype)
    sin = jnp.sin(ang).astype(x.dtype)
    x1, x2 = x[..., :half], x[..., half:]
    return jnp.concatenate([x1 * cos - x2 * sin, x2 * cos + x1 * sin], axis=-1)


def fox_attention(q, k, v, log_f):
    B, H, S, hd = q.shape
    nb = S // Q_BLOCK
    c = jnp.cumsum(log_f.astype(jnp.float32), axis=-1)
    q_blocks = q.reshape(B, H, nb, Q_BLOCK, hd).transpose(2, 0, 1, 3, 4)
    c_blocks = c.reshape(B, H, nb, Q_BLOCK).transpose(2, 0, 1, 3)
    key_pos = jnp.arange(S)
    scale = hd ** -0.5

    def block(args):
        q_i, c_i, i = args
        s = jnp.einsum('bhqd,bhkd->bhqk', q_i, k).astype(jnp.float32) * scale
        s = s + c_i[..., None] - c[:, :, None, :]
        q_pos = i * Q_BLOCK + jnp.arange(Q_BLOCK)
        s = jnp.where(q_pos[:, None] >= key_pos[None, :], s, -jnp.inf)
        p = jax.nn.softmax(s, axis=-1)
        return jnp.einsum('bhqk,bhkd->bhqd', p.astype(v.dtype), v)

    o = lax.map(block, (q_blocks, c_blocks, jnp.arange(nb)))
    return o.transpose(1, 2, 0, 3, 4).reshape(B, H, S, hd)


def dilated_group_attention(q, k, v, window, dilation):
    B, H, S, hd = q.shape
    L = S // dilation
    nk = window // dilation
    nb = -(-L // nk)
    Lp = nb * nk

    def by_residue(t):
        return t.reshape(B, H, L, dilation, hd).transpose(0, 1, 3, 2, 4)

    qs = jnp.pad(by_residue(q), ((0, 0), (0, 0), (0, 0), (0, Lp - L), (0, 0)))
    qs = qs.reshape(B, H, dilation, nb, nk, hd)

    def band(t):
        tp = jnp.pad(by_residue(t), ((0, 0), (0, 0), (0, 0), (nk, Lp - L), (0, 0)))
        prev = tp[..., :Lp, :].reshape(B, H, dilation, nb, nk, hd)
        cur = tp[..., nk:, :].reshape(B, H, dilation, nb, nk, hd)
        return jnp.concatenate([prev, cur], axis=-2)

    kb, vb = band(k), band(v)
    s = jnp.einsum('bhrnqd,bhrnkd->bhrnqk', qs, kb).astype(jnp.float32) * (hd ** -0.5)
    qi = jnp.arange(nk)[:, None]
    ki = jnp.arange(2 * nk)[None, :]
    dist = qi + nk - ki
    blk = jnp.arange(nb)[:, None, None]
    valid = (dist >= 0) & (dist <= nk) & (blk * nk - nk + ki >= 0)
    s = jnp.where(valid, s, -jnp.inf)
    lse = jax.nn.logsumexp(s, axis=-1)
    p = jnp.exp(s - lse[..., None])
    o = jnp.einsum('bhrnqk,bhrnkd->bhrnqd', p.astype(vb.dtype), vb)
    o = o.reshape(B, H, dilation, Lp, hd)[..., :L, :].transpose(0, 1, 3, 2, 4).reshape(B, H, S, hd)
    lse = lse.reshape(B, H, dilation, Lp)[..., :L].transpose(0, 1, 3, 2).reshape(B, H, S)
    return o, lse


def token_mixer(u, w_in, b_f, w_br_a, w_br_b, w_out):
    B, S, _ = u.shape
    sizes = (FOX_W, FOX_W, FOX_W, FOX_HEADS, DIL_W, DIL_W, DIL_W, D_MODEL, D_MODEL)
    offsets = [int(o) for o in np.cumsum(sizes)[:-1]]
    z = u @ w_in
    q_a, k_a, v_a, f_a, q_b, k_b, v_b, g_a, g_b = jnp.split(z, offsets, axis=-1)

    log_f = jax.nn.log_sigmoid((f_a + b_f).astype(jnp.float32)).transpose(0, 2, 1)
    o_a = fox_attention(split_heads(q_a, FOX_HEADS), split_heads(k_a, FOX_HEADS),
                        split_heads(v_a, FOX_HEADS), log_f)
    y_a = merge_heads(o_a) @ w_br_a

    pos = jnp.arange(S, dtype=jnp.float32)
    qb = rope(split_heads(q_b, DIL_HEADS), pos)
    kb = rope(split_heads(k_b, DIL_HEADS), pos)
    vb = split_heads(v_b, DIL_HEADS)
    outs, lses = [], []
    for g, (window, dilation) in enumerate(DIL_PATTERNS):
        sl = slice(g * DIL_HEADS_PER_GROUP, (g + 1) * DIL_HEADS_PER_GROUP)
        o_g, lse_g = dilated_group_attention(qb[:, sl], kb[:, sl], vb[:, sl], window, dilation)
        outs.append(o_g)
        lses.append(lse_g)
    w_g = jax.nn.softmax(jnp.stack(lses, axis=0), axis=0)
    o_b = jnp.sum(w_g[..., None].astype(vb.dtype) * jnp.stack(outs, axis=0), axis=0)
    y_b = merge_heads(o_b) @ w_br_b

    merged = jax.nn.sigmoid(g_a) * y_a + jax.nn.sigmoid(g_b) * y_b
    return merged @ w_out


def setup_inputs(seed: int = 0) -> dict:
    key = jax.random.key(seed)
    ks = jax.random.split(key, 16)

    def w(k, shape, fan_in):
        return jax.random.normal(k, shape, jnp.float32) * (fan_in ** -0.5)

    return {
        "x": jax.random.normal(ks[0], (BATCH, SEQ, D_MODEL), jnp.float32),
        "p": jax.random.normal(ks[1], (DEPTH, BATCH, SEQ, PLE_DIM), jnp.float32),
        "norm_g": 1.0 + 0.05 * jax.random.normal(ks[2], (DEPTH, N_NORMS, D_MODEL), jnp.float32),
        "ffn1_w_in": w(ks[3], (DEPTH, D_MODEL, 2 * D_FF), D_MODEL),
        "ffn1_w_out": w(ks[4], (DEPTH, D_FF, D_MODEL), D_FF),
        "mix_w_in": w(ks[5], (DEPTH, D_MODEL, N_COLS), D_MODEL),
        "fox_b_f": jax.random.uniform(ks[6], (DEPTH, FOX_HEADS), jnp.float32, 1.0, 4.0),
        "mix_w_br_a": w(ks[7], (DEPTH, FOX_W, D_MODEL), FOX_W),
        "mix_w_br_b": w(ks[8], (DEPTH, DIL_OUT_W, D_MODEL), DIL_OUT_W),
        "mix_w_out": w(ks[9], (DEPTH, D_MODEL, D_MODEL), D_MODEL),
        "ffn2_w_in": w(ks[10], (DEPTH, D_MODEL, 2 * D_FF), D_MODEL),
        "ffn2_w_out": w(ks[11], (DEPTH, D_FF, D_MODEL), D_FF),
        "ple_w_proj": w(ks[12], (DEPTH, PLE_DIM, D_MODEL), PLE_DIM),
        "ple_w_gate": w(ks[13], (DEPTH, D_MODEL, D_MODEL), D_MODEL),
    }


def reference(x, p, norm_g, ffn1_w_in, ffn1_w_out, mix_w_in, fox_b_f, mix_w_br_a,
              mix_w_br_b, mix_w_out, ffn2_w_in, ffn2_w_out, ple_w_proj, ple_w_gate):
    for i in range(DEPTH):
        g = norm_g[i]
        x = x + 0.5 * rms_norm(swiglu(rms_norm(x, g[0]), ffn1_w_in[i], ffn1_w_out[i]), g[1])
        y = token_mixer(rms_norm(x, g[2]), mix_w_in[i], fox_b_f[i], mix_w_br_a[i],
                        mix_w_br_b[i], mix_w_out[i])
        x = x + rms_norm(y, g[3])
        x = x + 0.5 * rms_norm(swiglu(rms_norm(x, g[4]), ffn2_w_in[i], ffn2_w_out[i]), g[5])
        e = p[i] @ ple_w_proj[i]
        gate = jax.nn.sigmoid(rms_norm(x, g[6]) @ ple_w_gate[i])
        x = x + rms_norm(gate * e, g[7])
    return x
```

```python
import functools

import jax
import jax.numpy as jnp
import numpy as np
from jax import lax
from jax.experimental import pallas as pl
from jax.experimental.pallas import tpu as pltpu

HEAD_DIM = 128
DIL_PATTERNS = ((128, 1), (512, 4), (2048, 16))
DIL_BLOCK = 128
ROPE_THETA = 10000.0
RMS_EPS = 1e-6
MASK_VALUE = -1e30
VMEM_LIMIT_V7X = 56 * 1024 * 1024

F32 = jnp.float32
BF16 = jnp.bfloat16


def _rms(x, g):
    return x * lax.rsqrt(jnp.mean(x * x, axis=-1, keepdims=True) + RMS_EPS) * g


def _tile(n, want):
    t = min(n, want)
    assert n % t == 0, (n, t)
    return t


def _params(sem, vmem=VMEM_LIMIT_V7X):
    return pltpu.CompilerParams(dimension_semantics=sem, vmem_limit_bytes=vmem)


def _ffn_body(x_ref, g_ref, wa_ref, wb_ref, wo_ref, o_ref, u_ref, acc_ref, *, gi, nj):
    j = pl.program_id(1)

    @pl.when(j == 0)
    def _():
        u_ref[...] = _rms(x_ref[...], g_ref[gi:gi + 1, :]).astype(BF16)

    u = u_ref[...]
    a = jnp.dot(u, wa_ref[...], preferred_element_type=F32)
    b = jnp.dot(u, wb_ref[...], preferred_element_type=F32)
    h = (a * jax.nn.sigmoid(a) * b).astype(BF16)
    part = jnp.dot(h, wo_ref[...], preferred_element_type=F32)

    @pl.when(j == 0)
    def _():
        acc_ref[...] = part

    @pl.when(j > 0)
    def _():
        acc_ref[...] += part

    @pl.when(j == nj - 1)
    def _():
        o_ref[...] = x_ref[...] + 0.5 * _rms(acc_ref[...], g_ref[gi + 1:gi + 2, :])


def _ffn(x2, g, w_in, w_out, gi):
    T, D = x2.shape
    F = w_out.shape[0]
    tm = _tile(T, 512)
    tf = _tile(F, 512)
    nj = F // tf
    return pl.pallas_call(
        functools.partial(_ffn_body, gi=gi, nj=nj),
        out_shape=jax.ShapeDtypeStruct((T, D), F32),
        grid=(T // tm, nj),
        in_specs=[
            pl.BlockSpec((tm, D), lambda i, j: (i, 0)),
            pl.BlockSpec(g.shape, lambda i, j: (0, 0)),
            pl.BlockSpec((D, tf), lambda i, j: (0, j)),
            pl.BlockSpec((D, tf), lambda i, j: (0, j + nj)),
            pl.BlockSpec((tf, D), lambda i, j: (j, 0)),
        ],
        out_specs=pl.BlockSpec((tm, D), lambda i, j: (i, 0)),
        scratch_shapes=[pltpu.VMEM((tm, D), BF16), pltpu.VMEM((tm, D), F32)],
        compiler_params=_params(("parallel", "arbitrary")),
        name="ffn",
    )(x2, g, w_in, w_in, w_out)


def _rope(t, cos, sin):
    return t * cos + pltpu.roll(t, HEAD_DIM // 2, 1) * sin


def _proj_body(x_ref, g_ref, w_ref, wf_ref, cos_ref, sin_ref, z_ref, ft_ref, u_ref, *,
               gi, tn, bounds, n_fox_heads):
    j = pl.program_id(2)
    gate_end, qa_end, va_end, qb_end, kb_end = bounds
    scale = HEAD_DIM ** -0.5

    @pl.when(j == 0)
    def _():
        u = _rms(x_ref[0], g_ref[gi:gi + 1, :]).astype(BF16)
        u_ref[...] = u
        f = jnp.dot(u, wf_ref[...], preferred_element_type=F32)
        ft_ref[0] = jnp.transpose(f)[0:n_fox_heads, :]

    acc = jnp.dot(u_ref[...], w_ref[...], preferred_element_type=F32)

    def roped(t, s):
        parts = []
        for c in range(tn // HEAD_DIM):
            sl = t[:, c * HEAD_DIM:(c + 1) * HEAD_DIM]
            parts.append(_rope(sl, cos_ref[...], sin_ref[...]) * s)
        return jnp.concatenate(parts, axis=1)

    @pl.when(j < gate_end)
    def _():
        z_ref[0] = jax.nn.sigmoid(acc).astype(BF16)

    @pl.when((j >= gate_end) & (j < qa_end))
    def _():
        z_ref[0] = (acc * scale).astype(BF16)

    @pl.when(((j >= qa_end) & (j < va_end)) | (j >= kb_end))
    def _():
        z_ref[0] = acc.astype(BF16)

    @pl.when((j >= va_end) & (j < qb_end))
    def _():
        z_ref[0] = roped(acc, scale).astype(BF16)

    @pl.when((j >= qb_end) & (j < kb_end))
    def _():
        z_ref[0] = roped(acc, 1.0).astype(BF16)


def _proj(x, g, w_main, w_f, cos, sin, gi, fox_w, dil_w, n_fox_heads):
    B, S, D = x.shape
    N = w_main.shape[1]
    tm = _tile(S, 512)
    tn = int(np.gcd.reduce([512, fox_w, dil_w, D]))
    assert tn % HEAD_DIM == 0 and N % tn == 0
    e = np.cumsum([2 * D, fox_w, 2 * fox_w, dil_w, dil_w]) // tn
    bounds = tuple(int(v) for v in e)
    return pl.pallas_call(
        functools.partial(_proj_body, gi=gi, tn=tn, bounds=bounds, n_fox_heads=n_fox_heads),
        out_shape=(jax.ShapeDtypeStruct((B, S, N), BF16),
                   jax.ShapeDtypeStruct((B, n_fox_heads, S), F32)),
        grid=(B, S // tm, N // tn),
        in_specs=[
            pl.BlockSpec((1, tm, D), lambda b, i, j: (b, i, 0)),
            pl.BlockSpec(g.shape, lambda b, i, j: (0, 0)),
            pl.BlockSpec((D, tn), lambda b, i, j: (0, j)),
            pl.BlockSpec(w_f.shape, lambda b, i, j: (0, 0)),
            pl.BlockSpec((tm, HEAD_DIM), lambda b, i, j: (i, 0)),
            pl.BlockSpec((tm, HEAD_DIM), lambda b, i, j: (i, 0)),
        ],
        out_specs=(pl.BlockSpec((1, tm, tn), lambda b, i, j: (b, i, j)),
                   pl.BlockSpec((1, n_fox_heads, tm), lambda b, i, j: (b, 0, i))),
        scratch_shapes=[pltpu.VMEM((tm, D), BF16)],
        compiler_params=_params(("parallel", "parallel", "arbitrary")),
        name="mixer_proj",
    )(x, g, w_main, w_f, cos, sin)


def _cumsum_body(ft_ref, b_ref, c_ref, *, chunk):
    H, S = ft_ref.shape[1], ft_ref.shape[2]
    row = lax.broadcasted_iota(jnp.int32, (chunk, chunk), 0)
    col = lax.broadcasted_iota(jnp.int32, (chunk, chunk), 1)
    tri = jnp.where(row <= col, 1.0, 0.0).astype(BF16)

    def step(k, carry):
        off = pl.multiple_of(k * chunk, chunk)
        t = ft_ref[0, :, pl.ds(off, chunk)] + b_ref[...]
        ls = jnp.minimum(t, 0.0) - jnp.log1p(jnp.exp(-jnp.abs(t)))
        hi = ls.astype(BF16)
        r1 = ls - hi.astype(F32)
        mid = r1.astype(BF16)
        lo = (r1 - mid.astype(F32)).astype(BF16)
        cs = (jnp.dot(hi, tri, preferred_element_type=F32)
              + jnp.dot(mid, tri, preferred_element_type=F32)
              + jnp.dot(lo, tri, preferred_element_type=F32)) + carry
        c_ref[0, :, pl.ds(off, chunk)] = cs
        return cs[:, chunk - 1:chunk]

    lax.fori_loop(0, S // chunk, step, jnp.zeros((H, 1), F32))


def _forget_cumsum(ft, b_f):
    B, H, S = ft.shape
    chunk = _tile(S, 128)
    return pl.pallas_call(
        functools.partial(_cumsum_body, chunk=chunk),
        out_shape=jax.ShapeDtypeStruct((B, H, S), F32),
        grid=(B,),
        in_specs=[pl.BlockSpec((1, H, S), lambda b: (b, 0, 0)),
                  pl.BlockSpec((H, 1), lambda b: (0, 0))],
        out_specs=pl.BlockSpec((1, H, S), lambda b: (b, 0, 0)),
        compiler_params=_params(("parallel",), 32 * 1024 * 1024),
        name="forget_cumsum",
    )(ft, b_f.reshape(H, 1))


def _fox_body(q_ref, k_ref, v_ref, c_ref, o_ref, m_ref, l_ref, acc_ref, *, tq, tk):
    qi = pl.program_id(2)
    q = q_ref[0]
    q0 = pl.multiple_of(qi * tq, tq)
    c0 = c_ref[0, 0, :, pl.ds(q0, HEAD_DIM)][:, 0:1]

    m_ref[...] = jnp.full(m_ref.shape, -jnp.inf, F32)
    l_ref[...] = jnp.zeros(l_ref.shape, F32)
    acc_ref[...] = jnp.zeros(acc_ref.shape, F32)

    def tile(k0, mask):
        k = k_ref[0, pl.ds(k0, tk), :]
        v = v_ref[0, pl.ds(k0, tk), :]
        s = lax.dot_general(q, k, (((1,), (1,)), ((), ())), preferred_element_type=F32)
        s = s + (c0 - c_ref[0, 0, :, pl.ds(k0, tk)])
        if mask is not None:
            s = jnp.where(mask, s, MASK_VALUE)
        m_prev = m_ref[...]
        m_new = jnp.maximum(m_prev, jnp.max(s, axis=1, keepdims=True))
        alpha = jnp.exp(m_prev - m_new)
        p = jnp.exp(s - m_new)
        l_ref[...] = alpha * l_ref[...] + jnp.sum(p, axis=1, keepdims=True)
        acc_ref[...] = alpha * acc_ref[...] + jnp.dot(p.astype(BF16), v, preferred_element_type=F32)
        m_ref[...] = m_new

    def full_tile(kj, carry):
        tile(pl.multiple_of(kj * tk, tk), None)
        return carry

    lax.fori_loop(0, (qi * tq) // tk, full_tile, 0)

    row = lax.broadcasted_iota(jnp.int32, (tq, tk), 0)
    col = lax.broadcasted_iota(jnp.int32, (tq, tk), 1)
    for d in range(tq // tk):
        tile(pl.multiple_of(q0 + d * tk, tk), row >= col + d * tk)

    o_ref[0] = (acc_ref[...] / l_ref[...]).astype(o_ref.dtype)


def _fox(z, c4, n_heads, col0):
    B, S, _ = z.shape
    tq = _tile(S, 512)
    tk = _tile(tq, 512)
    hb = col0 // HEAD_DIM
    return pl.pallas_call(
        functools.partial(_fox_body, tq=tq, tk=tk),
        out_shape=jax.ShapeDtypeStruct((B, S, n_heads * HEAD_DIM), BF16),
        grid=(B, n_heads, S // tq),
        in_specs=[
            pl.BlockSpec((1, tq, HEAD_DIM), lambda b, h, i: (b, i, hb + h)),
            pl.BlockSpec((1, S, HEAD_DIM), lambda b, h, i: (b, 0, hb + n_heads + h)),
            pl.BlockSpec((1, S, HEAD_DIM), lambda b, h, i: (b, 0, hb + 2 * n_heads + h)),
            pl.BlockSpec((1, 1, 1, S), lambda b, h, i: (b, h, 0, 0)),
        ],
        out_specs=pl.BlockSpec((1, tq, HEAD_DIM), lambda b, h, i: (b, i, h)),
        scratch_shapes=[pltpu.VMEM((tq, 1), F32), pltpu.VMEM((tq, 1), F32),
                        pltpu.VMEM((tq, HEAD_DIM), F32)],
        compiler_params=_params(("parallel", "parallel", "arbitrary")),
        name="fox_attention",
    )(z, z, z, c4)


def _dil_body(q_ref, k_ref, v_ref, o_ref, lse_ref):
    nk = DIL_BLOCK
    L = q_ref.shape[2]

    def attend(q, k, v, mask):
        s = lax.dot_general(q, k, (((1,), (1,)), ((), ())), preferred_element_type=F32)
        s = jnp.where(mask, s, MASK_VALUE)
        m = jnp.max(s, axis=1, keepdims=True)
        p = jnp.exp(s - m)
        l = jnp.sum(p, axis=1, keepdims=True)
        o = jnp.dot(p.astype(BF16), v, preferred_element_type=F32) / l
        return o, m + jnp.log(l)

    def emit(r0, o, lse):
        o_ref[0, 0, pl.ds(r0, nk), :] = o
        lse_ref[0, 0, pl.ds(r0, nk), :] = jnp.broadcast_to(lse, (nk, HEAD_DIM))

    qi = lax.broadcasted_iota(jnp.int32, (nk, nk), 0)
    ki = lax.broadcasted_iota(jnp.int32, (nk, nk), 1)
    o, lse = attend(q_ref[0, 0, 0:nk, :], k_ref[0, 0, 0:nk, :], v_ref[0, 0, 0:nk, :], ki <= qi)
    emit(0, o, lse)

    qi2 = lax.broadcasted_iota(jnp.int32, (nk, 2 * nk), 0)
    ki2 = lax.broadcasted_iota(jnp.int32, (nk, 2 * nk), 1)
    band = (ki2 >= qi2) & (ki2 <= qi2 + nk)

    def step(n, carry):
        r0 = pl.multiple_of(n * nk, nk)
        p0 = pl.multiple_of((n - 1) * nk, nk)
        o, lse = attend(q_ref[0, 0, pl.ds(r0, nk), :], k_ref[0, 0, pl.ds(p0, 2 * nk), :],
                        v_ref[0, 0, pl.ds(p0, 2 * nk), :], band)
        emit(r0, o, lse)
        return carry

    lax.fori_loop(1, L // nk, step, 0)


def _dilated_group(q, k, v):
    B, d, L, W = q.shape
    spec = pl.BlockSpec((1, 1, L, HEAD_DIM), lambda b, h, r: (b, r, 0, h))
    return pl.pallas_call(
        _dil_body,
        out_shape=(jax.ShapeDtypeStruct((B, d, L, W), F32), jax.ShapeDtypeStruct((B, d, L, W), F32)),
        grid=(B, W // HEAD_DIM, d),
        in_specs=[spec, spec, spec],
        out_specs=(spec, spec),
        compiler_params=_params(("parallel", "parallel", "parallel")),
        name="dilated_attention",
    )(q, k, v)


def _merge_body(x_ref, g_ref, oa_ref, o0_ref, o1_ref, o2_ref, l0_ref, l1_ref, l2_ref, ga_ref, gb_ref,
                wa_ref, wb_ref, wo_ref, out_ref, *, gi):
    l0, l1, l2 = l0_ref[0], l1_ref[0], l2_ref[0]
    m = jnp.maximum(jnp.maximum(l0, l1), l2)
    e0, e1, e2 = jnp.exp(l0 - m), jnp.exp(l1 - m), jnp.exp(l2 - m)
    o_b = (e0 * o0_ref[0] + e1 * o1_ref[0] + e2 * o2_ref[0]) / (e0 + e1 + e2)
    y_a = jnp.dot(oa_ref[0], wa_ref[...], preferred_element_type=F32)
    y_b = jnp.dot(o_b.astype(BF16), wb_ref[...], preferred_element_type=F32)
    merged = ga_ref[0].astype(F32) * y_a + gb_ref[0].astype(F32) * y_b
    y = jnp.dot(merged.astype(BF16), wo_ref[...], preferred_element_type=F32)
    out_ref[0] = x_ref[0] + _rms(y, g_ref[gi:gi + 1, :])


def _merge(x, g, o_a, o_g, lse_g, z, w_br_a, w_br_b, w_out, gi):
    B, S, D = x.shape
    tm = _tile(S, 256)
    Wa, Wb = o_a.shape[-1], o_g[0].shape[-1]
    gblk = 0
    row = lambda w: pl.BlockSpec((1, tm, w), lambda b, i: (b, i, 0))
    full = lambda a: pl.BlockSpec(a.shape, lambda b, i: (0, 0))
    return pl.pallas_call(
        functools.partial(_merge_body, gi=gi),
        out_shape=jax.ShapeDtypeStruct((B, S, D), F32),
        grid=(B, S // tm),
        in_specs=[row(D), full(g), row(Wa), row(Wb), row(Wb), row(Wb), row(Wb), row(Wb), row(Wb),
                  pl.BlockSpec((1, tm, D), lambda b, i: (b, i, gblk)),
                  pl.BlockSpec((1, tm, D), lambda b, i: (b, i, gblk + 1)),
                  full(w_br_a), full(w_br_b), full(w_out)],
        out_specs=row(D),
        compiler_params=_params(("parallel", "parallel")),
        name="mixer_merge",
    )(x, g, o_a, *o_g, *lse_g, z, z, w_br_a, w_br_b, w_out)


def _ple_body(x_ref, g_ref, p_ref, wp_ref, wg_ref, out_ref, *, gi):
    x = x_ref[0]
    e = jnp.dot(p_ref[0, 0].astype(BF16), wp_ref[...], preferred_element_type=F32)
    u = _rms(x, g_ref[gi:gi + 1, :]).astype(BF16)
    gate = jax.nn.sigmoid(jnp.dot(u, wg_ref[...], preferred_element_type=F32))
    out_ref[0] = x + _rms(gate * e, g_ref[gi + 1:gi + 2, :])


def _ple(x, g, p, layer, w_proj, w_gate, gi):
    B, S, D = x.shape
    P = p.shape[-1]
    tm = _tile(S, 512)
    full = lambda a: pl.BlockSpec(a.shape, lambda b, i: (0, 0))
    return pl.pallas_call(
        functools.partial(_ple_body, gi=gi),
        out_shape=jax.ShapeDtypeStruct((B, S, D), F32),
        grid=(B, S // tm),
        in_specs=[pl.BlockSpec((1, tm, D), lambda b, i: (b, i, 0)), full(g),
                  pl.BlockSpec((1, 1, tm, P), lambda b, i: (layer, b, i, 0)),
                  full(w_proj), full(w_gate)],
        out_specs=pl.BlockSpec((1, tm, D), lambda b, i: (b, i, 0)),
        compiler_params=_params(("parallel", "parallel")),
        name="ple_gate",
    )(x, g, p, w_proj, w_gate)


def _rope_tables(S):
    half = HEAD_DIM // 2
    inv = ROPE_THETA ** (-jnp.arange(half, dtype=F32) * 2.0 / HEAD_DIM)
    ang = jnp.arange(S, dtype=F32)[:, None] * inv[None, :]
    cos, sin = jnp.cos(ang), jnp.sin(ang)
    return jnp.concatenate([cos, cos], axis=1), jnp.concatenate([-sin, sin], axis=1)


def _mixer(x, g, w_in, b_f, w_br_a, w_br_b, w_out, cos, sin):
    B, S, D = x.shape
    fox_w = w_br_a.shape[0]
    dil_out_w = w_br_b.shape[0]
    n_groups = len(DIL_PATTERNS)
    dil_w = n_groups * dil_out_w
    n_fox = b_f.shape[0]
    assert fox_w == n_fox * HEAD_DIM and n_fox <= HEAD_DIM
    assert w_in.shape[1] == 3 * fox_w + n_fox + 3 * dil_w + 2 * D

    f_lo = 3 * fox_w
    g_lo = f_lo + n_fox + 3 * dil_w
    w_main = jnp.concatenate([w_in[:, g_lo:], w_in[:, :f_lo], w_in[:, f_lo + n_fox:g_lo]],
                             axis=1).astype(BF16)
    w_f = jnp.pad(w_in[:, f_lo:f_lo + n_fox], ((0, 0), (0, HEAD_DIM - n_fox))).astype(BF16)

    z, ft = _proj(x, g, w_main, w_f, cos, sin, 2, fox_w, dil_w, n_fox)
    c = _forget_cumsum(ft, b_f)
    o_a = _fox(z, c.reshape(B, n_fox, 1, S), n_fox, 2 * D)

    qb0 = 2 * D + 3 * fox_w
    o_g, lse_g = [], []
    for gidx, (window, d) in enumerate(DIL_PATTERNS):
        assert window // d == DIL_BLOCK and S % (d * DIL_BLOCK) == 0
        L = S // d

        def residue_major(col0):
            t = z[:, :, col0 + gidx * dil_out_w: col0 + (gidx + 1) * dil_out_w]
            return t.reshape(B, L, d, dil_out_w).transpose(0, 2, 1, 3)

        o, lse = _dilated_group(residue_major(qb0), residue_major(qb0 + dil_w),
                                residue_major(qb0 + 2 * dil_w))
        token_major = lambda t: t.transpose(0, 2, 1, 3).reshape(B, S, dil_out_w)
        o_g.append(token_major(o))
        lse_g.append(token_major(lse))

    return _merge(x, g, o_a, o_g, lse_g, z, w_br_a.astype(BF16), w_br_b.astype(BF16),
                  w_out.astype(BF16), 3)


def kernel(x, p, norm_g, ffn1_w_in, ffn1_w_out, mix_w_in, fox_b_f, mix_w_br_a, mix_w_br_b, mix_w_out,
           ffn2_w_in, ffn2_w_out, ple_w_proj, ple_w_gate):
    B, S, D = x.shape
    cos, sin = _rope_tables(S)
    for i in range(norm_g.shape[0]):
        g = norm_g[i]
        x = _ffn(x.reshape(B * S, D), g, ffn1_w_in[i].astype(BF16), ffn1_w_out[i].astype(BF16), 0)
        x = _mixer(x.reshape(B, S, D), g, mix_w_in[i], fox_b_f[i], mix_w_br_a[i], mix_w_br_b[i],
                   mix_w_out[i], cos, sin)
        x = _ffn(x.reshape(B * S, D), g, ffn2_w_in[i].astype(BF16), ffn2_w_out[i].astype(BF16), 4)
        x = _ple(x.reshape(B, S, D), g, p, i, ple_w_proj[i].astype(BF16), ple_w_gate[i].astype(BF16), 6)
    return x
```

```python
import functools
import math

import jax
import jax.numpy as jnp
from jax import lax
from jax.experimental import pallas as pl
from jax.experimental.pallas import tpu as pltpu

HEAD_DIM = 128
DIL_PATTERNS = ((128, 1), (512, 4), (2048, 16))
DIL_BLOCK = 128
DIL_SPAN = 2048
DIL_UNROLL = 4
ROPE_THETA = 10000.0
RMS_EPS = 1e-6
MASK_VALUE = -1e30
LOG2E = math.log2(math.e)
N_PIECES = 3
VMEM_LIMIT_V7X = 56 * 1024 * 1024

F32 = jnp.float32
BF16 = jnp.bfloat16


def _rms(x, g):
    return x * lax.rsqrt(jnp.mean(x * x, axis=-1, keepdims=True) + RMS_EPS) * g


def _tile(n, want):
    t = min(n, want)
    assert n % t == 0, (n, t)
    return t


def _params(sem, vmem=VMEM_LIMIT_V7X):
    return pltpu.CompilerParams(dimension_semantics=sem, vmem_limit_bytes=vmem)


def _split3(v):
    hi = v.astype(BF16)
    r = v - hi.astype(F32)
    mid = r.astype(BF16)
    lo = (r - mid.astype(F32)).astype(BF16)
    return hi, mid, lo


def _norm_body(x_ref, g_ref, u_ref, *, gi):
    u_ref[...] = _rms(x_ref[...], g_ref[gi:gi + 1, :]).astype(BF16)


def _norm(x2, g, gi):
    T, D = x2.shape
    tm = _tile(T, 1024)
    return pl.pallas_call(
        functools.partial(_norm_body, gi=gi),
        out_shape=jax.ShapeDtypeStruct((T, D), BF16),
        grid=(T // tm,),
        in_specs=[pl.BlockSpec((tm, D), lambda i: (i, 0)), pl.BlockSpec(g.shape, lambda i: (0, 0))],
        out_specs=pl.BlockSpec((tm, D), lambda i: (i, 0)),
        compiler_params=_params(("parallel",)),
        name="first_norm",
    )(x2, g)


def _ffn_body(u_ref, x_ref, g_ref, gn_ref, wa_ref, wb_ref, wo_ref, o_ref, un_ref, acc_ref, *, gi, gni, nj):
    j = pl.program_id(1)

    @pl.when(j == 0)
    def _():
        acc_ref[...] = jnp.zeros(acc_ref.shape, F32)

    u = u_ref[...]
    a = jnp.dot(u, wa_ref[...], preferred_element_type=F32)
    b = jnp.dot(u, wb_ref[...], preferred_element_type=F32)
    h = (a * jax.nn.sigmoid(a) * b).astype(BF16)
    acc_ref[...] += jnp.dot(h, wo_ref[...], preferred_element_type=F32)

    @pl.when(j == nj - 1)
    def _():
        x = x_ref[...] + 0.5 * _rms(acc_ref[...], g_ref[gi:gi + 1, :])
        o_ref[...] = x
        un_ref[...] = _rms(x, gn_ref[gni:gni + 1, :]).astype(BF16)


def _ffn(u2, x2, g, gn, w_in, w_out, gi, gni):
    T, D = x2.shape
    F = w_out.shape[0]
    tm = _tile(T, 512)
    tf = _tile(F, 512)
    nj = F // tf
    row = pl.BlockSpec((tm, D), lambda i, j: (i, 0))
    return pl.pallas_call(
        functools.partial(_ffn_body, gi=gi, gni=gni, nj=nj),
        out_shape=(jax.ShapeDtypeStruct((T, D), F32), jax.ShapeDtypeStruct((T, D), BF16)),
        grid=(T // tm, nj),
        in_specs=[
            row, row,
            pl.BlockSpec(g.shape, lambda i, j: (0, 0)),
            pl.BlockSpec(gn.shape, lambda i, j: (0, 0)),
            pl.BlockSpec((D, tf), lambda i, j: (0, j)),
            pl.BlockSpec((D, tf), lambda i, j: (0, j + nj)),
            pl.BlockSpec((tf, D), lambda i, j: (j, 0)),
        ],
        out_specs=(row, row),
        scratch_shapes=[pltpu.VMEM((tm, D), F32)],
        compiler_params=_params(("parallel", "arbitrary")),
        name="ffn",
    )(u2, x2, g, gn, w_in, w_in, w_out)


def _fox_proj_body(u_ref, wq_ref, wk_ref, wv_ref, qt_ref, k_ref, vt_ref, *, heads_per_chunk):
    u = u_ref[0]

    def transposed(t, out_ref):
        for c in range(heads_per_chunk):
            out_ref[0, c] = jnp.transpose(t[:, c * HEAD_DIM:(c + 1) * HEAD_DIM]).astype(BF16)

    q = jnp.dot(u, wq_ref[...], preferred_element_type=F32)
    transposed(q * (HEAD_DIM ** -0.5 * LOG2E), qt_ref)
    k_ref[0] = jnp.dot(u, wk_ref[...], preferred_element_type=F32).astype(BF16)
    transposed(jnp.dot(u, wv_ref[...], preferred_element_type=F32), vt_ref)


def _fox_proj(u, w_qkv, n_heads):
    B, S, D = u.shape
    fox_w = n_heads * HEAD_DIM
    tm = _tile(S, 1024)
    tn = _tile(fox_w, 512)
    npk = fox_w // tn
    hpc = tn // HEAD_DIM
    t_spec = pl.BlockSpec((1, hpc, HEAD_DIM, tm), lambda b, i, j: (b, j, 0, i))
    t_shape = jax.ShapeDtypeStruct((B, n_heads, HEAD_DIM, S), BF16)
    return pl.pallas_call(
        functools.partial(_fox_proj_body, heads_per_chunk=hpc),
        out_shape=(t_shape, jax.ShapeDtypeStruct((B, S, fox_w), BF16), t_shape),
        grid=(B, S // tm, npk),
        in_specs=[pl.BlockSpec((1, tm, D), lambda b, i, j: (b, i, 0)),
                  pl.BlockSpec((D, tn), lambda b, i, j: (0, j)),
                  pl.BlockSpec((D, tn), lambda b, i, j: (0, j + npk)),
                  pl.BlockSpec((D, tn), lambda b, i, j: (0, j + 2 * npk))],
        out_specs=(t_spec, pl.BlockSpec((1, tm, tn), lambda b, i, j: (b, i, j)), t_spec),
        compiler_params=_params(("parallel", "parallel", "arbitrary")),
        name="fox_proj",
    )(u, w_qkv, w_qkv, w_qkv)


def _rope(t, cos, sin):
    return t * cos + pltpu.roll(t, HEAD_DIM // 2, 1) * sin


def _dil_proj_body(u_ref, wq_ref, wk_ref, wv_ref, cos_ref, sin_ref, q_ref, k_ref, v_ref, *scr, d, tm, tn):
    u = u_ref[0]
    heads = tn // HEAD_DIM
    for kind, (w_ref, out_ref) in enumerate(((wq_ref, q_ref), (wk_ref, k_ref), (wv_ref, v_ref))):
        t = jnp.dot(u, w_ref[...], preferred_element_type=F32)
        for c in range(heads):
            cols = slice(c * HEAD_DIM, (c + 1) * HEAD_DIM)
            sl = t[:, cols]
            if kind < 2:
                sl = _rope(sl, cos_ref[...], sin_ref[...])
            if kind == 0:
                sl = sl * HEAD_DIM ** -0.5
            if d == 1:
                out_ref[0, 0, :, cols] = sl.astype(BF16)
            else:
                slot = kind * heads + c
                scr[0][slot] = sl
                for r in range(d):
                    out_ref[0, r, :, cols] = scr[0][slot, pl.ds(r, tm // d, stride=d), :].astype(BF16)


def _dil_proj(u, w_qkv, cos, sin, group, d, n_groups, dil_out_w):
    B, S, D = u.shape
    tm = _tile(S, 1024)
    tn = dil_out_w
    assert tm % (d * 16) == 0
    shape = jax.ShapeDtypeStruct((B, d, S // d, tn), BF16)
    out_spec = pl.BlockSpec((1, d, tm // d, tn), lambda b, i: (b, 0, i, 0))
    w_spec = lambda kind: pl.BlockSpec((D, tn), lambda b, i: (0, kind * n_groups + group))
    return pl.pallas_call(
        functools.partial(_dil_proj_body, d=d, tm=tm, tn=tn),
        out_shape=(shape, shape, shape),
        grid=(B, S // tm),
        in_specs=[pl.BlockSpec((1, tm, D), lambda b, i: (b, i, 0)), w_spec(0), w_spec(1), w_spec(2),
                  pl.BlockSpec((tm, HEAD_DIM), lambda b, i: (i, 0)),
                  pl.BlockSpec((tm, HEAD_DIM), lambda b, i: (i, 0))],
        out_specs=(out_spec, out_spec, out_spec),
        scratch_shapes=[] if d == 1 else [pltpu.VMEM((3 * tn // HEAD_DIM, tm, HEAD_DIM), F32)],
        compiler_params=_params(("parallel", "parallel")),
        name="dil_proj",
    )(u, w_qkv, w_qkv, w_qkv, cos, sin)


def _gate_proj_body(u_ref, w_ref, o_ref):
    acc = jnp.dot(u_ref[0], w_ref[...], preferred_element_type=F32)
    o_ref[0] = jax.nn.sigmoid(acc).astype(BF16)


def _gate_proj(u, w_g):
    B, S, D = u.shape
    N = w_g.shape[1]
    tm = _tile(S, 1024)
    tn = _tile(N, 1024)
    return pl.pallas_call(
        _gate_proj_body,
        out_shape=jax.ShapeDtypeStruct((B, S, N), BF16),
        grid=(B, S // tm, N // tn),
        in_specs=[pl.BlockSpec((1, tm, D), lambda b, i, j: (b, i, 0)),
                  pl.BlockSpec((D, tn), lambda b, i, j: (0, j))],
        out_specs=pl.BlockSpec((1, tm, tn), lambda b, i, j: (b, i, j)),
        compiler_params=_params(("parallel", "parallel", "arbitrary")),
        name="gate_proj",
    )(u, w_g)


def _scan_body(u_ref, wf_ref, b_ref, e_ref, carry_ref, *, chunk, n_heads):
    tm = u_ref.shape[1]

    @pl.when(pl.program_id(1) == 0)
    def _():
        carry_ref[...] = jnp.zeros(carry_ref.shape, F32)

    row = lax.broadcasted_iota(jnp.int32, (chunk, chunk), 0)
    col = lax.broadcasted_iota(jnp.int32, (chunk, chunk), 1)
    tri = jnp.where(col <= row, 1.0, 0.0).astype(BF16)
    pr = lax.broadcasted_iota(jnp.int32, (HEAD_DIM, HEAD_DIM), 0)
    pc = lax.broadcasted_iota(jnp.int32, (HEAD_DIM, HEAD_DIM), 1)
    place = [jnp.where((pc == N_PIECES * pr + k) & (pr < n_heads), 1.0, 0.0).astype(BF16)
             for k in range(N_PIECES)]

    t = jnp.dot(u_ref[0], wf_ref[...], preferred_element_type=F32) + b_ref[...]
    ls = jnp.minimum(t, 0.0) - jnp.log1p(jnp.exp(-jnp.abs(t)))
    carry = carry_ref[...]
    for c in range(tm // chunk):
        cs = carry
        for piece in _split3(ls[c * chunk:(c + 1) * chunk]):
            cs = cs + jnp.dot(tri, piece, preferred_element_type=F32)
        e = jnp.zeros((chunk, HEAD_DIM), F32)
        for piece, pmat in zip(_split3(cs * (-LOG2E)), place):
            e = e + jnp.dot(piece, pmat, preferred_element_type=F32)
        e_ref[0, c * chunk:(c + 1) * chunk, :] = e.astype(BF16)
        carry = cs[chunk - 1:chunk, :]
    carry_ref[...] = carry


def _forget_scan(u, w_f, b_f):
    B, S, D = u.shape
    H = b_f.shape[0]
    assert N_PIECES * H <= HEAD_DIM
    tm = _tile(S, 1024)
    chunk = _tile(tm, 256)
    return pl.pallas_call(
        functools.partial(_scan_body, chunk=chunk, n_heads=H),
        out_shape=jax.ShapeDtypeStruct((B, S, HEAD_DIM), BF16),
        grid=(B, S // tm),
        in_specs=[pl.BlockSpec((1, tm, D), lambda b, i: (b, i, 0)),
                  pl.BlockSpec(w_f.shape, lambda b, i: (0, 0)),
                  pl.BlockSpec((1, HEAD_DIM), lambda b, i: (0, 0))],
        out_specs=pl.BlockSpec((1, tm, HEAD_DIM), lambda b, i: (b, i, 0)),
        scratch_shapes=[pltpu.VMEM((1, HEAD_DIM), F32)],
        compiler_params=_params(("parallel", "arbitrary")),
        name="forget_scan",
    )(u, w_f, jnp.pad(b_f, (0, HEAD_DIM - H)).reshape(1, HEAD_DIM))


def _fox_body(qt_ref, k_ref, e_ref, vt_ref, o_ref, m_ref, l_ref, acc_ref, *, tq, hg):
    h0 = pl.program_id(1) * hg
    qi = pl.program_id(2)
    r = lax.broadcasted_iota(jnp.int32, (HEAD_DIM, tq), 0)
    qts = []
    for hh in range(hg):
        lo = N_PIECES * (h0 + hh)
        sel = jnp.where((r >= lo) & (r < lo + N_PIECES), 1.0, 0.0).astype(BF16)
        qts.append(jnp.concatenate([qt_ref[0, hh], sel], axis=0))

    m_ref[...] = jnp.full(m_ref.shape, MASK_VALUE, F32)
    l_ref[...] = jnp.zeros(l_ref.shape, F32)
    acc_ref[...] = jnp.zeros(acc_ref.shape, F32)

    def block(k0, tk, mask):
        e = e_ref[0, pl.ds(k0, tk), :]
        ss = []
        for hh in range(hg):
            cols = slice(hh * HEAD_DIM, (hh + 1) * HEAD_DIM)
            kk = jnp.concatenate([k_ref[0, pl.ds(k0, tk), cols], e], axis=1)
            ss.append(jnp.dot(kk, qts[hh], preferred_element_type=F32))
        ps, alphas = [], []
        for hh in range(hg):
            s = ss[hh] if mask is None else jnp.where(mask, ss[hh], MASK_VALUE)
            m_prev = m_ref[hh]
            m_new = jnp.maximum(m_prev, jnp.max(s, axis=0, keepdims=True))
            alpha = jnp.exp2(m_prev - m_new)
            p = jnp.exp2(s - m_new)
            l_ref[hh] = alpha * l_ref[hh] + jnp.sum(p, axis=0, keepdims=True)
            m_ref[hh] = m_new
            ps.append(p.astype(BF16))
            alphas.append(alpha)
        for hh in range(hg):
            pv = jnp.dot(vt_ref[0, hh, :, pl.ds(k0, tk)], ps[hh], preferred_element_type=F32)
            acc_ref[hh] = alphas[hh] * acc_ref[hh] + pv

    def full_pair(j, carry):
        block(pl.multiple_of(j * 2 * tq, 2 * tq), 2 * tq, None)
        return carry

    lax.fori_loop(0, lax.div(qi, 2), full_pair, 0)
    q0 = pl.multiple_of(qi * tq, tq)

    @pl.when(lax.rem(qi, 2) == 0)
    def _():
        key = lax.broadcasted_iota(jnp.int32, (tq, tq), 0)
        qry = lax.broadcasted_iota(jnp.int32, (tq, tq), 1)
        block(q0, tq, key <= qry)

    @pl.when(lax.rem(qi, 2) == 1)
    def _():
        key = lax.broadcasted_iota(jnp.int32, (2 * tq, tq), 0)
        qry = lax.broadcasted_iota(jnp.int32, (2 * tq, tq), 1)
        block(pl.multiple_of(q0 - tq, tq), 2 * tq, key <= qry + tq)

    for hh in range(hg):
        o = acc_ref[hh] / l_ref[hh]
        o_ref[0, :, hh * HEAD_DIM:(hh + 1) * HEAD_DIM] = jnp.transpose(o).astype(o_ref.dtype)


def _fox(qt, k, e, vt):
    B, H, _, S = qt.shape
    tq = _tile(S, 256)
    hg = _tile(H, 4)
    return pl.pallas_call(
        functools.partial(_fox_body, tq=tq, hg=hg),
        out_shape=jax.ShapeDtypeStruct((B, S, H * HEAD_DIM), BF16),
        grid=(B, H // hg, S // tq),
        in_specs=[
            pl.BlockSpec((1, hg, HEAD_DIM, tq), lambda b, h, i: (b, h, 0, i)),
            pl.BlockSpec((1, S, hg * HEAD_DIM), lambda b, h, i: (b, 0, h)),
            pl.BlockSpec((1, S, HEAD_DIM), lambda b, h, i: (b, 0, 0)),
            pl.BlockSpec((1, hg, HEAD_DIM, S), lambda b, h, i: (b, h, 0, 0)),
        ],
        out_specs=pl.BlockSpec((1, tq, hg * HEAD_DIM), lambda b, h, i: (b, i, h)),
        scratch_shapes=[pltpu.VMEM((hg, 1, tq), F32), pltpu.VMEM((hg, 1, tq), F32),
                        pltpu.VMEM((hg, HEAD_DIM, tq), F32)],
        compiler_params=_params(("parallel", "parallel", "arbitrary")),
        name="fox_attention",
    )(qt, k, e, vt)


def _dil_body(*refs, dils, span):
    ng = len(dils)
    q_refs, k_refs, v_refs = refs[:ng], refs[ng:2 * ng], refs[2 * ng:3 * ng]
    o_ref, o_scr, lse_scr = refs[3 * ng:]
    nk = DIL_BLOCK
    i = pl.program_id(2)
    qi = lax.broadcasted_iota(jnp.int32, (nk, 2 * nk), 0)
    ki = lax.broadcasted_iota(jnp.int32, (nk, 2 * nk), 1)
    band = (ki >= qi) & (ki <= qi + nk)

    for g, d in enumerate(dils):
        rows = span // d
        nblk = rows // nk
        q_ref, k_ref, v_ref = q_refs[g], k_refs[g], v_refs[g]

        def several(it, carry, g=g, d=d, rows=rows, nblk=nblk, q_ref=q_ref, k_ref=k_ref, v_ref=v_ref):
            where, ss = [], []
            for j in range(DIL_UNROLL):
                idx = it * DIL_UNROLL + j
                r = lax.div(idx, nblk)
                n = lax.rem(idx, nblk)
                row0 = i * rows + n * nk
                prev0 = pl.multiple_of(jnp.maximum(row0 - nk, 0), nk)
                cur0 = pl.multiple_of(row0, nk)
                q = q_ref[0, r, pl.ds(pl.multiple_of(n * nk, nk), nk), :]
                kk = jnp.concatenate([k_ref[0, r, pl.ds(prev0, nk), :], k_ref[0, r, pl.ds(cur0, nk), :]],
                                     axis=0)
                s = lax.dot_general(q, kk, (((1,), (1,)), ((), ())), preferred_element_type=F32)
                ss.append(jnp.where(band & ((ki >= nk) | (row0 > 0)), s, MASK_VALUE))
                where.append((r, n, prev0, cur0))
            ps, ls, lses = [], [], []
            for s in ss:
                m = jnp.max(s, axis=1, keepdims=True)
                p = jnp.exp(s - m)
                l = jnp.sum(p, axis=1, keepdims=True)
                ps.append(p.astype(BF16))
                ls.append(l)
                lses.append(jnp.broadcast_to(m + jnp.log(l), (nk, HEAD_DIM)))
            for (r, n, prev0, cur0), p, l, lse in zip(where, ps, ls, lses):
                vv = jnp.concatenate([v_ref[0, r, pl.ds(prev0, nk), :], v_ref[0, r, pl.ds(cur0, nk), :]],
                                     axis=0)
                o = jnp.dot(p, vv, preferred_element_type=F32) / l
                tok0 = n * (nk * d) + r
                if d == 1:
                    o_scr[g, pl.ds(pl.multiple_of(tok0, nk), nk), :] = o
                    lse_scr[g, pl.ds(pl.multiple_of(tok0, nk), nk), :] = lse
                else:
                    o_scr[g, pl.ds(tok0, nk, stride=d), :] = o
                    lse_scr[g, pl.ds(tok0, nk, stride=d), :] = lse
            return carry

        assert (d * nblk) % DIL_UNROLL == 0
        lax.fori_loop(0, d * nblk // DIL_UNROLL, several, 0)

    lse = [lse_scr[g] for g in range(ng)]
    top = functools.reduce(jnp.maximum, lse)
    w = [jnp.exp(v - top) for v in lse]
    den = functools.reduce(lambda a, b: a + b, w)
    num = functools.reduce(lambda a, b: a + b, [w[g] * o_scr[g] for g in range(ng)])
    o_ref[0] = (num / den).astype(o_ref.dtype)


def _dilated(qs, ks, vs):
    B, d0, L0, W = qs[0].shape
    S = d0 * L0
    dils = tuple(t.shape[1] for t in qs)
    span = _tile(S, DIL_SPAN)
    assert all(span % (d * DIL_BLOCK) == 0 for d in dils)
    q_specs = [pl.BlockSpec((1, d, span // d, HEAD_DIM), lambda b, h, i: (b, 0, i, h)) for d in dils]
    kv_specs = [pl.BlockSpec((1, d, S // d, HEAD_DIM), lambda b, h, i: (b, 0, 0, h)) for d in dils]
    ng = len(dils)
    return pl.pallas_call(
        functools.partial(_dil_body, dils=dils, span=span),
        out_shape=jax.ShapeDtypeStruct((B, S, W), BF16),
        grid=(B, W // HEAD_DIM, S // span),
        in_specs=q_specs + kv_specs + kv_specs,
        out_specs=pl.BlockSpec((1, span, HEAD_DIM), lambda b, h, i: (b, i, h)),
        scratch_shapes=[pltpu.VMEM((ng, span, HEAD_DIM), F32), pltpu.VMEM((ng, span, HEAD_DIM), F32)],
        compiler_params=_params(("parallel", "parallel", "arbitrary")),
        name="dilated_attention",
    )(*qs, *ks, *vs)


def _merge_body(x_ref, g_ref, oa_ref, ob_ref, ga_ref, gb_ref, wa_ref, wb_ref, wo_ref, out_ref, un_ref, *,
                gi, gni):
    y_a = jnp.dot(oa_ref[0], wa_ref[...], preferred_element_type=F32)
    y_b = jnp.dot(ob_ref[0], wb_ref[...], preferred_element_type=F32)
    merged = ga_ref[0].astype(F32) * y_a + gb_ref[0].astype(F32) * y_b
    y = jnp.dot(merged.astype(BF16), wo_ref[...], preferred_element_type=F32)
    x = x_ref[0] + _rms(y, g_ref[gi:gi + 1, :])
    out_ref[0] = x
    un_ref[0] = _rms(x, g_ref[gni:gni + 1, :]).astype(BF16)


def _merge(x, g, o_a, o_b, gates, w_br_a, w_br_b, w_out, gi, gni):
    B, S, D = x.shape
    tm = _tile(S, 256)
    row = lambda w, c=0: pl.BlockSpec((1, tm, w), lambda b, i: (b, i, c))
    full = lambda a: pl.BlockSpec(a.shape, lambda b, i: (0, 0))
    return pl.pallas_call(
        functools.partial(_merge_body, gi=gi, gni=gni),
        out_shape=(jax.ShapeDtypeStruct((B, S, D), F32), jax.ShapeDtypeStruct((B, S, D), BF16)),
        grid=(B, S // tm),
        in_specs=[row(D), full(g), row(o_a.shape[-1]), row(o_b.shape[-1]), row(D, 0), row(D, 1),
                  full(w_br_a), full(w_br_b), full(w_out)],
        out_specs=(row(D), row(D)),
        compiler_params=_params(("parallel", "parallel")),
        name="mixer_merge",
    )(x, g, o_a, o_b, gates, gates, w_br_a, w_br_b, w_out)


def _ple_body(x_ref, u_ref, g_ref, gn_ref, p_ref, wp_ref, wg_ref, out_ref, un_ref, *, gi, gni):
    e = jnp.dot(p_ref[0, 0].astype(BF16), wp_ref[...], preferred_element_type=F32)
    gate = jax.nn.sigmoid(jnp.dot(u_ref[0], wg_ref[...], preferred_element_type=F32))
    x = x_ref[0] + _rms(gate * e, g_ref[gi:gi + 1, :])
    out_ref[0] = x
    un_ref[0] = _rms(x, gn_ref[gni:gni + 1, :]).astype(BF16)


def _ple(x, u, g, gn, p, layer, w_proj, w_gate, gi, gni):
    B, S, D = x.shape
    P = p.shape[-1]
    tm = _tile(S, 512)
    full = lambda a: pl.BlockSpec(a.shape, lambda b, i: (0, 0))
    row = pl.BlockSpec((1, tm, D), lambda b, i: (b, i, 0))
    return pl.pallas_call(
        functools.partial(_ple_body, gi=gi, gni=gni),
        out_shape=(jax.ShapeDtypeStruct((B, S, D), F32), jax.ShapeDtypeStruct((B, S, D), BF16)),
        grid=(B, S // tm),
        in_specs=[row, row, full(g), full(gn),
                  pl.BlockSpec((1, 1, tm, P), lambda b, i: (layer, b, i, 0)),
                  full(w_proj), full(w_gate)],
        out_specs=(row, row),
        compiler_params=_params(("parallel", "parallel")),
        name="ple_gate",
    )(x, u, g, gn, p, w_proj, w_gate)


def _rope_tables(S):
    half = HEAD_DIM // 2
    inv = ROPE_THETA ** (-jnp.arange(half, dtype=F32) * 2.0 / HEAD_DIM)
    ang = jnp.arange(S, dtype=F32)[:, None] * inv[None, :]
    cos, sin = jnp.cos(ang), jnp.sin(ang)
    return jnp.concatenate([cos, cos], axis=1), jnp.concatenate([-sin, sin], axis=1)


def _mixer(x, u, g, w_in, b_f, w_br_a, w_br_b, w_out, cos, sin):
    B, S, D = x.shape
    fox_w = w_br_a.shape[0]
    dil_out_w = w_br_b.shape[0]
    n_groups = len(DIL_PATTERNS)
    dil_w = n_groups * dil_out_w
    n_fox = b_f.shape[0]
    assert fox_w == n_fox * HEAD_DIM
    assert w_in.shape[1] == 3 * fox_w + n_fox + 3 * dil_w + 2 * D
    for window, d in DIL_PATTERNS:
        assert window // d == DIL_BLOCK

    f_lo = 3 * fox_w
    b_lo = f_lo + n_fox
    g_lo = b_lo + 3 * dil_w
    w_fox = w_in[:, :f_lo].astype(BF16)
    w_f = jnp.pad(w_in[:, f_lo:b_lo], ((0, 0), (0, HEAD_DIM - n_fox))).astype(BF16)
    w_dil = w_in[:, b_lo:g_lo].astype(BF16)
    w_gate = w_in[:, g_lo:].astype(BF16)

    qt, k, vt = _fox_proj(u, w_fox, n_fox)
    e = _forget_scan(u, w_f, b_f)
    o_a = _fox(qt, k, e, vt)

    qs, ks, vs = [], [], []
    for gidx, (_, d) in enumerate(DIL_PATTERNS):
        q, kk, v = _dil_proj(u, w_dil, cos, sin, gidx, d, n_groups, dil_out_w)
        qs.append(q), ks.append(kk), vs.append(v)
    o_b = _dilated(qs, ks, vs)

    gates = _gate_proj(u, w_gate)
    return _merge(x, g, o_a, o_b, gates, w_br_a.astype(BF16), w_br_b.astype(BF16), w_out.astype(BF16), 3, 4)


def kernel(x, p, norm_g, ffn1_w_in, ffn1_w_out, mix_w_in, fox_b_f, mix_w_br_a, mix_w_br_b, mix_w_out,
           ffn2_w_in, ffn2_w_out, ple_w_proj, ple_w_gate):
    B, S, D = x.shape
    T = B * S
    depth = norm_g.shape[0]
    cos, sin = _rope_tables(S)
    x = x.reshape(T, D)
    u = _norm(x, norm_g[0], 0)
    for i in range(depth):
        g = norm_g[i]
        gn = norm_g[(i + 1) % depth]
        x, u = _ffn(u, x, g, g, ffn1_w_in[i].astype(BF16), ffn1_w_out[i].astype(BF16), 1, 2)
        x, u = _mixer(x.reshape(B, S, D), u.reshape(B, S, D), g, mix_w_in[i], fox_b_f[i], mix_w_br_a[i],
                      mix_w_br_b[i], mix_w_out[i], cos, sin)
        x, u = _ffn(u.reshape(T, D), x.reshape(T, D), g, g, ffn2_w_in[i].astype(BF16),
                    ffn2_w_out[i].astype(BF16), 5, 6)
        x, u = _ple(x.reshape(B, S, D), u.reshape(B, S, D), g, gn, p, i, ple_w_proj[i].astype(BF16),
                    ple_w_gate[i].astype(BF16), 7, 0)
        x, u = x.reshape(T, D), u.reshape(T, D)
    return x.reshape(B, S, D)
```

```python
import functools
import math

import jax
import jax.numpy as jnp
from jax import lax
from jax.experimental import pallas as pl
from jax.experimental.pallas import tpu as pltpu

HEAD_DIM = 128
DIL_PATTERNS = ((128, 1), (512, 4), (2048, 16))
DIL_BLOCK = 128
DIL_SPAN = 2048
DIL_UNROLL = 8
ROPE_THETA = 10000.0
RMS_EPS = 1e-6
MASK_VALUE = -1e30
LOG2E = math.log2(math.e)
N_PIECES = 3
VMEM_LIMIT_V7X = 56 * 1024 * 1024

F32 = jnp.float32
BF16 = jnp.bfloat16


def _rms(x, g):
    return x * lax.rsqrt(jnp.mean(x * x, axis=-1, keepdims=True) + RMS_EPS) * g


def _tile(n, want):
    t = min(n, want)
    assert n % t == 0, (n, t)
    return t


def _params(sem, vmem=VMEM_LIMIT_V7X):
    return pltpu.CompilerParams(dimension_semantics=sem, vmem_limit_bytes=vmem)


def _split3(v):
    hi = v.astype(BF16)
    r = v - hi.astype(F32)
    mid = r.astype(BF16)
    lo = (r - mid.astype(F32)).astype(BF16)
    return hi, mid, lo


def _norm_body(x_ref, g_ref, u_ref, *, gi):
    u_ref[...] = _rms(x_ref[...], g_ref[gi:gi + 1, :]).astype(BF16)


def _norm(x2, g, gi):
    T, D = x2.shape
    tm = _tile(T, 1024)
    return pl.pallas_call(
        functools.partial(_norm_body, gi=gi),
        out_shape=jax.ShapeDtypeStruct((T, D), BF16),
        grid=(T // tm,),
        in_specs=[pl.BlockSpec((tm, D), lambda i: (i, 0)), pl.BlockSpec(g.shape, lambda i: (0, 0))],
        out_specs=pl.BlockSpec((tm, D), lambda i: (i, 0)),
        compiler_params=_params(("parallel",)),
        name="first_norm",
    )(x2, g)


def _ffn_body(u_ref, x_ref, g_ref, gn_ref, wa_ref, wb_ref, wo_ref, o_ref, un_ref, acc_ref, *, gi, gni, nj):
    j = pl.program_id(1)

    @pl.when(j == 0)
    def _():
        acc_ref[...] = jnp.zeros(acc_ref.shape, F32)

    u = u_ref[...]
    a = jnp.dot(u, wa_ref[...], preferred_element_type=F32)
    b = jnp.dot(u, wb_ref[...], preferred_element_type=F32)
    h = (a * jax.nn.sigmoid(a) * b).astype(BF16)
    acc_ref[...] += jnp.dot(h, wo_ref[...], preferred_element_type=F32)

    @pl.when(j == nj - 1)
    def _():
        x = x_ref[...] + 0.5 * _rms(acc_ref[...], g_ref[gi:gi + 1, :])
        o_ref[...] = x
        un_ref[...] = _rms(x, gn_ref[gni:gni + 1, :]).astype(BF16)


def _ffn(u2, x2, g, gn, w_in, w_out, gi, gni):
    T, D = x2.shape
    F = w_out.shape[0]
    tm = _tile(T, 512)
    tf = _tile(F, 512)
    nj = F // tf
    row = pl.BlockSpec((tm, D), lambda i, j: (i, 0))
    return pl.pallas_call(
        functools.partial(_ffn_body, gi=gi, gni=gni, nj=nj),
        out_shape=(jax.ShapeDtypeStruct((T, D), F32), jax.ShapeDtypeStruct((T, D), BF16)),
        grid=(T // tm, nj),
        in_specs=[
            row, row,
            pl.BlockSpec(g.shape, lambda i, j: (0, 0)),
            pl.BlockSpec(gn.shape, lambda i, j: (0, 0)),
            pl.BlockSpec((D, tf), lambda i, j: (0, j)),
            pl.BlockSpec((D, tf), lambda i, j: (0, j + nj)),
            pl.BlockSpec((tf, D), lambda i, j: (j, 0)),
        ],
        out_specs=(row, row),
        scratch_shapes=[pltpu.VMEM((tm, D), F32)],
        compiler_params=_params(("parallel", "arbitrary")),
        name="ffn",
    )(u2, x2, g, gn, w_in, w_in, w_out)


def _fox_proj_body(u_ref, wq_ref, wk_ref, wv_ref, qt_ref, k_ref, vt_ref, *, heads_per_chunk):
    u = u_ref[0]

    def transposed(t, out_ref):
        for c in range(heads_per_chunk):
            out_ref[0, c] = jnp.transpose(t[:, c * HEAD_DIM:(c + 1) * HEAD_DIM]).astype(BF16)

    q = jnp.dot(u, wq_ref[...], preferred_element_type=F32)
    transposed(q * (HEAD_DIM ** -0.5 * LOG2E), qt_ref)
    k_ref[0] = jnp.dot(u, wk_ref[...], preferred_element_type=F32).astype(BF16)
    transposed(jnp.dot(u, wv_ref[...], preferred_element_type=F32), vt_ref)


def _fox_proj(u, w_qkv, n_heads):
    B, S, D = u.shape
    fox_w = n_heads * HEAD_DIM
    tm = _tile(S, 1024)
    tn = _tile(fox_w, 512)
    npk = fox_w // tn
    hpc = tn // HEAD_DIM
    t_spec = pl.BlockSpec((1, hpc, HEAD_DIM, tm), lambda b, i, j: (b, j, 0, i))
    t_shape = jax.ShapeDtypeStruct((B, n_heads, HEAD_DIM, S), BF16)
    return pl.pallas_call(
        functools.partial(_fox_proj_body, heads_per_chunk=hpc),
        out_shape=(t_shape, jax.ShapeDtypeStruct((B, S, fox_w), BF16), t_shape),
        grid=(B, S // tm, npk),
        in_specs=[pl.BlockSpec((1, tm, D), lambda b, i, j: (b, i, 0)),
                  pl.BlockSpec((D, tn), lambda b, i, j: (0, j)),
                  pl.BlockSpec((D, tn), lambda b, i, j: (0, j + npk)),
                  pl.BlockSpec((D, tn), lambda b, i, j: (0, j + 2 * npk))],
        out_specs=(t_spec, pl.BlockSpec((1, tm, tn), lambda b, i, j: (b, i, j)), t_spec),
        compiler_params=_params(("parallel", "parallel", "arbitrary")),
        name="fox_proj",
    )(u, w_qkv, w_qkv, w_qkv)


def _rope(t, cos, sin):
    return t * cos + pltpu.roll(t, HEAD_DIM // 2, 1) * sin


def _dil_proj_body(u_ref, wq_ref, wk_ref, wv_ref, cos_ref, sin_ref, q_ref, k_ref, v_ref, *scr, d, tm, tn):
    u = u_ref[0]
    heads = tn // HEAD_DIM
    for kind, (w_ref, out_ref) in enumerate(((wq_ref, q_ref), (wk_ref, k_ref), (wv_ref, v_ref))):
        t = jnp.dot(u, w_ref[...], preferred_element_type=F32)
        for c in range(heads):
            cols = slice(c * HEAD_DIM, (c + 1) * HEAD_DIM)
            sl = t[:, cols]
            if kind < 2:
                sl = _rope(sl, cos_ref[...], sin_ref[...])
            if kind == 0:
                sl = sl * HEAD_DIM ** -0.5
            if d == 1:
                out_ref[0, 0, :, cols] = sl.astype(BF16)
            else:
                slot = kind * heads + c
                scr[0][slot] = sl
                for r in range(d):
                    out_ref[0, r, :, cols] = scr[0][slot, pl.ds(r, tm // d, stride=d), :].astype(BF16)


def _dil_proj(u, w_qkv, cos, sin, group, d, n_groups, dil_out_w):
    B, S, D = u.shape
    tm = _tile(S, 1024)
    tn = dil_out_w
    assert tm % (d * 16) == 0
    shape = jax.ShapeDtypeStruct((B, d, S // d, tn), BF16)
    out_spec = pl.BlockSpec((1, d, tm // d, tn), lambda b, i: (b, 0, i, 0))
    w_spec = lambda kind: pl.BlockSpec((D, tn), lambda b, i: (0, kind * n_groups + group))
    return pl.pallas_call(
        functools.partial(_dil_proj_body, d=d, tm=tm, tn=tn),
        out_shape=(shape, shape, shape),
        grid=(B, S // tm),
        in_specs=[pl.BlockSpec((1, tm, D), lambda b, i: (b, i, 0)), w_spec(0), w_spec(1), w_spec(2),
                  pl.BlockSpec((tm, HEAD_DIM), lambda b, i: (i, 0)),
                  pl.BlockSpec((tm, HEAD_DIM), lambda b, i: (i, 0))],
        out_specs=(out_spec, out_spec, out_spec),
        scratch_shapes=[] if d == 1 else [pltpu.VMEM((3 * tn // HEAD_DIM, tm, HEAD_DIM), F32)],
        compiler_params=_params(("parallel", "parallel")),
        name="dil_proj",
    )(u, w_qkv, w_qkv, w_qkv, cos, sin)


def _gate_proj_body(u_ref, w_ref, o_ref):
    acc = jnp.dot(u_ref[0], w_ref[...], preferred_element_type=F32)
    o_ref[0] = jax.nn.sigmoid(acc).astype(BF16)


def _gate_proj(u, w_g):
    B, S, D = u.shape
    N = w_g.shape[1]
    tm = _tile(S, 1024)
    tn = _tile(N, 1024)
    return pl.pallas_call(
        _gate_proj_body,
        out_shape=jax.ShapeDtypeStruct((B, S, N), BF16),
        grid=(B, S // tm, N // tn),
        in_specs=[pl.BlockSpec((1, tm, D), lambda b, i, j: (b, i, 0)),
                  pl.BlockSpec((D, tn), lambda b, i, j: (0, j))],
        out_specs=pl.BlockSpec((1, tm, tn), lambda b, i, j: (b, i, j)),
        compiler_params=_params(("parallel", "parallel", "arbitrary")),
        name="gate_proj",
    )(u, w_g)


def _scan_body(u_ref, wf_ref, b_ref, e_ref, carry_ref, *, chunk, n_heads):
    tm = u_ref.shape[1]

    @pl.when(pl.program_id(1) == 0)
    def _():
        carry_ref[...] = jnp.zeros(carry_ref.shape, F32)

    row = lax.broadcasted_iota(jnp.int32, (chunk, chunk), 0)
    col = lax.broadcasted_iota(jnp.int32, (chunk, chunk), 1)
    tri = jnp.where(col <= row, 1.0, 0.0).astype(BF16)
    pr = lax.broadcasted_iota(jnp.int32, (HEAD_DIM, HEAD_DIM), 0)
    pc = lax.broadcasted_iota(jnp.int32, (HEAD_DIM, HEAD_DIM), 1)
    place = [jnp.where((pc == N_PIECES * pr + k) & (pr < n_heads), 1.0, 0.0).astype(BF16)
             for k in range(N_PIECES)]

    t = jnp.dot(u_ref[0], wf_ref[...], preferred_element_type=F32) + b_ref[...]
    ls = jnp.minimum(t, 0.0) - jnp.log1p(jnp.exp(-jnp.abs(t)))
    carry = carry_ref[...]
    for c in range(tm // chunk):
        cs = carry
        for piece in _split3(ls[c * chunk:(c + 1) * chunk]):
            cs = cs + jnp.dot(tri, piece, preferred_element_type=F32)
        e = jnp.zeros((chunk, HEAD_DIM), F32)
        for piece, pmat in zip(_split3(cs * (-LOG2E)), place):
            e = e + jnp.dot(piece, pmat, preferred_element_type=F32)
        e_ref[0, c * chunk:(c + 1) * chunk, :] = e.astype(BF16)
        carry = cs[chunk - 1:chunk, :]
    carry_ref[...] = carry


def _forget_scan(u, w_f, b_f):
    B, S, D = u.shape
    H = b_f.shape[0]
    assert N_PIECES * H <= HEAD_DIM
    tm = _tile(S, 1024)
    chunk = _tile(tm, 256)
    return pl.pallas_call(
        functools.partial(_scan_body, chunk=chunk, n_heads=H),
        out_shape=jax.ShapeDtypeStruct((B, S, HEAD_DIM), BF16),
        grid=(B, S // tm),
        in_specs=[pl.BlockSpec((1, tm, D), lambda b, i: (b, i, 0)),
                  pl.BlockSpec(w_f.shape, lambda b, i: (0, 0)),
                  pl.BlockSpec((1, HEAD_DIM), lambda b, i: (0, 0))],
        out_specs=pl.BlockSpec((1, tm, HEAD_DIM), lambda b, i: (b, i, 0)),
        scratch_shapes=[pltpu.VMEM((1, HEAD_DIM), F32)],
        compiler_params=_params(("parallel", "arbitrary")),
        name="forget_scan",
    )(u, w_f, jnp.pad(b_f, (0, HEAD_DIM - H)).reshape(1, HEAD_DIM))


def _fox_body(qt_ref, k_ref, e_ref, vt_ref, o_ref, m_ref, l_ref, acc_ref, *, tq, hg):
    h0 = pl.program_id(1) * hg
    qi = pl.program_id(2)
    r = lax.broadcasted_iota(jnp.int32, (HEAD_DIM, tq), 0)
    qts = []
    for hh in range(hg):
        lo = N_PIECES * (h0 + hh)
        sel = jnp.where((r >= lo) & (r < lo + N_PIECES), 1.0, 0.0).astype(BF16)
        qts.append(jnp.concatenate([qt_ref[0, hh], sel], axis=0))

    m_ref[...] = jnp.full(m_ref.shape, MASK_VALUE, F32)
    l_ref[...] = jnp.zeros(l_ref.shape, F32)
    acc_ref[...] = jnp.zeros(acc_ref.shape, F32)

    def block(k0, tk, mask):
        e = e_ref[0, pl.ds(k0, tk), :]
        ss = []
        for hh in range(hg):
            cols = slice(hh * HEAD_DIM, (hh + 1) * HEAD_DIM)
            kk = jnp.concatenate([k_ref[0, pl.ds(k0, tk), cols], e], axis=1)
            ss.append(jnp.dot(kk, qts[hh], preferred_element_type=F32))
        ps, alphas = [], []
        for hh in range(hg):
            s = ss[hh] if mask is None else jnp.where(mask, ss[hh], MASK_VALUE)
            m_prev = m_ref[hh]
            m_new = jnp.maximum(m_prev, jnp.max(s, axis=0, keepdims=True))
            alpha = jnp.exp2(m_prev - m_new)
            p = jnp.exp2(s - m_new)
            l_ref[hh] = alpha * l_ref[hh] + jnp.sum(p, axis=0, keepdims=True)
            m_ref[hh] = m_new
            ps.append(p.astype(BF16))
            alphas.append(alpha)
        for hh in range(hg):
            pv = jnp.dot(vt_ref[0, hh, :, pl.ds(k0, tk)], ps[hh], preferred_element_type=F32)
            acc_ref[hh] = alphas[hh] * acc_ref[hh] + pv

    def full_block(j, carry):
        block(pl.multiple_of(j * tq, tq), tq, None)
        return carry

    lax.fori_loop(0, qi, full_block, 0)
    key = lax.broadcasted_iota(jnp.int32, (tq, tq), 0)
    qry = lax.broadcasted_iota(jnp.int32, (tq, tq), 1)
    block(pl.multiple_of(qi * tq, tq), tq, key <= qry)

    for hh in range(hg):
        o = acc_ref[hh] / l_ref[hh]
        o_ref[0, :, hh * HEAD_DIM:(hh + 1) * HEAD_DIM] = jnp.transpose(o).astype(o_ref.dtype)


def _fox(qt, k, e, vt):
    B, H, _, S = qt.shape
    tq = _tile(S, 512)
    hg = _tile(H, 4)
    return pl.pallas_call(
        functools.partial(_fox_body, tq=tq, hg=hg),
        out_shape=jax.ShapeDtypeStruct((B, S, H * HEAD_DIM), BF16),
        grid=(B, H // hg, S // tq),
        in_specs=[
            pl.BlockSpec((1, hg, HEAD_DIM, tq), lambda b, h, i: (b, h, 0, i)),
            pl.BlockSpec((1, S, hg * HEAD_DIM), lambda b, h, i: (b, 0, h)),
            pl.BlockSpec((1, S, HEAD_DIM), lambda b, h, i: (b, 0, 0)),
            pl.BlockSpec((1, hg, HEAD_DIM, S), lambda b, h, i: (b, h, 0, 0)),
        ],
        out_specs=pl.BlockSpec((1, tq, hg * HEAD_DIM), lambda b, h, i: (b, i, h)),
        scratch_shapes=[pltpu.VMEM((hg, 1, tq), F32), pltpu.VMEM((hg, 1, tq), F32),
                        pltpu.VMEM((hg, HEAD_DIM, tq), F32)],
        compiler_params=_params(("parallel", "parallel", "arbitrary")),
        name="fox_attention",
    )(qt, k, e, vt)


def _dil_body(*refs, dils, span):
    ng = len(dils)
    q_refs, k_refs, v_refs = refs[:ng], refs[ng:2 * ng], refs[2 * ng:3 * ng]
    o_ref, o_scr, lse_scr = refs[3 * ng:]
    nk = DIL_BLOCK
    i = pl.program_id(2)
    qi = lax.broadcasted_iota(jnp.int32, (nk, 2 * nk), 0)
    ki = lax.broadcasted_iota(jnp.int32, (nk, 2 * nk), 1)
    band = (ki >= qi) & (ki <= qi + nk)

    for g, d in enumerate(dils):
        rows = span // d
        nblk = rows // nk
        q_ref, k_ref, v_ref = q_refs[g], k_refs[g], v_refs[g]

        def several(it, carry, g=g, d=d, rows=rows, nblk=nblk, q_ref=q_ref, k_ref=k_ref, v_ref=v_ref):
            where, ss = [], []
            for j in range(DIL_UNROLL):
                idx = it * DIL_UNROLL + j
                r = lax.div(idx, nblk)
                n = lax.rem(idx, nblk)
                row0 = i * rows + n * nk
                prev0 = pl.multiple_of(jnp.maximum(row0 - nk, 0), nk)
                cur0 = pl.multiple_of(row0, nk)
                q = q_ref[0, r, pl.ds(pl.multiple_of(n * nk, nk), nk), :]
                kk = jnp.concatenate([k_ref[0, r, pl.ds(prev0, nk), :], k_ref[0, r, pl.ds(cur0, nk), :]],
                                     axis=0)
                s = lax.dot_general(q, kk, (((1,), (1,)), ((), ())), preferred_element_type=F32)
                ss.append(jnp.where(band & ((ki >= nk) | (row0 > 0)), s, MASK_VALUE))
                where.append((r, n, prev0, cur0))
            ps, ls, lses = [], [], []
            for s in ss:
                m = jnp.max(s, axis=1, keepdims=True)
                p = jnp.exp(s - m)
                l = jnp.sum(p, axis=1, keepdims=True)
                ps.append(p.astype(BF16))
                ls.append(l)
                lses.append(jnp.broadcast_to(m + jnp.log(l), (nk, HEAD_DIM)))
            for (r, n, prev0, cur0), p, l, lse in zip(where, ps, ls, lses):
                vv = jnp.concatenate([v_ref[0, r, pl.ds(prev0, nk), :], v_ref[0, r, pl.ds(cur0, nk), :]],
                                     axis=0)
                o = jnp.dot(p, vv, preferred_element_type=F32) / l
                tok0 = n * (nk * d) + r
                if d == 1:
                    o_scr[g, pl.ds(pl.multiple_of(tok0, nk), nk), :] = o
                    lse_scr[g, pl.ds(pl.multiple_of(tok0, nk), nk), :] = lse
                else:
                    o_scr[g, pl.ds(tok0, nk, stride=d), :] = o
                    lse_scr[g, pl.ds(tok0, nk, stride=d), :] = lse
            return carry

        assert (d * nblk) % DIL_UNROLL == 0
        lax.fori_loop(0, d * nblk // DIL_UNROLL, several, 0)

    lse = [lse_scr[g] for g in range(ng)]
    top = functools.reduce(jnp.maximum, lse)
    w = [jnp.exp(v - top) for v in lse]
    den = functools.reduce(lambda a, b: a + b, w)
    num = functools.reduce(lambda a, b: a + b, [w[g] * o_scr[g] for g in range(ng)])
    o_ref[0] = (num / den).astype(o_ref.dtype)


def _dilated(qs, ks, vs):
    B, d0, L0, W = qs[0].shape
    S = d0 * L0
    dils = tuple(t.shape[1] for t in qs)
    span = _tile(S, DIL_SPAN)
    assert all(span % (d * DIL_BLOCK) == 0 for d in dils)
    q_specs = [pl.BlockSpec((1, d, span // d, HEAD_DIM), lambda b, h, i: (b, 0, i, h)) for d in dils]
    kv_specs = [pl.BlockSpec((1, d, S // d, HEAD_DIM), lambda b, h, i: (b, 0, 0, h)) for d in dils]
    ng = len(dils)
    return pl.pallas_call(
        functools.partial(_dil_body, dils=dils, span=span),
        out_shape=jax.ShapeDtypeStruct((B, S, W), BF16),
        grid=(B, W // HEAD_DIM, S // span),
        in_specs=q_specs + kv_specs + kv_specs,
        out_specs=pl.BlockSpec((1, span, HEAD_DIM), lambda b, h, i: (b, i, h)),
        scratch_shapes=[pltpu.VMEM((ng, span, HEAD_DIM), F32), pltpu.VMEM((ng, span, HEAD_DIM), F32)],
        compiler_params=_params(("parallel", "parallel", "arbitrary")),
        name="dilated_attention",
    )(*qs, *ks, *vs)


def _merge_body(x_ref, g_ref, oa_ref, ob_ref, ga_ref, gb_ref, wa_ref, wb_ref, wo_ref, out_ref, un_ref, *,
                gi, gni):
    y_a = jnp.dot(oa_ref[0], wa_ref[...], preferred_element_type=F32)
    y_b = jnp.dot(ob_ref[0], wb_ref[...], preferred_element_type=F32)
    merged = ga_ref[0].astype(F32) * y_a + gb_ref[0].astype(F32) * y_b
    y = jnp.dot(merged.astype(BF16), wo_ref[...], preferred_element_type=F32)
    x = x_ref[0] + _rms(y, g_ref[gi:gi + 1, :])
    out_ref[0] = x
    un_ref[0] = _rms(x, g_ref[gni:gni + 1, :]).astype(BF16)


def _merge(x, g, o_a, o_b, gates, w_br_a, w_br_b, w_out, gi, gni):
    B, S, D = x.shape
    tm = _tile(S, 256)
    row = lambda w, c=0: pl.BlockSpec((1, tm, w), lambda b, i: (b, i, c))
    full = lambda a: pl.BlockSpec(a.shape, lambda b, i: (0, 0))
    return pl.pallas_call(
        functools.partial(_merge_body, gi=gi, gni=gni),
        out_shape=(jax.ShapeDtypeStruct((B, S, D), F32), jax.ShapeDtypeStruct((B, S, D), BF16)),
        grid=(B, S // tm),
        in_specs=[row(D), full(g), row(o_a.shape[-1]), row(o_b.shape[-1]), row(D, 0), row(D, 1),
                  full(w_br_a), full(w_br_b), full(w_out)],
        out_specs=(row(D), row(D)),
        compiler_params=_params(("parallel", "parallel")),
        name="mixer_merge",
    )(x, g, o_a, o_b, gates, gates, w_br_a, w_br_b, w_out)


def _ple_body(x_ref, u_ref, g_ref, gn_ref, p_ref, wp_ref, wg_ref, out_ref, un_ref, *, gi, gni):
    e = jnp.dot(p_ref[0, 0].astype(BF16), wp_ref[...], preferred_element_type=F32)
    gate = jax.nn.sigmoid(jnp.dot(u_ref[0], wg_ref[...], preferred_element_type=F32))
    x = x_ref[0] + _rms(gate * e, g_ref[gi:gi + 1, :])
    out_ref[0] = x
    un_ref[0] = _rms(x, gn_ref[gni:gni + 1, :]).astype(BF16)


def _ple(x, u, g, gn, p, layer, w_proj, w_gate, gi, gni):
    B, S, D = x.shape
    P = p.shape[-1]
    tm = _tile(S, 512)
    full = lambda a: pl.BlockSpec(a.shape, lambda b, i: (0, 0))
    row = pl.BlockSpec((1, tm, D), lambda b, i: (b, i, 0))
    return pl.pallas_call(
        functools.partial(_ple_body, gi=gi, gni=gni),
        out_shape=(jax.ShapeDtypeStruct((B, S, D), F32), jax.ShapeDtypeStruct((B, S, D), BF16)),
        grid=(B, S // tm),
        in_specs=[row, row, full(g), full(gn),
                  pl.BlockSpec((1, 1, tm, P), lambda b, i: (layer, b, i, 0)),
                  full(w_proj), full(w_gate)],
        out_specs=(row, row),
        compiler_params=_params(("parallel", "parallel")),
        name="ple_gate",
    )(x, u, g, gn, p, w_proj, w_gate)


def _rope_tables(S):
    half = HEAD_DIM // 2
    inv = ROPE_THETA ** (-jnp.arange(half, dtype=F32) * 2.0 / HEAD_DIM)
    ang = jnp.arange(S, dtype=F32)[:, None] * inv[None, :]
    cos, sin = jnp.cos(ang), jnp.sin(ang)
    return jnp.concatenate([cos, cos], axis=1), jnp.concatenate([-sin, sin], axis=1)


def _mixer(x, u, g, w_in, b_f, w_br_a, w_br_b, w_out, cos, sin):
    B, S, D = x.shape
    fox_w = w_br_a.shape[0]
    dil_out_w = w_br_b.shape[0]
    n_groups = len(DIL_PATTERNS)
    dil_w = n_groups * dil_out_w
    n_fox = b_f.shape[0]
    assert fox_w == n_fox * HEAD_DIM
    assert w_in.shape[1] == 3 * fox_w + n_fox + 3 * dil_w + 2 * D
    for window, d in DIL_PATTERNS:
        assert window // d == DIL_BLOCK

    f_lo = 3 * fox_w
    b_lo = f_lo + n_fox
    g_lo = b_lo + 3 * dil_w
    w_fox = w_in[:, :f_lo].astype(BF16)
    w_f = jnp.pad(w_in[:, f_lo:b_lo], ((0, 0), (0, HEAD_DIM - n_fox))).astype(BF16)
    w_dil = w_in[:, b_lo:g_lo].astype(BF16)
    w_gate = w_in[:, g_lo:].astype(BF16)

    qt, k, vt = _fox_proj(u, w_fox, n_fox)
    e = _forget_scan(u, w_f, b_f)
    o_a = _fox(qt, k, e, vt)

    qs, ks, vs = [], [], []
    for gidx, (_, d) in enumerate(DIL_PATTERNS):
        q, kk, v = _dil_proj(u, w_dil, cos, sin, gidx, d, n_groups, dil_out_w)
        qs.append(q), ks.append(kk), vs.append(v)
    o_b = _dilated(qs, ks, vs)

    gates = _gate_proj(u, w_gate)
    return _merge(x, g, o_a, o_b, gates, w_br_a.astype(BF16), w_br_b.astype(BF16), w_out.astype(BF16), 3, 4)


def kernel(x, p, norm_g, ffn1_w_in, ffn1_w_out, mix_w_in, fox_b_f, mix_w_br_a, mix_w_br_b, mix_w_out,
           ffn2_w_in, ffn2_w_out, ple_w_proj, ple_w_gate):
    B, S, D = x.shape
    T = B * S
    depth = norm_g.shape[0]
    cos, sin = _rope_tables(S)
    x = x.reshape(T, D)
    u = _norm(x, norm_g[0], 0)
    for i in range(depth):
        g = norm_g[i]
        gn = norm_g[(i + 1) % depth]
        x, u = _ffn(u, x, g, g, ffn1_w_in[i].astype(BF16), ffn1_w_out[i].astype(BF16), 1, 2)
        x, u = _mixer(x.reshape(B, S, D), u.reshape(B, S, D), g, mix_w_in[i], fox_b_f[i], mix_w_br_a[i],
                      mix_w_br_b[i], mix_w_out[i], cos, sin)
        x, u = _ffn(u.reshape(T, D), x.reshape(T, D), g, g, ffn2_w_in[i].astype(BF16),
                    ffn2_w_out[i].astype(BF16), 5, 6)
        x, u = _ple(x.reshape(B, S, D), u.reshape(B, S, D), g, gn, p, i, ple_w_proj[i].astype(BF16),
                    ple_w_gate[i].astype(BF16), 7, 0)
        x, u = x.reshape(T, D), u.reshape(T, D)
    return x.reshape(B, S, D)
```

```python
import functools
import math

import jax
import jax.numpy as jnp
from jax import lax
from jax.experimental import pallas as pl
from jax.experimental.pallas import tpu as pltpu

HEAD_DIM = 128
DIL_PATTERNS = ((128, 1), (512, 4), (2048, 16))
DIL_BLOCK = 128
DIL_SPAN = 2048
DIL_UNROLL = 8
ROPE_THETA = 10000.0
RMS_EPS = 1e-6
MASK_VALUE = -1e30
LOG2E = math.log2(math.e)
N_PIECES = 3
V_ONES_ROWS = 16
VMEM_LIMIT_V7X = 56 * 1024 * 1024

F32 = jnp.float32
BF16 = jnp.bfloat16


def _rms(x, g):
    return x * lax.rsqrt(jnp.mean(x * x, axis=-1, keepdims=True) + RMS_EPS) * g


def _tile(n, want):
    t = min(n, want)
    assert n % t == 0, (n, t)
    return t


def _params(sem, vmem=VMEM_LIMIT_V7X):
    return pltpu.CompilerParams(dimension_semantics=sem, vmem_limit_bytes=vmem)


def _split3(v):
    hi = v.astype(BF16)
    r = v - hi.astype(F32)
    mid = r.astype(BF16)
    lo = (r - mid.astype(F32)).astype(BF16)
    return hi, mid, lo


def _norm_body(x_ref, g_ref, u_ref, *, gi):
    u_ref[...] = _rms(x_ref[...], g_ref[gi:gi + 1, :]).astype(BF16)


def _norm(x2, g, gi):
    T, D = x2.shape
    tm = _tile(T, 1024)
    return pl.pallas_call(
        functools.partial(_norm_body, gi=gi),
        out_shape=jax.ShapeDtypeStruct((T, D), BF16),
        grid=(T // tm,),
        in_specs=[pl.BlockSpec((tm, D), lambda i: (i, 0)), pl.BlockSpec(g.shape, lambda i: (0, 0))],
        out_specs=pl.BlockSpec((tm, D), lambda i: (i, 0)),
        compiler_params=_params(("parallel",)),
        name="first_norm",
    )(x2, g)


def _ffn_body(u_ref, x_ref, g_ref, gn_ref, wa_ref, wb_ref, wo_ref, o_ref, un_ref, acc_ref, *, gi, gni, nj):
    j = pl.program_id(1)

    @pl.when(j == 0)
    def _():
        acc_ref[...] = jnp.zeros(acc_ref.shape, F32)

    u = u_ref[...]
    a = jnp.dot(u, wa_ref[...], preferred_element_type=F32)
    b = jnp.dot(u, wb_ref[...], preferred_element_type=F32)
    h = (a * jax.nn.sigmoid(a) * b).astype(BF16)
    acc_ref[...] += jnp.dot(h, wo_ref[...], preferred_element_type=F32)

    @pl.when(j == nj - 1)
    def _():
        x = x_ref[...] + 0.5 * _rms(acc_ref[...], g_ref[gi:gi + 1, :])
        o_ref[...] = x
        un_ref[...] = _rms(x, gn_ref[gni:gni + 1, :]).astype(BF16)


def _ffn(u2, x2, g, gn, w_in, w_out, gi, gni):
    T, D = x2.shape
    F = w_out.shape[0]
    tm = _tile(T, 512)
    tf = _tile(F, 512)
    nj = F // tf
    row = pl.BlockSpec((tm, D), lambda i, j: (i, 0))
    return pl.pallas_call(
        functools.partial(_ffn_body, gi=gi, gni=gni, nj=nj),
        out_shape=(jax.ShapeDtypeStruct((T, D), F32), jax.ShapeDtypeStruct((T, D), BF16)),
        grid=(T // tm, nj),
        in_specs=[
            row, row,
            pl.BlockSpec(g.shape, lambda i, j: (0, 0)),
            pl.BlockSpec(gn.shape, lambda i, j: (0, 0)),
            pl.BlockSpec((D, tf), lambda i, j: (0, j)),
            pl.BlockSpec((D, tf), lambda i, j: (0, j + nj)),
            pl.BlockSpec((tf, D), lambda i, j: (j, 0)),
        ],
        out_specs=(row, row),
        scratch_shapes=[pltpu.VMEM((tm, D), F32)],
        compiler_params=_params(("parallel", "arbitrary")),
        name="ffn",
    )(u2, x2, g, gn, w_in, w_in, w_out)


def _fox_proj_body(u_ref, wq_ref, wk_ref, wv_ref, qt_ref, k_ref, vt_ref, *, heads_per_chunk):
    u = u_ref[0]

    def transposed(t, out_ref):
        for c in range(heads_per_chunk):
            out_ref[0, c, 0:HEAD_DIM, :] = jnp.transpose(t[:, c * HEAD_DIM:(c + 1) * HEAD_DIM]).astype(BF16)

    q = jnp.dot(u, wq_ref[...], preferred_element_type=F32)
    transposed(q * (HEAD_DIM ** -0.5 * LOG2E), qt_ref)
    k_ref[0] = jnp.dot(u, wk_ref[...], preferred_element_type=F32).astype(BF16)
    transposed(jnp.dot(u, wv_ref[...], preferred_element_type=F32), vt_ref)
    for c in range(heads_per_chunk):
        vt_ref[0, c, HEAD_DIM:, :] = jnp.ones((V_ONES_ROWS, u.shape[0]), BF16)


def _fox_proj(u, w_qkv, n_heads):
    B, S, D = u.shape
    fox_w = n_heads * HEAD_DIM
    tm = _tile(S, 1024)
    tn = _tile(fox_w, 512)
    npk = fox_w // tn
    hpc = tn // HEAD_DIM
    t_spec = lambda rows: pl.BlockSpec((1, hpc, rows, tm), lambda b, i, j: (b, j, 0, i))
    t_shape = lambda rows: jax.ShapeDtypeStruct((B, n_heads, rows, S), BF16)
    v_rows = HEAD_DIM + V_ONES_ROWS
    return pl.pallas_call(
        functools.partial(_fox_proj_body, heads_per_chunk=hpc),
        out_shape=(t_shape(HEAD_DIM), jax.ShapeDtypeStruct((B, S, fox_w), BF16), t_shape(v_rows)),
        grid=(B, S // tm, npk),
        in_specs=[pl.BlockSpec((1, tm, D), lambda b, i, j: (b, i, 0)),
                  pl.BlockSpec((D, tn), lambda b, i, j: (0, j)),
                  pl.BlockSpec((D, tn), lambda b, i, j: (0, j + npk)),
                  pl.BlockSpec((D, tn), lambda b, i, j: (0, j + 2 * npk))],
        out_specs=(t_spec(HEAD_DIM), pl.BlockSpec((1, tm, tn), lambda b, i, j: (b, i, j)), t_spec(v_rows)),
        compiler_params=_params(("parallel", "parallel", "arbitrary")),
        name="fox_proj",
    )(u, w_qkv, w_qkv, w_qkv)


def _rope(t, cos, sin):
    return t * cos + pltpu.roll(t, HEAD_DIM // 2, 1) * sin


def _dil_proj_body(u_ref, wq_ref, wk_ref, wv_ref, cos_ref, sin_ref, q_ref, k_ref, v_ref, *scr, d, tm, tn):
    u = u_ref[0]
    heads = tn // HEAD_DIM
    for kind, (w_ref, out_ref) in enumerate(((wq_ref, q_ref), (wk_ref, k_ref), (wv_ref, v_ref))):
        t = jnp.dot(u, w_ref[...], preferred_element_type=F32)
        for c in range(heads):
            cols = slice(c * HEAD_DIM, (c + 1) * HEAD_DIM)
            sl = t[:, cols]
            if kind < 2:
                sl = _rope(sl, cos_ref[...], sin_ref[...])
            if kind == 0:
                sl = sl * HEAD_DIM ** -0.5
            if d == 1:
                out_ref[0, 0, :, cols] = sl.astype(BF16)
            else:
                slot = kind * heads + c
                scr[0][slot] = sl
                for r in range(d):
                    out_ref[0, r, :, cols] = scr[0][slot, pl.ds(r, tm // d, stride=d), :].astype(BF16)


def _dil_proj(u, w_qkv, cos, sin, group, d, n_groups, dil_out_w):
    B, S, D = u.shape
    tm = _tile(S, 1024)
    tn = dil_out_w
    assert tm % (d * 16) == 0
    shape = jax.ShapeDtypeStruct((B, d, S // d, tn), BF16)
    out_spec = pl.BlockSpec((1, d, tm // d, tn), lambda b, i: (b, 0, i, 0))
    w_spec = lambda kind: pl.BlockSpec((D, tn), lambda b, i: (0, kind * n_groups + group))
    return pl.pallas_call(
        functools.partial(_dil_proj_body, d=d, tm=tm, tn=tn),
        out_shape=(shape, shape, shape),
        grid=(B, S // tm),
        in_specs=[pl.BlockSpec((1, tm, D), lambda b, i: (b, i, 0)), w_spec(0), w_spec(1), w_spec(2),
                  pl.BlockSpec((tm, HEAD_DIM), lambda b, i: (i, 0)),
                  pl.BlockSpec((tm, HEAD_DIM), lambda b, i: (i, 0))],
        out_specs=(out_spec, out_spec, out_spec),
        scratch_shapes=[] if d == 1 else [pltpu.VMEM((3 * tn // HEAD_DIM, tm, HEAD_DIM), F32)],
        compiler_params=_params(("parallel", "parallel")),
        name="dil_proj",
    )(u, w_qkv, w_qkv, w_qkv, cos, sin)


def _gate_proj_body(u_ref, w_ref, o_ref):
    acc = jnp.dot(u_ref[0], w_ref[...], preferred_element_type=F32)
    o_ref[0] = jax.nn.sigmoid(acc).astype(BF16)


def _gate_proj(u, w_g):
    B, S, D = u.shape
    N = w_g.shape[1]
    tm = _tile(S, 1024)
    tn = _tile(N, 1024)
    return pl.pallas_call(
        _gate_proj_body,
        out_shape=jax.ShapeDtypeStruct((B, S, N), BF16),
        grid=(B, S // tm, N // tn),
        in_specs=[pl.BlockSpec((1, tm, D), lambda b, i, j: (b, i, 0)),
                  pl.BlockSpec((D, tn), lambda b, i, j: (0, j))],
        out_specs=pl.BlockSpec((1, tm, tn), lambda b, i, j: (b, i, j)),
        compiler_params=_params(("parallel", "parallel", "arbitrary")),
        name="gate_proj",
    )(u, w_g)


def _scan_body(u_ref, wf_ref, b_ref, e_ref, carry_ref, *, chunk, n_heads):
    tm = u_ref.shape[1]

    @pl.when(pl.program_id(1) == 0)
    def _():
        carry_ref[...] = jnp.zeros(carry_ref.shape, F32)

    row = lax.broadcasted_iota(jnp.int32, (chunk, chunk), 0)
    col = lax.broadcasted_iota(jnp.int32, (chunk, chunk), 1)
    tri = jnp.where(col <= row, 1.0, 0.0).astype(BF16)
    pr = lax.broadcasted_iota(jnp.int32, (HEAD_DIM, HEAD_DIM), 0)
    pc = lax.broadcasted_iota(jnp.int32, (HEAD_DIM, HEAD_DIM), 1)
    place = [jnp.where((pc == N_PIECES * pr + k) & (pr < n_heads), 1.0, 0.0).astype(BF16)
             for k in range(N_PIECES)]

    t = jnp.dot(u_ref[0], wf_ref[...], preferred_element_type=F32) + b_ref[...]
    ls = jnp.minimum(t, 0.0) - jnp.log1p(jnp.exp(-jnp.abs(t)))
    carry = carry_ref[...]
    for c in range(tm // chunk):
        cs = carry
        for piece in _split3(ls[c * chunk:(c + 1) * chunk]):
            cs = cs + jnp.dot(tri, piece, preferred_element_type=F32)
        e = jnp.zeros((chunk, HEAD_DIM), F32)
        for piece, pmat in zip(_split3(cs * (-LOG2E)), place):
            e = e + jnp.dot(piece, pmat, preferred_element_type=F32)
        e_ref[0, c * chunk:(c + 1) * chunk, :] = e.astype(BF16)
        carry = cs[chunk - 1:chunk, :]
    carry_ref[...] = carry


def _forget_scan(u, w_f, b_f):
    B, S, D = u.shape
    H = b_f.shape[0]
    assert N_PIECES * H <= HEAD_DIM
    tm = _tile(S, 1024)
    chunk = _tile(tm, 256)
    return pl.pallas_call(
        functools.partial(_scan_body, chunk=chunk, n_heads=H),
        out_shape=jax.ShapeDtypeStruct((B, S, HEAD_DIM), BF16),
        grid=(B, S // tm),
        in_specs=[pl.BlockSpec((1, tm, D), lambda b, i: (b, i, 0)),
                  pl.BlockSpec(w_f.shape, lambda b, i: (0, 0)),
                  pl.BlockSpec((1, HEAD_DIM), lambda b, i: (0, 0))],
        out_specs=pl.BlockSpec((1, tm, HEAD_DIM), lambda b, i: (b, i, 0)),
        scratch_shapes=[pltpu.VMEM((1, HEAD_DIM), F32)],
        compiler_params=_params(("parallel", "arbitrary")),
        name="forget_scan",
    )(u, w_f, jnp.pad(b_f, (0, HEAD_DIM - H)).reshape(1, HEAD_DIM))


def _fox_body(qt_ref, k_ref, e_ref, vt_ref, o_ref, m_ref, acc_ref, sa_ref, sb_ref, *, tq, hg):
    h0 = pl.program_id(1) * hg
    qi = pl.program_id(2)
    r = lax.broadcasted_iota(jnp.int32, (HEAD_DIM, tq), 0)
    qts = []
    for hh in range(hg):
        lo = N_PIECES * (h0 + hh)
        sel = jnp.where((r >= lo) & (r < lo + N_PIECES), 1.0, 0.0).astype(BF16)
        qts.append(jnp.concatenate([qt_ref[0, hh], sel], axis=0))

    m_ref[...] = jnp.full(m_ref.shape, MASK_VALUE, F32)
    acc_ref[...] = jnp.zeros(acc_ref.shape, F32)

    def scores(j, dst):
        k0 = pl.multiple_of(j * tq, tq)
        e = e_ref[0, pl.ds(k0, tq), :]
        for hh in range(hg):
            cols = slice(hh * HEAD_DIM, (hh + 1) * HEAD_DIM)
            kk = jnp.concatenate([k_ref[0, pl.ds(k0, tq), cols], e], axis=1)
            dst[hh] = jnp.dot(kk, qts[hh], preferred_element_type=F32)

    def softmax(src, mask):
        ps, alphas = [], []
        for hh in range(hg):
            s = src[hh] if mask is None else jnp.where(mask, src[hh], MASK_VALUE)
            m_prev = m_ref[hh]
            m_new = jnp.maximum(m_prev, jnp.max(s, axis=0, keepdims=True))
            alphas.append(jnp.exp2(m_prev - m_new))
            ps.append(jnp.exp2((s - m_new).astype(BF16)))
            m_ref[hh] = m_new
        return ps, alphas

    def values(j, ps, alphas):
        k0 = pl.multiple_of(j * tq, tq)
        for hh in range(hg):
            pv = jnp.dot(vt_ref[0, hh, :, pl.ds(k0, tq)], ps[hh], preferred_element_type=F32)
            acc_ref[hh] = alphas[hh] * acc_ref[hh] + pv

    def step(j, src, dst):
        ps, alphas = softmax(src, None)
        scores(j + 1, dst)
        values(j, ps, alphas)

    def last(src):
        key = lax.broadcasted_iota(jnp.int32, (tq, tq), 0)
        qry = lax.broadcasted_iota(jnp.int32, (tq, tq), 1)
        ps, alphas = softmax(src, key <= qry)
        values(qi, ps, alphas)

    scores(0, sa_ref)

    def two_steps(t, carry):
        step(2 * t, sa_ref, sb_ref)
        step(2 * t + 1, sb_ref, sa_ref)
        return carry

    lax.fori_loop(0, lax.div(qi, 2), two_steps, 0)

    @pl.when(lax.rem(qi, 2) == 0)
    def _():
        last(sa_ref)

    @pl.when(lax.rem(qi, 2) == 1)
    def _():
        step(qi - 1, sa_ref, sb_ref)
        last(sb_ref)

    for hh in range(hg):
        o = acc_ref[hh, 0:HEAD_DIM, :] / acc_ref[hh, HEAD_DIM:HEAD_DIM + 1, :]
        o_ref[0, :, hh * HEAD_DIM:(hh + 1) * HEAD_DIM] = jnp.transpose(o).astype(o_ref.dtype)


def _fox(qt, k, e, vt):
    B, H, _, S = qt.shape
    v_rows = vt.shape[2]
    tq = _tile(S, 512)
    hg = _tile(H, 4)
    return pl.pallas_call(
        functools.partial(_fox_body, tq=tq, hg=hg),
        out_shape=jax.ShapeDtypeStruct((B, S, H * HEAD_DIM), BF16),
        grid=(B, H // hg, S // tq),
        in_specs=[
            pl.BlockSpec((1, hg, HEAD_DIM, tq), lambda b, h, i: (b, h, 0, i)),
            pl.BlockSpec((1, S, hg * HEAD_DIM), lambda b, h, i: (b, 0, h)),
            pl.BlockSpec((1, S, HEAD_DIM), lambda b, h, i: (b, 0, 0)),
            pl.BlockSpec((1, hg, v_rows, S), lambda b, h, i: (b, h, 0, 0)),
        ],
        out_specs=pl.BlockSpec((1, tq, hg * HEAD_DIM), lambda b, h, i: (b, i, h)),
        scratch_shapes=[pltpu.VMEM((hg, 1, tq), F32), pltpu.VMEM((hg, v_rows, tq), F32),
                        pltpu.VMEM((hg, tq, tq), F32), pltpu.VMEM((hg, tq, tq), F32)],
        compiler_params=_params(("parallel", "parallel", "arbitrary")),
        name="fox_attention",
    )(qt, k, e, vt)


def _dil_body(*refs, dils, span):
    ng = len(dils)
    q_refs, k_refs, v_refs = refs[:ng], refs[ng:2 * ng], refs[2 * ng:3 * ng]
    o_ref, o_scr, lse_scr = refs[3 * ng:]
    nk = DIL_BLOCK
    i = pl.program_id(2)
    qi = lax.broadcasted_iota(jnp.int32, (nk, 2 * nk), 0)
    ki = lax.broadcasted_iota(jnp.int32, (nk, 2 * nk), 1)
    band = (ki >= qi) & (ki <= qi + nk)

    for g, d in enumerate(dils):
        rows = span // d
        nblk = rows // nk
        q_ref, k_ref, v_ref = q_refs[g], k_refs[g], v_refs[g]

        def several(it, carry, g=g, d=d, rows=rows, nblk=nblk, q_ref=q_ref, k_ref=k_ref, v_ref=v_ref):
            where, ss = [], []
            for j in range(DIL_UNROLL):
                idx = it * DIL_UNROLL + j
                r = lax.div(idx, nblk)
                n = lax.rem(idx, nblk)
                row0 = i * rows + n * nk
                prev0 = pl.multiple_of(jnp.maximum(row0 - nk, 0), nk)
                cur0 = pl.multiple_of(row0, nk)
                q = q_ref[0, r, pl.ds(pl.multiple_of(n * nk, nk), nk), :]
                kk = jnp.concatenate([k_ref[0, r, pl.ds(prev0, nk), :], k_ref[0, r, pl.ds(cur0, nk), :]],
                                     axis=0)
                s = lax.dot_general(q, kk, (((1,), (1,)), ((), ())), preferred_element_type=F32)
                ss.append(jnp.where(band & ((ki >= nk) | (row0 > 0)), s, MASK_VALUE))
                where.append((r, n, prev0, cur0))
            ps, ls, lses = [], [], []
            for s in ss:
                m = jnp.max(s, axis=1, keepdims=True)
                p = jnp.exp(s - m)
                l = jnp.sum(p, axis=1, keepdims=True)
                ps.append(p.astype(BF16))
                ls.append(l)
                lses.append(jnp.broadcast_to(m + jnp.log(l), (nk, HEAD_DIM)))
            for (r, n, prev0, cur0), p, l, lse in zip(where, ps, ls, lses):
                vv = jnp.concatenate([v_ref[0, r, pl.ds(prev0, nk), :], v_ref[0, r, pl.ds(cur0, nk), :]],
                                     axis=0)
                o = jnp.dot(p, vv, preferred_element_type=F32) / l
                tok0 = n * (nk * d) + r
                if d == 1:
                    o_scr[g, pl.ds(pl.multiple_of(tok0, nk), nk), :] = o
                    lse_scr[g, pl.ds(pl.multiple_of(tok0, nk), nk), :] = lse
                else:
                    o_scr[g, pl.ds(tok0, nk, stride=d), :] = o
                    lse_scr[g, pl.ds(tok0, nk, stride=d), :] = lse
            return carry

        assert (d * nblk) % DIL_UNROLL == 0
        lax.fori_loop(0, d * nblk // DIL_UNROLL, several, 0)

    lse = [lse_scr[g] for g in range(ng)]
    top = functools.reduce(jnp.maximum, lse)
    w = [jnp.exp(v - top) for v in lse]
    den = functools.reduce(lambda a, b: a + b, w)
    num = functools.reduce(lambda a, b: a + b, [w[g] * o_scr[g] for g in range(ng)])
    o_ref[0] = (num / den).astype(o_ref.dtype)


def _dilated(qs, ks, vs):
    B, d0, L0, W = qs[0].shape
    S = d0 * L0
    dils = tuple(t.shape[1] for t in qs)
    span = _tile(S, DIL_SPAN)
    assert all(span % (d * DIL_BLOCK) == 0 for d in dils)
    q_specs = [pl.BlockSpec((1, d, span // d, HEAD_DIM), lambda b, h, i: (b, 0, i, h)) for d in dils]
    kv_specs = [pl.BlockSpec((1, d, S // d, HEAD_DIM), lambda b, h, i: (b, 0, 0, h)) for d in dils]
    ng = len(dils)
    return pl.pallas_call(
        functools.partial(_dil_body, dils=dils, span=span),
        out_shape=jax.ShapeDtypeStruct((B, S, W), BF16),
        grid=(B, W // HEAD_DIM, S // span),
        in_specs=q_specs + kv_specs + kv_specs,
        out_specs=pl.BlockSpec((1, span, HEAD_DIM), lambda b, h, i: (b, i, h)),
        scratch_shapes=[pltpu.VMEM((ng, span, HEAD_DIM), F32), pltpu.VMEM((ng, span, HEAD_DIM), F32)],
        compiler_params=_params(("parallel", "parallel", "arbitrary")),
        name="dilated_attention",
    )(*qs, *ks, *vs)


def _merge_body(x_ref, g_ref, oa_ref, ob_ref, ga_ref, gb_ref, wa_ref, wb_ref, wo_ref, out_ref, un_ref, *,
                gi, gni):
    y_a = jnp.dot(oa_ref[0], wa_ref[...], preferred_element_type=F32)
    y_b = jnp.dot(ob_ref[0], wb_ref[...], preferred_element_type=F32)
    merged = ga_ref[0].astype(F32) * y_a + gb_ref[0].astype(F32) * y_b
    y = jnp.dot(merged.astype(BF16), wo_ref[...], preferred_element_type=F32)
    x = x_ref[0] + _rms(y, g_ref[gi:gi + 1, :])
    out_ref[0] = x
    un_ref[0] = _rms(x, g_ref[gni:gni + 1, :]).astype(BF16)


def _merge(x, g, o_a, o_b, gates, w_br_a, w_br_b, w_out, gi, gni):
    B, S, D = x.shape
    tm = _tile(S, 256)
    row = lambda w, c=0: pl.BlockSpec((1, tm, w), lambda b, i: (b, i, c))
    full = lambda a: pl.BlockSpec(a.shape, lambda b, i: (0, 0))
    return pl.pallas_call(
        functools.partial(_merge_body, gi=gi, gni=gni),
        out_shape=(jax.ShapeDtypeStruct((B, S, D), F32), jax.ShapeDtypeStruct((B, S, D), BF16)),
        grid=(B, S // tm),
        in_specs=[row(D), full(g), row(o_a.shape[-1]), row(o_b.shape[-1]), row(D, 0), row(D, 1),
                  full(w_br_a), full(w_br_b), full(w_out)],
        out_specs=(row(D), row(D)),
        compiler_params=_params(("parallel", "parallel")),
        name="mixer_merge",
    )(x, g, o_a, o_b, gates, gates, w_br_a, w_br_b, w_out)


def _ple_body(x_ref, u_ref, g_ref, gn_ref, p_ref, wp_ref, wg_ref, out_ref, un_ref, *, gi, gni):
    e = jnp.dot(p_ref[0, 0].astype(BF16), wp_ref[...], preferred_element_type=F32)
    gate = jax.nn.sigmoid(jnp.dot(u_ref[0], wg_ref[...], preferred_element_type=F32))
    x = x_ref[0] + _rms(gate * e, g_ref[gi:gi + 1, :])
    out_ref[0] = x
    un_ref[0] = _rms(x, gn_ref[gni:gni + 1, :]).astype(BF16)


def _ple(x, u, g, gn, p, layer, w_proj, w_gate, gi, gni):
    B, S, D = x.shape
    P = p.shape[-1]
    tm = _tile(S, 512)
    full = lambda a: pl.BlockSpec(a.shape, lambda b, i: (0, 0))
    row = pl.BlockSpec((1, tm, D), lambda b, i: (b, i, 0))
    return pl.pallas_call(
        functools.partial(_ple_body, gi=gi, gni=gni),
        out_shape=(jax.ShapeDtypeStruct((B, S, D), F32), jax.ShapeDtypeStruct((B, S, D), BF16)),
        grid=(B, S // tm),
        in_specs=[row, row, full(g), full(gn),
                  pl.BlockSpec((1, 1, tm, P), lambda b, i: (layer, b, i, 0)),
                  full(w_proj), full(w_gate)],
        out_specs=(row, row),
        compiler_params=_params(("parallel", "parallel")),
        name="ple_gate",
    )(x, u, g, gn, p, w_proj, w_gate)


def _rope_tables(S):
    half = HEAD_DIM // 2
    inv = ROPE_THETA ** (-jnp.arange(half, dtype=F32) * 2.0 / HEAD_DIM)
    ang = jnp.arange(S, dtype=F32)[:, None] * inv[None, :]
    cos, sin = jnp.cos(ang), jnp.sin(ang)
    return jnp.concatenate([cos, cos], axis=1), jnp.concatenate([-sin, sin], axis=1)


def _mixer(x, u, g, w_in, b_f, w_br_a, w_br_b, w_out, cos, sin):
    B, S, D = x.shape
    fox_w = w_br_a.shape[0]
    dil_out_w = w_br_b.shape[0]
    n_groups = len(DIL_PATTERNS)
    dil_w = n_groups * dil_out_w
    n_fox = b_f.shape[0]
    assert fox_w == n_fox * HEAD_DIM
    assert w_in.shape[1] == 3 * fox_w + n_fox + 3 * dil_w + 2 * D
    for window, d in DIL_PATTERNS:
        assert window // d == DIL_BLOCK

    f_lo = 3 * fox_w
    b_lo = f_lo + n_fox
    g_lo = b_lo + 3 * dil_w
    w_fox = w_in[:, :f_lo].astype(BF16)
    w_f = jnp.pad(w_in[:, f_lo:b_lo], ((0, 0), (0, HEAD_DIM - n_fox))).astype(BF16)
    w_dil = w_in[:, b_lo:g_lo].astype(BF16)
    w_gate = w_in[:, g_lo:].astype(BF16)

    qt, k, vt = _fox_proj(u, w_fox, n_fox)
    e = _forget_scan(u, w_f, b_f)
    o_a = _fox(qt, k, e, vt)

    qs, ks, vs = [], [], []
    for gidx, (_, d) in enumerate(DIL_PATTERNS):
        q, kk, v = _dil_proj(u, w_dil, cos, sin, gidx, d, n_groups, dil_out_w)
        qs.append(q), ks.append(kk), vs.append(v)
    o_b = _dilated(qs, ks, vs)

    gates = _gate_proj(u, w_gate)
    return _merge(x, g, o_a, o_b, gates, w_br_a.astype(BF16), w_br_b.astype(BF16), w_out.astype(BF16), 3, 4)


def kernel(x, p, norm_g, ffn1_w_in, ffn1_w_out, mix_w_in, fox_b_f, mix_w_br_a, mix_w_br_b, mix_w_out,
           ffn2_w_in, ffn2_w_out, ple_w_proj, ple_w_gate):
    B, S, D = x.shape
    T = B * S
    depth = norm_g.shape[0]
    cos, sin = _rope_tables(S)
    x = x.reshape(T, D)
    u = _norm(x, norm_g[0], 0)
    for i in range(depth):
        g = norm_g[i]
        gn = norm_g[(i + 1) % depth]
        x, u = _ffn(u, x, g, g, ffn1_w_in[i].astype(BF16), ffn1_w_out[i].astype(BF16), 1, 2)
        x, u = _mixer(x.reshape(B, S, D), u.reshape(B, S, D), g, mix_w_in[i], fox_b_f[i], mix_w_br_a[i],
                      mix_w_br_b[i], mix_w_out[i], cos, sin)
        x, u = _ffn(u.reshape(T, D), x.reshape(T, D), g, g, ffn2_w_in[i].astype(BF16),
                    ffn2_w_out[i].astype(BF16), 5, 6)
        x, u = _ple(x.reshape(B, S, D), u.reshape(B, S, D), g, gn, p, i, ple_w_proj[i].astype(BF16),
                    ple_w_gate[i].astype(BF16), 7, 0)
        x, u = x.reshape(T, D), u.reshape(T, D)
    return x.reshape(B, S, D)
```

```python
import functools
import math

import jax
import jax.numpy as jnp
from jax import lax
from jax.experimental import pallas as pl
from jax.experimental.pallas import tpu as pltpu

HEAD_DIM = 128
DIL_PATTERNS = ((128, 1), (512, 4), (2048, 16))
DIL_BLOCK = 128
DIL_SPAN = 2048
DIL_UNROLL = 8
ROPE_THETA = 10000.0
RMS_EPS = 1e-6
MASK_VALUE = -1e30
LOG2E = math.log2(math.e)
N_PIECES = 3
V_ONES_ROWS = 16
VMEM_LIMIT_V7X = 56 * 1024 * 1024

F32 = jnp.float32
BF16 = jnp.bfloat16


def _rms(x, g):
    return x * lax.rsqrt(jnp.mean(x * x, axis=-1, keepdims=True) + RMS_EPS) * g


def _tile(n, want):
    t = min(n, want)
    assert n % t == 0, (n, t)
    return t


def _params(sem, vmem=VMEM_LIMIT_V7X):
    return pltpu.CompilerParams(dimension_semantics=sem, vmem_limit_bytes=vmem)


def _col_blocks(w, tn):
    D, N = w.shape
    return w.reshape(D, N // tn, tn).transpose(1, 0, 2)


def _split3(v):
    hi = v.astype(BF16)
    r = v - hi.astype(F32)
    mid = r.astype(BF16)
    lo = (r - mid.astype(F32)).astype(BF16)
    return hi, mid, lo


def _norm_body(x_ref, g_ref, u_ref, *, gi):
    u_ref[...] = _rms(x_ref[...], g_ref[gi:gi + 1, :]).astype(BF16)


def _norm(x2, g, gi):
    T, D = x2.shape
    tm = _tile(T, 1024)
    return pl.pallas_call(
        functools.partial(_norm_body, gi=gi),
        out_shape=jax.ShapeDtypeStruct((T, D), BF16),
        grid=(T // tm,),
        in_specs=[pl.BlockSpec((tm, D), lambda i: (i, 0)), pl.BlockSpec(g.shape, lambda i: (0, 0))],
        out_specs=pl.BlockSpec((tm, D), lambda i: (i, 0)),
        compiler_params=_params(("parallel",)),
        name="first_norm",
    )(x2, g)


def _ffn_body(u_ref, x_ref, g_ref, gn_ref, wa_ref, wb_ref, wo_ref, o_ref, un_ref, acc_ref, *, gi, gni, nt):
    i = pl.program_id(0)
    j = pl.program_id(1)

    def swiglu_chunk():
        u = u_ref[...]
        a = jnp.dot(u, wa_ref[0], preferred_element_type=F32)
        b = jnp.dot(u, wb_ref[0], preferred_element_type=F32)
        h = (a * jax.nn.sigmoid(a) * b).astype(BF16)
        return jnp.dot(h, wo_ref[...], preferred_element_type=F32)

    def finish():
        x = x_ref[...] + 0.5 * _rms(acc_ref[...], g_ref[gi:gi + 1, :])
        o_ref[...] = x
        un_ref[...] = _rms(x, gn_ref[gni:gni + 1, :]).astype(BF16)

    @pl.when((j == 0) & (i == 0))
    def _():
        acc_ref[...] = swiglu_chunk()

    @pl.when((j == 0) & (i > 0) & (i < nt))
    def _():
        finish()
        acc_ref[...] = swiglu_chunk()

    @pl.when((j == 0) & (i == nt))
    def _():
        finish()

    @pl.when((j > 0) & (i < nt))
    def _():
        acc_ref[...] += swiglu_chunk()


def _ffn(u2, x2, g, gn, w_in, w_out, gi, gni):
    T, D = x2.shape
    F = w_out.shape[0]
    tm = _tile(T, 512)
    tf = _tile(F, 512)
    nt = T // tm
    nj = F // tf
    w_in_blocks = _col_blocks(w_in, tf)
    cur = pl.BlockSpec((tm, D), lambda i, j: (jnp.minimum(i, nt - 1), 0))
    prev = pl.BlockSpec((tm, D), lambda i, j: (jnp.maximum(i - 1, 0), 0))
    chunk = lambda i, j: jnp.where(i < nt, j, nj - 1)
    return pl.pallas_call(
        functools.partial(_ffn_body, gi=gi, gni=gni, nt=nt),
        out_shape=(jax.ShapeDtypeStruct((T, D), F32), jax.ShapeDtypeStruct((T, D), BF16)),
        grid=(nt + 1, nj),
        in_specs=[
            cur, prev,
            pl.BlockSpec(g.shape, lambda i, j: (0, 0)),
            pl.BlockSpec(gn.shape, lambda i, j: (0, 0)),
            pl.BlockSpec((1, D, tf), lambda i, j: (chunk(i, j), 0, 0)),
            pl.BlockSpec((1, D, tf), lambda i, j: (chunk(i, j) + nj, 0, 0)),
            pl.BlockSpec((tf, D), lambda i, j: (chunk(i, j), 0)),
        ],
        out_specs=(prev, prev),
        scratch_shapes=[pltpu.VMEM((tm, D), F32)],
        compiler_params=_params(("arbitrary", "arbitrary")),
        name="ffn",
    )(u2, x2, g, gn, w_in_blocks, w_in_blocks, w_out)


def _fox_proj_body(u_ref, wq_ref, wk_ref, wv_ref, qt_ref, k_ref, vt_ref, *, heads_per_chunk):
    u = u_ref[0]

    def transposed(t, out_ref):
        for c in range(heads_per_chunk):
            out_ref[0, c, 0:HEAD_DIM, :] = jnp.transpose(t[:, c * HEAD_DIM:(c + 1) * HEAD_DIM]).astype(BF16)

    q = jnp.dot(u, wq_ref[0], preferred_element_type=F32)
    transposed(q * (HEAD_DIM ** -0.5 * LOG2E), qt_ref)
    k_ref[0] = jnp.dot(u, wk_ref[0], preferred_element_type=F32).astype(BF16)
    transposed(jnp.dot(u, wv_ref[0], preferred_element_type=F32), vt_ref)
    for c in range(heads_per_chunk):
        vt_ref[0, c, HEAD_DIM:, :] = jnp.ones((V_ONES_ROWS, u.shape[0]), BF16)


def _fox_proj(u, w_qkv, n_heads):
    B, S, D = u.shape
    fox_w = n_heads * HEAD_DIM
    tm = _tile(S, 1024)
    tn = _tile(fox_w, 512)
    npk = fox_w // tn
    hpc = tn // HEAD_DIM
    t_spec = lambda rows: pl.BlockSpec((1, hpc, rows, tm), lambda b, i, j: (b, j, 0, i))
    t_shape = lambda rows: jax.ShapeDtypeStruct((B, n_heads, rows, S), BF16)
    v_rows = HEAD_DIM + V_ONES_ROWS
    return pl.pallas_call(
        functools.partial(_fox_proj_body, heads_per_chunk=hpc),
        out_shape=(t_shape(HEAD_DIM), jax.ShapeDtypeStruct((B, S, fox_w), BF16), t_shape(v_rows)),
        grid=(B, S // tm, npk),
        in_specs=[pl.BlockSpec((1, tm, D), lambda b, i, j: (b, i, 0)),
                  pl.BlockSpec((1, D, tn), lambda b, i, j: (j, 0, 0)),
                  pl.BlockSpec((1, D, tn), lambda b, i, j: (j + npk, 0, 0)),
                  pl.BlockSpec((1, D, tn), lambda b, i, j: (j + 2 * npk, 0, 0))],
        out_specs=(t_spec(HEAD_DIM), pl.BlockSpec((1, tm, tn), lambda b, i, j: (b, i, j)), t_spec(v_rows)),
        compiler_params=_params(("parallel", "parallel", "arbitrary")),
        name="fox_proj",
    )(u, *([_col_blocks(w_qkv, tn)] * 3))


def _rope(t, cos, sin):
    return t * cos + pltpu.roll(t, HEAD_DIM // 2, 1) * sin


def _dil_proj_body(u_ref, wq_ref, wk_ref, wv_ref, cos_ref, sin_ref, q_ref, k_ref, v_ref, *scr, d, tm, tn):
    u = u_ref[0]
    heads = tn // HEAD_DIM
    for kind, (w_ref, out_ref) in enumerate(((wq_ref, q_ref), (wk_ref, k_ref), (wv_ref, v_ref))):
        t = jnp.dot(u, w_ref[0], preferred_element_type=F32)
        for c in range(heads):
            cols = slice(c * HEAD_DIM, (c + 1) * HEAD_DIM)
            sl = t[:, cols]
            if kind < 2:
                sl = _rope(sl, cos_ref[...], sin_ref[...])
            if kind == 0:
                sl = sl * HEAD_DIM ** -0.5
            if d == 1:
                out_ref[0, 0, :, cols] = sl.astype(BF16)
            else:
                slot = kind * heads + c
                scr[0][slot] = sl
                for r in range(d):
                    out_ref[0, r, :, cols] = scr[0][slot, pl.ds(r, tm // d, stride=d), :].astype(BF16)


def _dil_proj(u, w_blocks, cos, sin, group, d, n_groups):
    B, S, D = u.shape
    tm = _tile(S, 1024)
    tn = w_blocks.shape[2]
    assert tm % (d * 16) == 0
    shape = jax.ShapeDtypeStruct((B, d, S // d, tn), BF16)
    out_spec = pl.BlockSpec((1, d, tm // d, tn), lambda b, i: (b, 0, i, 0))
    w_spec = lambda kind: pl.BlockSpec((1, D, tn), lambda b, i: (kind * n_groups + group, 0, 0))
    return pl.pallas_call(
        functools.partial(_dil_proj_body, d=d, tm=tm, tn=tn),
        out_shape=(shape, shape, shape),
        grid=(B, S // tm),
        in_specs=[pl.BlockSpec((1, tm, D), lambda b, i: (b, i, 0)), w_spec(0), w_spec(1), w_spec(2),
                  pl.BlockSpec((tm, HEAD_DIM), lambda b, i: (i, 0)),
                  pl.BlockSpec((tm, HEAD_DIM), lambda b, i: (i, 0))],
        out_specs=(out_spec, out_spec, out_spec),
        scratch_shapes=[] if d == 1 else [pltpu.VMEM((3 * tn // HEAD_DIM, tm, HEAD_DIM), F32)],
        compiler_params=_params(("parallel", "parallel")),
        name="dil_proj",
    )(u, w_blocks, w_blocks, w_blocks, cos, sin)


def _gate_proj_body(u_ref, w_ref, o_ref):
    acc = jnp.dot(u_ref[0], w_ref[0], preferred_element_type=F32)
    o_ref[0] = jax.nn.sigmoid(acc).astype(BF16)


def _gate_proj(u, w_g):
    B, S, D = u.shape
    N = w_g.shape[1]
    tm = _tile(S, 1024)
    tn = _tile(N, 1024)
    return pl.pallas_call(
        _gate_proj_body,
        out_shape=jax.ShapeDtypeStruct((B, S, N), BF16),
        grid=(B, S // tm, N // tn),
        in_specs=[pl.BlockSpec((1, tm, D), lambda b, i, j: (b, i, 0)),
                  pl.BlockSpec((1, D, tn), lambda b, i, j: (j, 0, 0))],
        out_specs=pl.BlockSpec((1, tm, tn), lambda b, i, j: (b, i, j)),
        compiler_params=_params(("parallel", "parallel", "arbitrary")),
        name="gate_proj",
    )(u, _col_blocks(w_g, tn))


def _scan_body(u_ref, wf_ref, b_ref, e_ref, carry_ref, *, chunk, n_heads):
    tm = u_ref.shape[1]

    @pl.when(pl.program_id(1) == 0)
    def _():
        carry_ref[...] = jnp.zeros(carry_ref.shape, F32)

    row = lax.broadcasted_iota(jnp.int32, (chunk, chunk), 0)
    col = lax.broadcasted_iota(jnp.int32, (chunk, chunk), 1)
    tri = jnp.where(col <= row, 1.0, 0.0).astype(BF16)
    pr = lax.broadcasted_iota(jnp.int32, (HEAD_DIM, HEAD_DIM), 0)
    pc = lax.broadcasted_iota(jnp.int32, (HEAD_DIM, HEAD_DIM), 1)
    place = [jnp.where((pc == N_PIECES * pr + k) & (pr < n_heads), 1.0, 0.0).astype(BF16)
             for k in range(N_PIECES)]

    t = jnp.dot(u_ref[0], wf_ref[...], preferred_element_type=F32) + b_ref[...]
    ls = jnp.minimum(t, 0.0) - jnp.log1p(jnp.exp(-jnp.abs(t)))
    carry = carry_ref[...]
    for c in range(tm // chunk):
        cs = carry
        for piece in _split3(ls[c * chunk:(c + 1) * chunk]):
            cs = cs + jnp.dot(tri, piece, preferred_element_type=F32)
        e = jnp.zeros((chunk, HEAD_DIM), F32)
        for piece, pmat in zip(_split3(cs * (-LOG2E)), place):
            e = e + jnp.dot(piece, pmat, preferred_element_type=F32)
        e_ref[0, c * chunk:(c + 1) * chunk, :] = e.astype(BF16)
        carry = cs[chunk - 1:chunk, :]
    carry_ref[...] = carry


def _forget_scan(u, w_f, b_f):
    B, S, D = u.shape
    H = b_f.shape[0]
    assert N_PIECES * H <= HEAD_DIM
    tm = _tile(S, 1024)
    chunk = _tile(tm, 256)
    return pl.pallas_call(
        functools.partial(_scan_body, chunk=chunk, n_heads=H),
        out_shape=jax.ShapeDtypeStruct((B, S, HEAD_DIM), BF16),
        grid=(B, S // tm),
        in_specs=[pl.BlockSpec((1, tm, D), lambda b, i: (b, i, 0)),
                  pl.BlockSpec(w_f.shape, lambda b, i: (0, 0)),
                  pl.BlockSpec((1, HEAD_DIM), lambda b, i: (0, 0))],
        out_specs=pl.BlockSpec((1, tm, HEAD_DIM), lambda b, i: (b, i, 0)),
        scratch_shapes=[pltpu.VMEM((1, HEAD_DIM), F32)],
        compiler_params=_params(("parallel", "arbitrary")),
        name="forget_scan",
    )(u, w_f, jnp.pad(b_f, (0, HEAD_DIM - H)).reshape(1, HEAD_DIM))


def _fox_body(qt_ref, k_ref, e_ref, vt_ref, o_ref, m_ref, acc_ref, sa_ref, sb_ref, *, tq, hg):
    h0 = pl.program_id(1) * hg
    qi = pl.program_id(2)
    r = lax.broadcasted_iota(jnp.int32, (HEAD_DIM, tq), 0)
    qts = []
    for hh in range(hg):
        lo = N_PIECES * (h0 + hh)
        sel = jnp.where((r >= lo) & (r < lo + N_PIECES), 1.0, 0.0).astype(BF16)
        qts.append(jnp.concatenate([qt_ref[0, hh], sel], axis=0))

    m_ref[...] = jnp.full(m_ref.shape, MASK_VALUE, F32)
    acc_ref[...] = jnp.zeros(acc_ref.shape, F32)

    def scores(j, dst):
        k0 = pl.multiple_of(j * tq, tq)
        e = e_ref[0, pl.ds(k0, tq), :]
        for hh in range(hg):
            cols = slice(hh * HEAD_DIM, (hh + 1) * HEAD_DIM)
            kk = jnp.concatenate([k_ref[0, pl.ds(k0, tq), cols], e], axis=1)
            dst[hh] = jnp.dot(kk, qts[hh], preferred_element_type=F32)

    def softmax(src, mask):
        ps, alphas = [], []
        for hh in range(hg):
            s = src[hh] if mask is None else jnp.where(mask, src[hh], MASK_VALUE)
            m_prev = m_ref[hh]
            m_new = jnp.maximum(m_prev, jnp.max(s, axis=0, keepdims=True))
            alphas.append(jnp.exp2(m_prev - m_new))
            ps.append(jnp.exp2((s - m_new).astype(BF16)))
            m_ref[hh] = m_new
        return ps, alphas

    def values(j, ps, alphas):
        k0 = pl.multiple_of(j * tq, tq)
        for hh in range(hg):
            pv = jnp.dot(vt_ref[0, hh, :, pl.ds(k0, tq)], ps[hh], preferred_element_type=F32)
            acc_ref[hh] = alphas[hh] * acc_ref[hh] + pv

    def step(j, src, dst):
        ps, alphas = softmax(src, None)
        scores(j + 1, dst)
        values(j, ps, alphas)

    def last(src):
        key = lax.broadcasted_iota(jnp.int32, (tq, tq), 0)
        qry = lax.broadcasted_iota(jnp.int32, (tq, tq), 1)
        ps, alphas = softmax(src, key <= qry)
        values(qi, ps, alphas)

    scores(0, sa_ref)

    def two_steps(t, carry):
        step(2 * t, sa_ref, sb_ref)
        step(2 * t + 1, sb_ref, sa_ref)
        return carry

    lax.fori_loop(0, lax.div(qi, 2), two_steps, 0)

    @pl.when(lax.rem(qi, 2) == 0)
    def _():
        last(sa_ref)

    @pl.when(lax.rem(qi, 2) == 1)
    def _():
        step(qi - 1, sa_ref, sb_ref)
        last(sb_ref)

    for hh in range(hg):
        o = acc_ref[hh, 0:HEAD_DIM, :] / acc_ref[hh, HEAD_DIM:HEAD_DIM + 1, :]
        o_ref[0, :, hh * HEAD_DIM:(hh + 1) * HEAD_DIM] = jnp.transpose(o).astype(o_ref.dtype)


def _fox(qt, k, e, vt):
    B, H, _, S = qt.shape
    v_rows = vt.shape[2]
    tq = _tile(S, 512)
    hg = _tile(H, 4)
    return pl.pallas_call(
        functools.partial(_fox_body, tq=tq, hg=hg),
        out_shape=jax.ShapeDtypeStruct((B, S, H * HEAD_DIM), BF16),
        grid=(B, H // hg, S // tq),
        in_specs=[
            pl.BlockSpec((1, hg, HEAD_DIM, tq), lambda b, h, i: (b, h, 0, i)),
            pl.BlockSpec((1, S, hg * HEAD_DIM), lambda b, h, i: (b, 0, h)),
            pl.BlockSpec((1, S, HEAD_DIM), lambda b, h, i: (b, 0, 0)),
            pl.BlockSpec((1, hg, v_rows, S), lambda b, h, i: (b, h, 0, 0)),
        ],
        out_specs=pl.BlockSpec((1, tq, hg * HEAD_DIM), lambda b, h, i: (b, i, h)),
        scratch_shapes=[pltpu.VMEM((hg, 1, tq), F32), pltpu.VMEM((hg, v_rows, tq), F32),
                        pltpu.VMEM((hg, tq, tq), F32), pltpu.VMEM((hg, tq, tq), F32)],
        compiler_params=_params(("parallel", "parallel", "arbitrary")),
        name="fox_attention",
    )(qt, k, e, vt)


def _dil_body(*refs, dils, span):
    ng = len(dils)
    q_refs, k_refs, v_refs = refs[:ng], refs[ng:2 * ng], refs[2 * ng:3 * ng]
    o_ref, o_scr, lse_scr = refs[3 * ng:]
    nk = DIL_BLOCK
    i = pl.program_id(2)
    qi = lax.broadcasted_iota(jnp.int32, (nk, 2 * nk), 0)
    ki = lax.broadcasted_iota(jnp.int32, (nk, 2 * nk), 1)
    band = (ki >= qi) & (ki <= qi + nk)

    for g, d in enumerate(dils):
        rows = span // d
        nblk = rows // nk
        q_ref, k_ref, v_ref = q_refs[g], k_refs[g], v_refs[g]

        def several(it, carry, g=g, d=d, rows=rows, nblk=nblk, q_ref=q_ref, k_ref=k_ref, v_ref=v_ref):
            where, ss = [], []
            for j in range(DIL_UNROLL):
                idx = it * DIL_UNROLL + j
                r = lax.div(idx, nblk)
                n = lax.rem(idx, nblk)
                row0 = i * rows + n * nk
                prev0 = pl.multiple_of(jnp.maximum(row0 - nk, 0), nk)
                cur0 = pl.multiple_of(row0, nk)
                q = q_ref[0, r, pl.ds(pl.multiple_of(n * nk, nk), nk), :]
                kk = jnp.concatenate([k_ref[0, r, pl.ds(prev0, nk), :], k_ref[0, r, pl.ds(cur0, nk), :]],
                                     axis=0)
                s = lax.dot_general(q, kk, (((1,), (1,)), ((), ())), preferred_element_type=F32)
                ss.append(jnp.where(band & ((ki >= nk) | (row0 > 0)), s, MASK_VALUE))
                where.append((r, n, prev0, cur0))
            ps, ls, lses = [], [], []
            for s in ss:
                m = jnp.max(s, axis=1, keepdims=True)
                p = jnp.exp(s - m)
                l = jnp.sum(p, axis=1, keepdims=True)
                ps.append(p.astype(BF16))
                ls.append(l)
                lses.append(jnp.broadcast_to(m + jnp.log(l), (nk, HEAD_DIM)))
            for (r, n, prev0, cur0), p, l, lse in zip(where, ps, ls, lses):
                vv = jnp.concatenate([v_ref[0, r, pl.ds(prev0, nk), :], v_ref[0, r, pl.ds(cur0, nk), :]],
                                     axis=0)
                o = jnp.dot(p, vv, preferred_element_type=F32) / l
                tok0 = n * (nk * d) + r
                if d == 1:
                    o_scr[g, pl.ds(pl.multiple_of(tok0, nk), nk), :] = o
                    lse_scr[g, pl.ds(pl.multiple_of(tok0, nk), nk), :] = lse
                else:
                    o_scr[g, pl.ds(tok0, nk, stride=d), :] = o
                    lse_scr[g, pl.ds(tok0, nk, stride=d), :] = lse
            return carry

        assert (d * nblk) % DIL_UNROLL == 0
        lax.fori_loop(0, d * nblk // DIL_UNROLL, several, 0)

    lse = [lse_scr[g] for g in range(ng)]
    top = functools.reduce(jnp.maximum, lse)
    w = [jnp.exp(v - top) for v in lse]
    den = functools.reduce(lambda a, b: a + b, w)
    num = functools.reduce(lambda a, b: a + b, [w[g] * o_scr[g] for g in range(ng)])
    o_ref[0] = (num / den).astype(o_ref.dtype)


def _dilated(qs, ks, vs):
    B, d0, L0, W = qs[0].shape
    S = d0 * L0
    dils = tuple(t.shape[1] for t in qs)
    span = _tile(S, DIL_SPAN)
    assert all(span % (d * DIL_BLOCK) == 0 for d in dils)
    q_specs = [pl.BlockSpec((1, d, span // d, HEAD_DIM), lambda b, h, i: (b, 0, i, h)) for d in dils]
    kv_specs = [pl.BlockSpec((1, d, S // d, HEAD_DIM), lambda b, h, i: (b, 0, 0, h)) for d in dils]
    ng = len(dils)
    return pl.pallas_call(
        functools.partial(_dil_body, dils=dils, span=span),
        out_shape=jax.ShapeDtypeStruct((B, S, W), BF16),
        grid=(B, W // HEAD_DIM, S // span),
        in_specs=q_specs + kv_specs + kv_specs,
        out_specs=pl.BlockSpec((1, span, HEAD_DIM), lambda b, h, i: (b, i, h)),
        scratch_shapes=[pltpu.VMEM((ng, span, HEAD_DIM), F32), pltpu.VMEM((ng, span, HEAD_DIM), F32)],
        compiler_params=_params(("parallel", "parallel", "arbitrary")),
        name="dilated_attention",
    )(*qs, *ks, *vs)


def _merge_body(x_ref, g_ref, oa_ref, ob_ref, ga_ref, gb_ref, wa_ref, wb_ref, wo_ref, out_ref, un_ref, *,
                gi, gni):
    y_a = jnp.dot(oa_ref[0], wa_ref[...], preferred_element_type=F32)
    y_b = jnp.dot(ob_ref[0], wb_ref[...], preferred_element_type=F32)
    merged = ga_ref[0].astype(F32) * y_a + gb_ref[0].astype(F32) * y_b
    y = jnp.dot(merged.astype(BF16), wo_ref[...], preferred_element_type=F32)
    x = x_ref[0] + _rms(y, g_ref[gi:gi + 1, :])
    out_ref[0] = x
    un_ref[0] = _rms(x, g_ref[gni:gni + 1, :]).astype(BF16)


def _merge(x, g, o_a, o_b, gates, w_br_a, w_br_b, w_out, gi, gni):
    B, S, D = x.shape
    tm = _tile(S, 256)
    row = lambda w, c=0: pl.BlockSpec((1, tm, w), lambda b, i: (b, i, c))
    full = lambda a: pl.BlockSpec(a.shape, lambda b, i: (0, 0))
    return pl.pallas_call(
        functools.partial(_merge_body, gi=gi, gni=gni),
        out_shape=(jax.ShapeDtypeStruct((B, S, D), F32), jax.ShapeDtypeStruct((B, S, D), BF16)),
        grid=(B, S // tm),
        in_specs=[row(D), full(g), row(o_a.shape[-1]), row(o_b.shape[-1]), row(D, 0), row(D, 1),
                  full(w_br_a), full(w_br_b), full(w_out)],
        out_specs=(row(D), row(D)),
        compiler_params=_params(("parallel", "parallel")),
        name="mixer_merge",
    )(x, g, o_a, o_b, gates, gates, w_br_a, w_br_b, w_out)


def _ple_body(x_ref, u_ref, g_ref, gn_ref, p_ref, wp_ref, wg_ref, out_ref, un_ref, *, gi, gni):
    e = jnp.dot(p_ref[0, 0].astype(BF16), wp_ref[...], preferred_element_type=F32)
    gate = jax.nn.sigmoid(jnp.dot(u_ref[0], wg_ref[...], preferred_element_type=F32))
    x = x_ref[0] + _rms(gate * e, g_ref[gi:gi + 1, :])
    out_ref[0] = x
    un_ref[0] = _rms(x, gn_ref[gni:gni + 1, :]).astype(BF16)


def _ple(x, u, g, gn, p, layer, w_proj, w_gate, gi, gni):
    B, S, D = x.shape
    P = p.shape[-1]
    tm = _tile(S, 512)
    full = lambda a: pl.BlockSpec(a.shape, lambda b, i: (0, 0))
    row = pl.BlockSpec((1, tm, D), lambda b, i: (b, i, 0))
    return pl.pallas_call(
        functools.partial(_ple_body, gi=gi, gni=gni),
        out_shape=(jax.ShapeDtypeStruct((B, S, D), F32), jax.ShapeDtypeStruct((B, S, D), BF16)),
        grid=(B, S // tm),
        in_specs=[row, row, full(g), full(gn),
                  pl.BlockSpec((1, 1, tm, P), lambda b, i: (layer, b, i, 0)),
                  full(w_proj), full(w_gate)],
        out_specs=(row, row),
        compiler_params=_params(("parallel", "parallel")),
        name="ple_gate",
    )(x, u, g, gn, p, w_proj, w_gate)


def _rope_tables(S):
    half = HEAD_DIM // 2
    inv = ROPE_THETA ** (-jnp.arange(half, dtype=F32) * 2.0 / HEAD_DIM)
    ang = jnp.arange(S, dtype=F32)[:, None] * inv[None, :]
    cos, sin = jnp.cos(ang), jnp.sin(ang)
    return jnp.concatenate([cos, cos], axis=1), jnp.concatenate([-sin, sin], axis=1)


def _mixer(x, u, g, w_in, b_f, w_br_a, w_br_b, w_out, cos, sin):
    B, S, D = x.shape
    fox_w = w_br_a.shape[0]
    dil_out_w = w_br_b.shape[0]
    n_groups = len(DIL_PATTERNS)
    dil_w = n_groups * dil_out_w
    n_fox = b_f.shape[0]
    assert fox_w == n_fox * HEAD_DIM
    assert w_in.shape[1] == 3 * fox_w + n_fox + 3 * dil_w + 2 * D
    for window, d in DIL_PATTERNS:
        assert window // d == DIL_BLOCK

    f_lo = 3 * fox_w
    b_lo = f_lo + n_fox
    g_lo = b_lo + 3 * dil_w
    w_fox = w_in[:, :f_lo].astype(BF16)
    w_f = jnp.pad(w_in[:, f_lo:b_lo], ((0, 0), (0, HEAD_DIM - n_fox))).astype(BF16)
    w_dil = _col_blocks(w_in[:, b_lo:g_lo].astype(BF16), dil_out_w)
    w_gate = w_in[:, g_lo:].astype(BF16)

    qt, k, vt = _fox_proj(u, w_fox, n_fox)
    e = _forget_scan(u, w_f, b_f)
    o_a = _fox(qt, k, e, vt)

    qs, ks, vs = [], [], []
    for gidx, (_, d) in enumerate(DIL_PATTERNS):
        q, kk, v = _dil_proj(u, w_dil, cos, sin, gidx, d, n_groups)
        qs.append(q), ks.append(kk), vs.append(v)
    o_b = _dilated(qs, ks, vs)

    gates = _gate_proj(u, w_gate)
    return _merge(x, g, o_a, o_b, gates, w_br_a.astype(BF16), w_br_b.astype(BF16), w_out.astype(BF16), 3, 4)


def kernel(x, p, norm_g, ffn1_w_in, ffn1_w_out, mix_w_in, fox_b_f, mix_w_br_a, mix_w_br_b, mix_w_out,
           ffn2_w_in, ffn2_w_out, ple_w_proj, ple_w_gate):
    B, S, D = x.shape
    T = B * S
    depth = norm_g.shape[0]
    cos, sin = _rope_tables(S)
    x = x.reshape(T, D)
    u = _norm(x, norm_g[0], 0)
    for i in range(depth):
        g = norm_g[i]
        gn = norm_g[(i + 1) % depth]
        x, u = _ffn(u, x, g, g, ffn1_w_in[i].astype(BF16), ffn1_w_out[i].astype(BF16), 1, 2)
        x, u = _mixer(x.reshape(B, S, D), u.reshape(B, S, D), g, mix_w_in[i], fox_b_f[i], mix_w_br_a[i],
                      mix_w_br_b[i], mix_w_out[i], cos, sin)
        x, u = _ffn(u.reshape(T, D), x.reshape(T, D), g, g, ffn2_w_in[i].astype(BF16),
                    ffn2_w_out[i].astype(BF16), 5, 6)
        x, u = _ple(x.reshape(B, S, D), u.reshape(B, S, D), g, gn, p, i, ple_w_proj[i].astype(BF16),
                    ple_w_gate[i].astype(BF16), 7, 0)
        x, u = x.reshape(T, D), u.reshape(T, D)
    return x.reshape(B, S, D)
```

```python
import functools
import math

import jax
import jax.numpy as jnp
from jax import lax
from jax.experimental import pallas as pl
from jax.experimental.pallas import tpu as pltpu

HEAD_DIM = 128
DIL_PATTERNS = ((128, 1), (512, 4), (2048, 16))
DIL_BLOCK = 128
DIL_SPAN = 2048
DIL_UNROLL = 8
ROPE_THETA = 10000.0
RMS_EPS = 1e-6
MASK_VALUE = -1e30
LOG2E = math.log2(math.e)
N_PIECES = 3
V_ONES_ROWS = 16
VMEM_LIMIT_V7X = 56 * 1024 * 1024

F32 = jnp.float32
BF16 = jnp.bfloat16


def _rms(x, g):
    return x * lax.rsqrt(jnp.mean(x * x, axis=-1, keepdims=True) + RMS_EPS) * g


def _tile(n, want):
    t = min(n, want)
    assert n % t == 0, (n, t)
    return t


def _params(sem, vmem=VMEM_LIMIT_V7X):
    return pltpu.CompilerParams(dimension_semantics=sem, vmem_limit_bytes=vmem)


def _split3(v):
    hi = v.astype(BF16)
    r = v - hi.astype(F32)
    mid = r.astype(BF16)
    lo = (r - mid.astype(F32)).astype(BF16)
    return hi, mid, lo


def _norm_body(x_ref, g_ref, u_ref, *, gi):
    u_ref[...] = _rms(x_ref[...], g_ref[gi:gi + 1, :]).astype(BF16)


def _norm(x2, g, gi):
    T, D = x2.shape
    tm = _tile(T, 1024)
    return pl.pallas_call(
        functools.partial(_norm_body, gi=gi),
        out_shape=jax.ShapeDtypeStruct((T, D), BF16),
        grid=(T // tm,),
        in_specs=[pl.BlockSpec((tm, D), lambda i: (i, 0)), pl.BlockSpec(g.shape, lambda i: (0, 0))],
        out_specs=pl.BlockSpec((tm, D), lambda i: (i, 0)),
        compiler_params=_params(("parallel",)),
        name="first_norm",
    )(x2, g)


def _ffn_body(u_ref, x_ref, g_ref, gn_ref, wa_ref, wb_ref, wo_ref, o_ref, un_ref, acc_ref, *, gi, gni, nt):
    i = pl.program_id(0)
    j = pl.program_id(1)

    def swiglu_chunk():
        u = u_ref[...]
        a = jnp.dot(u, wa_ref[...], preferred_element_type=F32)
        b = jnp.dot(u, wb_ref[...], preferred_element_type=F32)
        h = (a * jax.nn.sigmoid(a) * b).astype(BF16)
        return jnp.dot(h, wo_ref[...], preferred_element_type=F32)

    def finish():
        x = x_ref[...] + 0.5 * _rms(acc_ref[...], g_ref[gi:gi + 1, :])
        o_ref[...] = x
        un_ref[...] = _rms(x, gn_ref[gni:gni + 1, :]).astype(BF16)

    @pl.when((j == 0) & (i == 0))
    def _():
        acc_ref[...] = swiglu_chunk()

    @pl.when((j == 0) & (i > 0) & (i < nt))
    def _():
        finish()
        acc_ref[...] = swiglu_chunk()

    @pl.when((j == 0) & (i == nt))
    def _():
        finish()

    @pl.when((j > 0) & (i < nt))
    def _():
        acc_ref[...] += swiglu_chunk()


def _ffn(u2, x2, g, gn, w_in, w_out, gi, gni):
    T, D = x2.shape
    F = w_out.shape[0]
    tm = _tile(T, 512)
    tf = _tile(F, 512)
    nt = T // tm
    nj = F // tf
    cur = pl.BlockSpec((tm, D), lambda i, j: (jnp.minimum(i, nt - 1), 0))
    prev = pl.BlockSpec((tm, D), lambda i, j: (jnp.maximum(i - 1, 0), 0))
    chunk = lambda i, j: jnp.where(i < nt, j, nj - 1)
    return pl.pallas_call(
        functools.partial(_ffn_body, gi=gi, gni=gni, nt=nt),
        out_shape=(jax.ShapeDtypeStruct((T, D), F32), jax.ShapeDtypeStruct((T, D), BF16)),
        grid=(nt + 1, nj),
        in_specs=[
            cur, prev,
            pl.BlockSpec(g.shape, lambda i, j: (0, 0)),
            pl.BlockSpec(gn.shape, lambda i, j: (0, 0)),
            pl.BlockSpec((D, tf), lambda i, j: (0, chunk(i, j))),
            pl.BlockSpec((D, tf), lambda i, j: (0, chunk(i, j) + nj)),
            pl.BlockSpec((tf, D), lambda i, j: (chunk(i, j), 0)),
        ],
        out_specs=(prev, prev),
        scratch_shapes=[pltpu.VMEM((tm, D), F32)],
        compiler_params=_params(("arbitrary", "arbitrary")),
        name="ffn",
    )(u2, x2, g, gn, w_in, w_in, w_out)


def _fox_proj_body(u_ref, wq_ref, wk_ref, wv_ref, qt_ref, k_ref, vt_ref, *, heads_per_chunk):
    u = u_ref[0]

    def transposed(t, out_ref):
        for c in range(heads_per_chunk):
            out_ref[0, c, 0:HEAD_DIM, :] = jnp.transpose(t[:, c * HEAD_DIM:(c + 1) * HEAD_DIM]).astype(BF16)

    q = jnp.dot(u, wq_ref[...], preferred_element_type=F32)
    transposed(q * (HEAD_DIM ** -0.5 * LOG2E), qt_ref)
    k_ref[0] = jnp.dot(u, wk_ref[...], preferred_element_type=F32).astype(BF16)
    transposed(jnp.dot(u, wv_ref[...], preferred_element_type=F32), vt_ref)
    for c in range(heads_per_chunk):
        vt_ref[0, c, HEAD_DIM:, :] = jnp.ones((V_ONES_ROWS, u.shape[0]), BF16)


def _fox_proj(u, w_qkv, n_heads):
    B, S, D = u.shape
    fox_w = n_heads * HEAD_DIM
    tm = _tile(S, 1024)
    tn = _tile(fox_w, 512)
    npk = fox_w // tn
    hpc = tn // HEAD_DIM
    t_spec = lambda rows: pl.BlockSpec((1, hpc, rows, tm), lambda b, i, j: (b, j, 0, i))
    t_shape = lambda rows: jax.ShapeDtypeStruct((B, n_heads, rows, S), BF16)
    v_rows = HEAD_DIM + V_ONES_ROWS
    return pl.pallas_call(
        functools.partial(_fox_proj_body, heads_per_chunk=hpc),
        out_shape=(t_shape(HEAD_DIM), jax.ShapeDtypeStruct((B, S, fox_w), BF16), t_shape(v_rows)),
        grid=(B, S // tm, npk),
        in_specs=[pl.BlockSpec((1, tm, D), lambda b, i, j: (b, i, 0)),
                  pl.BlockSpec((D, tn), lambda b, i, j: (0, j)),
                  pl.BlockSpec((D, tn), lambda b, i, j: (0, j + npk)),
                  pl.BlockSpec((D, tn), lambda b, i, j: (0, j + 2 * npk))],
        out_specs=(t_spec(HEAD_DIM), pl.BlockSpec((1, tm, tn), lambda b, i, j: (b, i, j)), t_spec(v_rows)),
        compiler_params=_params(("parallel", "parallel", "arbitrary")),
        name="fox_proj",
    )(u, w_qkv, w_qkv, w_qkv)


def _rope(t, cos, sin):
    return t * cos + pltpu.roll(t, HEAD_DIM // 2, 1) * sin


def _dil_proj_body(u_ref, wq_ref, wk_ref, wv_ref, cos_ref, sin_ref, q_ref, k_ref, v_ref, *, tn):
    u = u_ref[0]
    for kind, (w_ref, out_ref) in enumerate(((wq_ref, q_ref), (wk_ref, k_ref), (wv_ref, v_ref))):
        t = jnp.dot(u, w_ref[...], preferred_element_type=F32)
        for c in range(tn // HEAD_DIM):
            cols = slice(c * HEAD_DIM, (c + 1) * HEAD_DIM)
            sl = t[:, cols]
            if kind < 2:
                sl = _rope(sl, cos_ref[...], sin_ref[...])
            if kind == 0:
                sl = sl * HEAD_DIM ** -0.5
            out_ref[0, 0, :, cols] = sl.astype(BF16)


def _dil_proj(u, w_qkv, cos, sin, group, d, n_groups, tn):
    B, S, D = u.shape
    L = S // d
    tm = _tile(L, 1024)
    shape = jax.ShapeDtypeStruct((B, d, L, tn), BF16)
    out_spec = pl.BlockSpec((1, 1, tm, tn), lambda b, r, i: (b, r, i, 0))
    w_spec = lambda kind: pl.BlockSpec((D, tn), lambda b, r, i: (0, kind * n_groups + group))
    rope_spec = pl.BlockSpec((tm, HEAD_DIM), lambda b, r, i: (i, r))
    return pl.pallas_call(
        functools.partial(_dil_proj_body, tn=tn),
        out_shape=(shape, shape, shape),
        grid=(B, d, L // tm),
        in_specs=[pl.BlockSpec((1, tm, D), lambda b, r, i: (b, i, r)), w_spec(0), w_spec(1), w_spec(2),
                  rope_spec, rope_spec],
        out_specs=(out_spec, out_spec, out_spec),
        compiler_params=_params(("parallel", "parallel", "parallel")),
        name="dil_proj",
    )(u.reshape(B, L, d * D), w_qkv, w_qkv, w_qkv,
      cos.reshape(L, d * HEAD_DIM), sin.reshape(L, d * HEAD_DIM))


def _gate_proj_body(u_ref, w_ref, o_ref):
    acc = jnp.dot(u_ref[0], w_ref[...], preferred_element_type=F32)
    o_ref[0] = jax.nn.sigmoid(acc).astype(BF16)


def _gate_proj(u, w_g):
    B, S, D = u.shape
    N = w_g.shape[1]
    tm = _tile(S, 1024)
    tn = _tile(N, 1024)
    return pl.pallas_call(
        _gate_proj_body,
        out_shape=jax.ShapeDtypeStruct((B, S, N), BF16),
        grid=(B, S // tm, N // tn),
        in_specs=[pl.BlockSpec((1, tm, D), lambda b, i, j: (b, i, 0)),
                  pl.BlockSpec((D, tn), lambda b, i, j: (0, j))],
        out_specs=pl.BlockSpec((1, tm, tn), lambda b, i, j: (b, i, j)),
        compiler_params=_params(("parallel", "parallel", "arbitrary")),
        name="gate_proj",
    )(u, w_g)


def _scan_body(u_ref, wf_ref, b_ref, e_ref, carry_ref, *, chunk, n_heads):
    tm = u_ref.shape[1]

    @pl.when(pl.program_id(1) == 0)
    def _():
        carry_ref[...] = jnp.zeros(carry_ref.shape, F32)

    row = lax.broadcasted_iota(jnp.int32, (chunk, chunk), 0)
    col = lax.broadcasted_iota(jnp.int32, (chunk, chunk), 1)
    tri = jnp.where(col <= row, 1.0, 0.0).astype(BF16)
    pr = lax.broadcasted_iota(jnp.int32, (HEAD_DIM, HEAD_DIM), 0)
    pc = lax.broadcasted_iota(jnp.int32, (HEAD_DIM, HEAD_DIM), 1)
    place = [jnp.where((pc == N_PIECES * pr + k) & (pr < n_heads), 1.0, 0.0).astype(BF16)
             for k in range(N_PIECES)]

    t = jnp.dot(u_ref[0], wf_ref[...], preferred_element_type=F32) + b_ref[...]
    ls = jnp.minimum(t, 0.0) - jnp.log1p(jnp.exp(-jnp.abs(t)))
    carry = carry_ref[...]
    for c in range(tm // chunk):
        cs = carry
        for piece in _split3(ls[c * chunk:(c + 1) * chunk]):
            cs = cs + jnp.dot(tri, piece, preferred_element_type=F32)
        e = jnp.zeros((chunk, HEAD_DIM), F32)
        for piece, pmat in zip(_split3(cs * (-LOG2E)), place):
            e = e + jnp.dot(piece, pmat, preferred_element_type=F32)
        e_ref[0, c * chunk:(c + 1) * chunk, :] = e.astype(BF16)
        carry = cs[chunk - 1:chunk, :]
    carry_ref[...] = carry


def _forget_scan(u, w_f, b_f):
    B, S, D = u.shape
    H = b_f.shape[0]
    assert N_PIECES * H <= HEAD_DIM
    tm = _tile(S, 1024)
    chunk = _tile(tm, 256)
    return pl.pallas_call(
        functools.partial(_scan_body, chunk=chunk, n_heads=H),
        out_shape=jax.ShapeDtypeStruct((B, S, HEAD_DIM), BF16),
        grid=(B, S // tm),
        in_specs=[pl.BlockSpec((1, tm, D), lambda b, i: (b, i, 0)),
                  pl.BlockSpec(w_f.shape, lambda b, i: (0, 0)),
                  pl.BlockSpec((1, HEAD_DIM), lambda b, i: (0, 0))],
        out_specs=pl.BlockSpec((1, tm, HEAD_DIM), lambda b, i: (b, i, 0)),
        scratch_shapes=[pltpu.VMEM((1, HEAD_DIM), F32)],
        compiler_params=_params(("parallel", "arbitrary")),
        name="forget_scan",
    )(u, w_f, jnp.pad(b_f, (0, HEAD_DIM - H)).reshape(1, HEAD_DIM))


def _fox_body(qt_ref, k_ref, e_ref, vt_ref, o_ref, m_ref, acc_ref, sa_ref, sb_ref, *, tq, hg):
    h0 = pl.program_id(1) * hg
    qi = pl.program_id(2)
    r = lax.broadcasted_iota(jnp.int32, (HEAD_DIM, tq), 0)
    qts = []
    for hh in range(hg):
        lo = N_PIECES * (h0 + hh)
        sel = jnp.where((r >= lo) & (r < lo + N_PIECES), 1.0, 0.0).astype(BF16)
        qts.append(jnp.concatenate([qt_ref[0, hh], sel], axis=0))

    m_ref[...] = jnp.full(m_ref.shape, MASK_VALUE, F32)
    acc_ref[...] = jnp.zeros(acc_ref.shape, F32)

    def scores(j, dst):
        k0 = pl.multiple_of(j * tq, tq)
        e = e_ref[0, pl.ds(k0, tq), :]
        for hh in range(hg):
            cols = slice(hh * HEAD_DIM, (hh + 1) * HEAD_DIM)
            kk = jnp.concatenate([k_ref[0, pl.ds(k0, tq), cols], e], axis=1)
            dst[hh] = jnp.dot(kk, qts[hh], preferred_element_type=F32)

    def softmax(src, mask):
        ps, alphas = [], []
        for hh in range(hg):
            s = src[hh] if mask is None else jnp.where(mask, src[hh], MASK_VALUE)
            m_prev = m_ref[hh]
            m_new = jnp.maximum(m_prev, jnp.max(s, axis=0, keepdims=True))
            alphas.append(jnp.exp2(m_prev - m_new))
            ps.append(jnp.exp2((s - m_new).astype(BF16)))
            m_ref[hh] = m_new
        return ps, alphas

    def values(j, ps, alphas):
        k0 = pl.multiple_of(j * tq, tq)
        for hh in range(hg):
            pv = jnp.dot(vt_ref[0, hh, :, pl.ds(k0, tq)], ps[hh], preferred_element_type=F32)
            acc_ref[hh] = alphas[hh] * acc_ref[hh] + pv

    def step(j, src, dst):
        ps, alphas = softmax(src, None)
        scores(j + 1, dst)
        values(j, ps, alphas)

    def last(src):
        key = lax.broadcasted_iota(jnp.int32, (tq, tq), 0)
        qry = lax.broadcasted_iota(jnp.int32, (tq, tq), 1)
        ps, alphas = softmax(src, key <= qry)
        values(qi, ps, alphas)

    scores(0, sa_ref)

    def two_steps(t, carry):
        step(2 * t, sa_ref, sb_ref)
        step(2 * t + 1, sb_ref, sa_ref)
        return carry

    lax.fori_loop(0, lax.div(qi, 2), two_steps, 0)

    @pl.when(lax.rem(qi, 2) == 0)
    def _():
        last(sa_ref)

    @pl.when(lax.rem(qi, 2) == 1)
    def _():
        step(qi - 1, sa_ref, sb_ref)
        last(sb_ref)

    for hh in range(hg):
        o = acc_ref[hh, 0:HEAD_DIM, :] / acc_ref[hh, HEAD_DIM:HEAD_DIM + 1, :]
        o_ref[0, :, hh * HEAD_DIM:(hh + 1) * HEAD_DIM] = jnp.transpose(o).astype(o_ref.dtype)


def _fox(qt, k, e, vt):
    B, H, _, S = qt.shape
    v_rows = vt.shape[2]
    tq = _tile(S, 512)
    hg = _tile(H, 4)
    return pl.pallas_call(
        functools.partial(_fox_body, tq=tq, hg=hg),
        out_shape=jax.ShapeDtypeStruct((B, S, H * HEAD_DIM), BF16),
        grid=(B, H // hg, S // tq),
        in_specs=[
            pl.BlockSpec((1, hg, HEAD_DIM, tq), lambda b, h, i: (b, h, 0, i)),
            pl.BlockSpec((1, S, hg * HEAD_DIM), lambda b, h, i: (b, 0, h)),
            pl.BlockSpec((1, S, HEAD_DIM), lambda b, h, i: (b, 0, 0)),
            pl.BlockSpec((1, hg, v_rows, S), lambda b, h, i: (b, h, 0, 0)),
        ],
        out_specs=pl.BlockSpec((1, tq, hg * HEAD_DIM), lambda b, h, i: (b, i, h)),
        scratch_shapes=[pltpu.VMEM((hg, 1, tq), F32), pltpu.VMEM((hg, v_rows, tq), F32),
                        pltpu.VMEM((hg, tq, tq), F32), pltpu.VMEM((hg, tq, tq), F32)],
        compiler_params=_params(("parallel", "parallel", "arbitrary")),
        name="fox_attention",
    )(qt, k, e, vt)


def _dil_body(*refs, dils, span):
    ng = len(dils)
    q_refs, k_refs, v_refs = refs[:ng], refs[ng:2 * ng], refs[2 * ng:3 * ng]
    o_ref, o_scr, lse_scr = refs[3 * ng:]
    nk = DIL_BLOCK
    i = pl.program_id(2)
    qi = lax.broadcasted_iota(jnp.int32, (nk, 2 * nk), 0)
    ki = lax.broadcasted_iota(jnp.int32, (nk, 2 * nk), 1)
    band = (ki >= qi) & (ki <= qi + nk)

    for g, d in enumerate(dils):
        rows = span // d
        nblk = rows // nk
        q_ref, k_ref, v_ref = q_refs[g], k_refs[g], v_refs[g]

        def several(it, carry, g=g, d=d, rows=rows, nblk=nblk, q_ref=q_ref, k_ref=k_ref, v_ref=v_ref):
            where, ss = [], []
            for j in range(DIL_UNROLL):
                idx = it * DIL_UNROLL + j
                r = lax.div(idx, nblk)
                n = lax.rem(idx, nblk)
                row0 = i * rows + n * nk
                prev0 = pl.multiple_of(jnp.maximum(row0 - nk, 0), nk)
                cur0 = pl.multiple_of(row0, nk)
                q = q_ref[0, r, pl.ds(pl.multiple_of(n * nk, nk), nk), :]
                kk = jnp.concatenate([k_ref[0, r, pl.ds(prev0, nk), :], k_ref[0, r, pl.ds(cur0, nk), :]],
                                     axis=0)
                s = lax.dot_general(q, kk, (((1,), (1,)), ((), ())), preferred_element_type=F32)
                ss.append(jnp.where(band & ((ki >= nk) | (row0 > 0)), s, MASK_VALUE))
                where.append((r, n, prev0, cur0))
            ps, ls, lses = [], [], []
            for s in ss:
                m = jnp.max(s, axis=1, keepdims=True)
                p = jnp.exp(s - m)
                l = jnp.sum(p, axis=1, keepdims=True)
                ps.append(p.astype(BF16))
                ls.append(l)
                lses.append(jnp.broadcast_to(m + jnp.log(l), (nk, HEAD_DIM)))
            for (r, n, prev0, cur0), p, l, lse in zip(where, ps, ls, lses):
                vv = jnp.concatenate([v_ref[0, r, pl.ds(prev0, nk), :], v_ref[0, r, pl.ds(cur0, nk), :]],
                                     axis=0)
                o = jnp.dot(p, vv, preferred_element_type=F32) / l
                tok0 = n * (nk * d) + r
                if d == 1:
                    o_scr[g, pl.ds(pl.multiple_of(tok0, nk), nk), :] = o
                    lse_scr[g, pl.ds(pl.multiple_of(tok0, nk), nk), :] = lse
                else:
                    o_scr[g, pl.ds(tok0, nk, stride=d), :] = o
                    lse_scr[g, pl.ds(tok0, nk, stride=d), :] = lse
            return carry

        assert (d * nblk) % DIL_UNROLL == 0
        lax.fori_loop(0, d * nblk // DIL_UNROLL, several, 0)

    lse = [lse_scr[g] for g in range(ng)]
    top = functools.reduce(jnp.maximum, lse)
    w = [jnp.exp(v - top) for v in lse]
    den = functools.reduce(lambda a, b: a + b, w)
    num = functools.reduce(lambda a, b: a + b, [w[g] * o_scr[g] for g in range(ng)])
    o_ref[0] = (num / den).astype(o_ref.dtype)


def _dilated(qs, ks, vs):
    B, d0, L0, W = qs[0].shape
    S = d0 * L0
    dils = tuple(t.shape[1] for t in qs)
    span = _tile(S, DIL_SPAN)
    assert all(span % (d * DIL_BLOCK) == 0 for d in dils)
    q_specs = [pl.BlockSpec((1, d, span // d, HEAD_DIM), lambda b, h, i: (b, 0, i, h)) for d in dils]
    kv_specs = [pl.BlockSpec((1, d, S // d, HEAD_DIM), lambda b, h, i: (b, 0, 0, h)) for d in dils]
    ng = len(dils)
    return pl.pallas_call(
        functools.partial(_dil_body, dils=dils, span=span),
        out_shape=jax.ShapeDtypeStruct((B, S, W), BF16),
        grid=(B, W // HEAD_DIM, S // span),
        in_specs=q_specs + kv_specs + kv_specs,
        out_specs=pl.BlockSpec((1, span, HEAD_DIM), lambda b, h, i: (b, i, h)),
        scratch_shapes=[pltpu.VMEM((ng, span, HEAD_DIM), F32), pltpu.VMEM((ng, span, HEAD_DIM), F32)],
        compiler_params=_params(("parallel", "parallel", "arbitrary")),
        name="dilated_attention",
    )(*qs, *ks, *vs)


def _merge_body(x_ref, g_ref, oa_ref, ob_ref, ga_ref, gb_ref, wa_ref, wb_ref, wo_ref, out_ref, un_ref, *,
                gi, gni):
    y_a = jnp.dot(oa_ref[0], wa_ref[...], preferred_element_type=F32)
    y_b = jnp.dot(ob_ref[0], wb_ref[...], preferred_element_type=F32)
    merged = ga_ref[0].astype(F32) * y_a + gb_ref[0].astype(F32) * y_b
    y = jnp.dot(merged.astype(BF16), wo_ref[...], preferred_element_type=F32)
    x = x_ref[0] + _rms(y, g_ref[gi:gi + 1, :])
    out_ref[0] = x
    un_ref[0] = _rms(x, g_ref[gni:gni + 1, :]).astype(BF16)


def _merge(x, g, o_a, o_b, gates, w_br_a, w_br_b, w_out, gi, gni):
    B, S, D = x.shape
    tm = _tile(S, 256)
    row = lambda w, c=0: pl.BlockSpec((1, tm, w), lambda b, i: (b, i, c))
    full = lambda a: pl.BlockSpec(a.shape, lambda b, i: (0, 0))
    return pl.pallas_call(
        functools.partial(_merge_body, gi=gi, gni=gni),
        out_shape=(jax.ShapeDtypeStruct((B, S, D), F32), jax.ShapeDtypeStruct((B, S, D), BF16)),
        grid=(B, S // tm),
        in_specs=[row(D), full(g), row(o_a.shape[-1]), row(o_b.shape[-1]), row(D, 0), row(D, 1),
                  full(w_br_a), full(w_br_b), full(w_out)],
        out_specs=(row(D), row(D)),
        compiler_params=_params(("parallel", "parallel")),
        name="mixer_merge",
    )(x, g, o_a, o_b, gates, gates, w_br_a, w_br_b, w_out)


def _ple_body(x_ref, u_ref, g_ref, gn_ref, p_ref, wp_ref, wg_ref, out_ref, un_ref, *, gi, gni):
    e = jnp.dot(p_ref[0, 0].astype(BF16), wp_ref[...], preferred_element_type=F32)
    gate = jax.nn.sigmoid(jnp.dot(u_ref[0], wg_ref[...], preferred_element_type=F32))
    x = x_ref[0] + _rms(gate * e, g_ref[gi:gi + 1, :])
    out_ref[0] = x
    un_ref[0] = _rms(x, gn_ref[gni:gni + 1, :]).astype(BF16)


def _ple(x, u, g, gn, p, layer, w_proj, w_gate, gi, gni):
    B, S, D = x.shape
    P = p.shape[-1]
    tm = _tile(S, 512)
    full = lambda a: pl.BlockSpec(a.shape, lambda b, i: (0, 0))
    row = pl.BlockSpec((1, tm, D), lambda b, i: (b, i, 0))
    return pl.pallas_call(
        functools.partial(_ple_body, gi=gi, gni=gni),
        out_shape=(jax.ShapeDtypeStruct((B, S, D), F32), jax.ShapeDtypeStruct((B, S, D), BF16)),
        grid=(B, S // tm),
        in_specs=[row, row, full(g), full(gn),
                  pl.BlockSpec((1, 1, tm, P), lambda b, i: (layer, b, i, 0)),
                  full(w_proj), full(w_gate)],
        out_specs=(row, row),
        compiler_params=_params(("parallel", "parallel")),
        name="ple_gate",
    )(x, u, g, gn, p, w_proj, w_gate)


def _rope_tables(S):
    half = HEAD_DIM // 2
    inv = ROPE_THETA ** (-jnp.arange(half, dtype=F32) * 2.0 / HEAD_DIM)
    ang = jnp.arange(S, dtype=F32)[:, None] * inv[None, :]
    cos, sin = jnp.cos(ang), jnp.sin(ang)
    return jnp.concatenate([cos, cos], axis=1), jnp.concatenate([-sin, sin], axis=1)


def _mixer(x, u, g, w_in, b_f, w_br_a, w_br_b, w_out, cos, sin):
    B, S, D = x.shape
    fox_w = w_br_a.shape[0]
    dil_out_w = w_br_b.shape[0]
    n_groups = len(DIL_PATTERNS)
    dil_w = n_groups * dil_out_w
    n_fox = b_f.shape[0]
    assert fox_w == n_fox * HEAD_DIM
    assert w_in.shape[1] == 3 * fox_w + n_fox + 3 * dil_w + 2 * D
    for window, d in DIL_PATTERNS:
        assert window // d == DIL_BLOCK

    f_lo = 3 * fox_w
    b_lo = f_lo + n_fox
    g_lo = b_lo + 3 * dil_w
    w_fox = w_in[:, :f_lo].astype(BF16)
    w_f = jnp.pad(w_in[:, f_lo:b_lo], ((0, 0), (0, HEAD_DIM - n_fox))).astype(BF16)
    w_dil = w_in[:, b_lo:g_lo].astype(BF16)
    w_gate = w_in[:, g_lo:].astype(BF16)

    qt, k, vt = _fox_proj(u, w_fox, n_fox)
    e = _forget_scan(u, w_f, b_f)
    o_a = _fox(qt, k, e, vt)

    qs, ks, vs = [], [], []
    for gidx, (_, d) in enumerate(DIL_PATTERNS):
        q, kk, v = _dil_proj(u, w_dil, cos, sin, gidx, d, n_groups, dil_out_w)
        qs.append(q), ks.append(kk), vs.append(v)
    o_b = _dilated(qs, ks, vs)

    gates = _gate_proj(u, w_gate)
    return _merge(x, g, o_a, o_b, gates, w_br_a.astype(BF16), w_br_b.astype(BF16), w_out.astype(BF16), 3, 4)


def kernel(x, p, norm_g, ffn1_w_in, ffn1_w_out, mix_w_in, fox_b_f, mix_w_br_a, mix_w_br_b, mix_w_out,
           ffn2_w_in, ffn2_w_out, ple_w_proj, ple_w_gate):
    B, S, D = x.shape
    T = B * S
    depth = norm_g.shape[0]
    cos, sin = _rope_tables(S)
    x = x.reshape(T, D)
    u = _norm(x, norm_g[0], 0)
    for i in range(depth):
        g = norm_g[i]
        gn = norm_g[(i + 1) % depth]
        x, u = _ffn(u, x, g, g, ffn1_w_in[i].astype(BF16), ffn1_w_out[i].astype(BF16), 1, 2)
        x, u = _mixer(x.reshape(B, S, D), u.reshape(B, S, D), g, mix_w_in[i], fox_b_f[i], mix_w_br_a[i],
                      mix_w_br_b[i], mix_w_out[i], cos, sin)
        x, u = _ffn(u.reshape(T, D), x.reshape(T, D), g, g, ffn2_w_in[i].astype(BF16),
                    ffn2_w_out[i].astype(BF16), 5, 6)
        x, u = _ple(x.reshape(B, S, D), u.reshape(B, S, D), g, gn, p, i, ple_w_proj[i].astype(BF16),
                    ple_w_gate[i].astype(BF16), 7, 0)
        x, u = x.reshape(T, D), u.reshape(T, D)
    return x.reshape(B, S, D)
```

```python
import functools
import math

import jax
import jax.numpy as jnp
from jax import lax
from jax.experimental import pallas as pl
from jax.experimental.pallas import tpu as pltpu

HEAD_DIM = 128
DIL_PATTERNS = ((128, 1), (512, 4), (2048, 16))
DIL_BLOCK = 128
DIL_SPAN = 2048
DIL_UNROLL = 8
ROPE_THETA = 10000.0
RMS_EPS = 1e-6
MASK_VALUE = -1e30
LOG2E = math.log2(math.e)
N_PIECES = 3
CAST_BLOCK_ELEMS = 3 << 19
V_ONES_ROWS = 16
VMEM_LIMIT_V7X = 56 * 1024 * 1024

F32 = jnp.float32
BF16 = jnp.bfloat16


def _rms(x, g):
    return x * lax.rsqrt(jnp.mean(x * x, axis=-1, keepdims=True) + RMS_EPS) * g


def _tile(n, want):
    t = min(n, want)
    assert n % t == 0, (n, t)
    return t


def _params(sem, vmem=VMEM_LIMIT_V7X):
    return pltpu.CompilerParams(dimension_semantics=sem, vmem_limit_bytes=vmem)


def _split3(v):
    hi = v.astype(BF16)
    r = v - hi.astype(F32)
    mid = r.astype(BF16)
    lo = (r - mid.astype(F32)).astype(BF16)
    return hi, mid, lo


def _cast_body(w_ref, o_ref):
    o_ref[...] = w_ref[...].astype(BF16)


def _to_bf16(w):
    L, K, N = w.shape
    tk = _tile(K, max(16, 1 << int(math.log2(max(1, CAST_BLOCK_ELEMS // N)))))
    spec = pl.BlockSpec((1, tk, N), lambda l, i: (l, i, 0))
    return pl.pallas_call(
        _cast_body,
        out_shape=jax.ShapeDtypeStruct(w.shape, BF16),
        grid=(L, K // tk),
        in_specs=[spec],
        out_specs=spec,
        compiler_params=_params(("parallel", "parallel")),
        name="weight_cast",
    )(w)


def _norm_body(x_ref, g_ref, u_ref, *, gi):
    u_ref[...] = _rms(x_ref[...], g_ref[gi:gi + 1, :]).astype(BF16)


def _norm(x2, g, gi):
    T, D = x2.shape
    tm = _tile(T, 1024)
    return pl.pallas_call(
        functools.partial(_norm_body, gi=gi),
        out_shape=jax.ShapeDtypeStruct((T, D), BF16),
        grid=(T // tm,),
        in_specs=[pl.BlockSpec((tm, D), lambda i: (i, 0)), pl.BlockSpec(g.shape, lambda i: (0, 0))],
        out_specs=pl.BlockSpec((tm, D), lambda i: (i, 0)),
        compiler_params=_params(("parallel",)),
        name="first_norm",
    )(x2, g)


def _ffn_body(u_ref, x_ref, g_ref, gn_ref, wa_ref, wb_ref, wo_ref, o_ref, un_ref, acc_ref, *, gi, gni, nt):
    i = pl.program_id(0)
    j = pl.program_id(1)

    def swiglu_chunk():
        u = u_ref[...]
        a = jnp.dot(u, wa_ref[0], preferred_element_type=F32)
        b = jnp.dot(u, wb_ref[0], preferred_element_type=F32)
        h = (a * jax.nn.sigmoid(a) * b).astype(BF16)
        return jnp.dot(h, wo_ref[0], preferred_element_type=F32)

    def finish():
        x = x_ref[...] + 0.5 * _rms(acc_ref[...], g_ref[gi:gi + 1, :])
        o_ref[...] = x
        un_ref[...] = _rms(x, gn_ref[gni:gni + 1, :]).astype(BF16)

    @pl.when((j == 0) & (i == 0))
    def _():
        acc_ref[...] = swiglu_chunk()

    @pl.when((j == 0) & (i > 0) & (i < nt))
    def _():
        finish()
        acc_ref[...] = swiglu_chunk()

    @pl.when((j == 0) & (i == nt))
    def _():
        finish()

    @pl.when((j > 0) & (i < nt))
    def _():
        acc_ref[...] += swiglu_chunk()


def _ffn(u2, x2, g, gn, w_in, w_out, layer, gi, gni):
    T, D = x2.shape
    F = w_out.shape[1]
    tm = _tile(T, 512)
    tf = _tile(F, 512)
    nt = T // tm
    nj = F // tf
    cur = pl.BlockSpec((tm, D), lambda i, j: (jnp.minimum(i, nt - 1), 0))
    prev = pl.BlockSpec((tm, D), lambda i, j: (jnp.maximum(i - 1, 0), 0))
    chunk = lambda i, j: jnp.where(i < nt, j, nj - 1)
    return pl.pallas_call(
        functools.partial(_ffn_body, gi=gi, gni=gni, nt=nt),
        out_shape=(jax.ShapeDtypeStruct((T, D), F32), jax.ShapeDtypeStruct((T, D), BF16)),
        grid=(nt + 1, nj),
        in_specs=[
            cur, prev,
            pl.BlockSpec(g.shape, lambda i, j: (0, 0)),
            pl.BlockSpec(gn.shape, lambda i, j: (0, 0)),
            pl.BlockSpec((1, D, tf), lambda i, j: (layer, 0, chunk(i, j))),
            pl.BlockSpec((1, D, tf), lambda i, j: (layer, 0, chunk(i, j) + nj)),
            pl.BlockSpec((1, tf, D), lambda i, j: (layer, chunk(i, j), 0)),
        ],
        out_specs=(prev, prev),
        scratch_shapes=[pltpu.VMEM((tm, D), F32)],
        compiler_params=_params(("arbitrary", "arbitrary")),
        name="ffn",
    )(u2, x2, g, gn, w_in, w_in, w_out)


def _fox_proj_body(u_ref, wq_ref, wk_ref, wv_ref, qt_ref, k_ref, vt_ref, *, heads_per_chunk):
    u = u_ref[0]

    def transposed(t, out_ref):
        for c in range(heads_per_chunk):
            out_ref[0, c, 0:HEAD_DIM, :] = jnp.transpose(t[:, c * HEAD_DIM:(c + 1) * HEAD_DIM]).astype(BF16)

    q = jnp.dot(u, wq_ref[...], preferred_element_type=F32)
    transposed(q * (HEAD_DIM ** -0.5 * LOG2E), qt_ref)
    k_ref[0] = jnp.dot(u, wk_ref[...], preferred_element_type=F32).astype(BF16)
    transposed(jnp.dot(u, wv_ref[...], preferred_element_type=F32), vt_ref)
    for c in range(heads_per_chunk):
        vt_ref[0, c, HEAD_DIM:, :] = jnp.ones((V_ONES_ROWS, u.shape[0]), BF16)


def _fox_proj(u, w_qkv, n_heads):
    B, S, D = u.shape
    fox_w = n_heads * HEAD_DIM
    tm = _tile(S, 1024)
    tn = _tile(fox_w, 512)
    npk = fox_w // tn
    hpc = tn // HEAD_DIM
    t_spec = lambda rows: pl.BlockSpec((1, hpc, rows, tm), lambda b, i, j: (b, j, 0, i))
    t_shape = lambda rows: jax.ShapeDtypeStruct((B, n_heads, rows, S), BF16)
    v_rows = HEAD_DIM + V_ONES_ROWS
    return pl.pallas_call(
        functools.partial(_fox_proj_body, heads_per_chunk=hpc),
        out_shape=(t_shape(HEAD_DIM), jax.ShapeDtypeStruct((B, S, fox_w), BF16), t_shape(v_rows)),
        grid=(B, S // tm, npk),
        in_specs=[pl.BlockSpec((1, tm, D), lambda b, i, j: (b, i, 0)),
                  pl.BlockSpec((D, tn), lambda b, i, j: (0, j)),
                  pl.BlockSpec((D, tn), lambda b, i, j: (0, j + npk)),
                  pl.BlockSpec((D, tn), lambda b, i, j: (0, j + 2 * npk))],
        out_specs=(t_spec(HEAD_DIM), pl.BlockSpec((1, tm, tn), lambda b, i, j: (b, i, j)), t_spec(v_rows)),
        compiler_params=_params(("parallel", "parallel", "arbitrary")),
        name="fox_proj",
    )(u, w_qkv, w_qkv, w_qkv)


def _rope(t, cos, sin):
    return t * cos + pltpu.roll(t, HEAD_DIM // 2, 1) * sin


def _dil_proj_body(u_ref, wq_ref, wk_ref, wv_ref, cos_ref, sin_ref, q_ref, k_ref, v_ref, *scr, d, tm, tn):
    u = u_ref[0]
    heads = tn // HEAD_DIM
    for kind, (w_ref, out_ref) in enumerate(((wq_ref, q_ref), (wk_ref, k_ref), (wv_ref, v_ref))):
        t = jnp.dot(u, w_ref[...], preferred_element_type=F32)
        for c in range(heads):
            cols = slice(c * HEAD_DIM, (c + 1) * HEAD_DIM)
            sl = t[:, cols]
            if kind < 2:
                sl = _rope(sl, cos_ref[...], sin_ref[...])
            if kind == 0:
                sl = sl * HEAD_DIM ** -0.5
            if d == 1:
                out_ref[0, 0, :, cols] = sl.astype(BF16)
            else:
                slot = kind * heads + c
                scr[0][slot] = sl
                for r in range(d):
                    out_ref[0, r, :, cols] = scr[0][slot, pl.ds(r, tm // d, stride=d), :].astype(BF16)


def _dil_proj(u, w_qkv, cos, sin, group, d, n_groups, dil_out_w):
    B, S, D = u.shape
    tm = _tile(S, 1024)
    tn = dil_out_w
    assert tm % (d * 16) == 0
    shape = jax.ShapeDtypeStruct((B, d, S // d, tn), BF16)
    out_spec = pl.BlockSpec((1, d, tm // d, tn), lambda b, i: (b, 0, i, 0))
    w_spec = lambda kind: pl.BlockSpec((D, tn), lambda b, i: (0, kind * n_groups + group))
    return pl.pallas_call(
        functools.partial(_dil_proj_body, d=d, tm=tm, tn=tn),
        out_shape=(shape, shape, shape),
        grid=(B, S // tm),
        in_specs=[pl.BlockSpec((1, tm, D), lambda b, i: (b, i, 0)), w_spec(0), w_spec(1), w_spec(2),
                  pl.BlockSpec((tm, HEAD_DIM), lambda b, i: (i, 0)),
                  pl.BlockSpec((tm, HEAD_DIM), lambda b, i: (i, 0))],
        out_specs=(out_spec, out_spec, out_spec),
        scratch_shapes=[] if d == 1 else [pltpu.VMEM((3 * tn // HEAD_DIM, tm, HEAD_DIM), F32)],
        compiler_params=_params(("parallel", "parallel")),
        name="dil_proj",
    )(u, w_qkv, w_qkv, w_qkv, cos, sin)


def _gate_proj_body(u_ref, w_ref, o_ref):
    acc = jnp.dot(u_ref[0], w_ref[...], preferred_element_type=F32)
    o_ref[0] = jax.nn.sigmoid(acc).astype(BF16)


def _gate_proj(u, w_g):
    B, S, D = u.shape
    N = w_g.shape[1]
    tm = _tile(S, 1024)
    tn = _tile(N, 1024)
    return pl.pallas_call(
        _gate_proj_body,
        out_shape=jax.ShapeDtypeStruct((B, S, N), BF16),
        grid=(B, S // tm, N // tn),
        in_specs=[pl.BlockSpec((1, tm, D), lambda b, i, j: (b, i, 0)),
                  pl.BlockSpec((D, tn), lambda b, i, j: (0, j))],
        out_specs=pl.BlockSpec((1, tm, tn), lambda b, i, j: (b, i, j)),
        compiler_params=_params(("parallel", "parallel", "arbitrary")),
        name="gate_proj",
    )(u, w_g)


def _scan_body(u_ref, wf_ref, b_ref, e_ref, carry_ref, *, chunk, n_heads):
    tm = u_ref.shape[1]

    @pl.when(pl.program_id(1) == 0)
    def _():
        carry_ref[...] = jnp.zeros(carry_ref.shape, F32)

    row = lax.broadcasted_iota(jnp.int32, (chunk, chunk), 0)
    col = lax.broadcasted_iota(jnp.int32, (chunk, chunk), 1)
    tri = jnp.where(col <= row, 1.0, 0.0).astype(BF16)
    pr = lax.broadcasted_iota(jnp.int32, (HEAD_DIM, HEAD_DIM), 0)
    pc = lax.broadcasted_iota(jnp.int32, (HEAD_DIM, HEAD_DIM), 1)
    place = [jnp.where((pc == N_PIECES * pr + k) & (pr < n_heads), 1.0, 0.0).astype(BF16)
             for k in range(N_PIECES)]

    t = jnp.dot(u_ref[0], wf_ref[...], preferred_element_type=F32) + b_ref[...]
    ls = jnp.minimum(t, 0.0) - jnp.log1p(jnp.exp(-jnp.abs(t)))
    carry = carry_ref[...]
    for c in range(tm // chunk):
        cs = carry
        for piece in _split3(ls[c * chunk:(c + 1) * chunk]):
            cs = cs + jnp.dot(tri, piece, preferred_element_type=F32)
        e = jnp.zeros((chunk, HEAD_DIM), F32)
        for piece, pmat in zip(_split3(cs * (-LOG2E)), place):
            e = e + jnp.dot(piece, pmat, preferred_element_type=F32)
        e_ref[0, c * chunk:(c + 1) * chunk, :] = e.astype(BF16)
        carry = cs[chunk - 1:chunk, :]
    carry_ref[...] = carry


def _forget_scan(u, w_f, b_f):
    B, S, D = u.shape
    H = b_f.shape[0]
    assert N_PIECES * H <= HEAD_DIM
    tm = _tile(S, 1024)
    chunk = _tile(tm, 256)
    return pl.pallas_call(
        functools.partial(_scan_body, chunk=chunk, n_heads=H),
        out_shape=jax.ShapeDtypeStruct((B, S, HEAD_DIM), BF16),
        grid=(B, S // tm),
        in_specs=[pl.BlockSpec((1, tm, D), lambda b, i: (b, i, 0)),
                  pl.BlockSpec(w_f.shape, lambda b, i: (0, 0)),
                  pl.BlockSpec((1, HEAD_DIM), lambda b, i: (0, 0))],
        out_specs=pl.BlockSpec((1, tm, HEAD_DIM), lambda b, i: (b, i, 0)),
        scratch_shapes=[pltpu.VMEM((1, HEAD_DIM), F32)],
        compiler_params=_params(("parallel", "arbitrary")),
        name="forget_scan",
    )(u, w_f, jnp.pad(b_f, (0, HEAD_DIM - H)).reshape(1, HEAD_DIM))


def _fox_body(qt_ref, k_ref, e_ref, vt_ref, o_ref, m_ref, acc_ref, sa_ref, sb_ref, *, tq, hg):
    h0 = pl.program_id(1) * hg
    qi = pl.program_id(2)
    r = lax.broadcasted_iota(jnp.int32, (HEAD_DIM, tq), 0)
    qts = []
    for hh in range(hg):
        lo = N_PIECES * (h0 + hh)
        sel = jnp.where((r >= lo) & (r < lo + N_PIECES), 1.0, 0.0).astype(BF16)
        qts.append(jnp.concatenate([qt_ref[0, hh], sel], axis=0))

    m_ref[...] = jnp.full(m_ref.shape, MASK_VALUE, F32)
    acc_ref[...] = jnp.zeros(acc_ref.shape, F32)

    def scores(j, dst):
        k0 = pl.multiple_of(j * tq, tq)
        e = e_ref[0, pl.ds(k0, tq), :]
        for hh in range(hg):
            cols = slice(hh * HEAD_DIM, (hh + 1) * HEAD_DIM)
            kk = jnp.concatenate([k_ref[0, pl.ds(k0, tq), cols], e], axis=1)
            dst[hh] = jnp.dot(kk, qts[hh], preferred_element_type=F32)

    def softmax(src, mask):
        ps, alphas = [], []
        for hh in range(hg):
            s = src[hh] if mask is None else jnp.where(mask, src[hh], MASK_VALUE)
            m_prev = m_ref[hh]
            m_new = jnp.maximum(m_prev, jnp.max(s, axis=0, keepdims=True))
            alphas.append(jnp.exp2(m_prev - m_new))
            ps.append(jnp.exp2((s - m_new).astype(BF16)))
            m_ref[hh] = m_new
        return ps, alphas

    def values(j, ps, alphas):
        k0 = pl.multiple_of(j * tq, tq)
        for hh in range(hg):
            pv = jnp.dot(vt_ref[0, hh, :, pl.ds(k0, tq)], ps[hh], preferred_element_type=F32)
            acc_ref[hh] = alphas[hh] * acc_ref[hh] + pv

    def step(j, src, dst):
        ps, alphas = softmax(src, None)
        scores(j + 1, dst)
        values(j, ps, alphas)

    def last(src):
        key = lax.broadcasted_iota(jnp.int32, (tq, tq), 0)
        qry = lax.broadcasted_iota(jnp.int32, (tq, tq), 1)
        ps, alphas = softmax(src, key <= qry)
        values(qi, ps, alphas)

    scores(0, sa_ref)

    def two_steps(t, carry):
        step(2 * t, sa_ref, sb_ref)
        step(2 * t + 1, sb_ref, sa_ref)
        return carry

    lax.fori_loop(0, lax.div(qi, 2), two_steps, 0)

    @pl.when(lax.rem(qi, 2) == 0)
    def _():
        last(sa_ref)

    @pl.when(lax.rem(qi, 2) == 1)
    def _():
        step(qi - 1, sa_ref, sb_ref)
        last(sb_ref)

    for hh in range(hg):
        o = acc_ref[hh, 0:HEAD_DIM, :] / acc_ref[hh, HEAD_DIM:HEAD_DIM + 1, :]
        o_ref[0, :, hh * HEAD_DIM:(hh + 1) * HEAD_DIM] = jnp.transpose(o).astype(o_ref.dtype)


def _fox(qt, k, e, vt):
    B, H, _, S = qt.shape
    v_rows = vt.shape[2]
    tq = _tile(S, 512)
    hg = _tile(H, 4)
    return pl.pallas_call(
        functools.partial(_fox_body, tq=tq, hg=hg),
        out_shape=jax.ShapeDtypeStruct((B, S, H * HEAD_DIM), BF16),
        grid=(B, H // hg, S // tq),
        in_specs=[
            pl.BlockSpec((1, hg, HEAD_DIM, tq), lambda b, h, i: (b, h, 0, i)),
            pl.BlockSpec((1, S, hg * HEAD_DIM), lambda b, h, i: (b, 0, h)),
            pl.BlockSpec((1, S, HEAD_DIM), lambda b, h, i: (b, 0, 0)),
            pl.BlockSpec((1, hg, v_rows, S), lambda b, h, i: (b, h, 0, 0)),
        ],
        out_specs=pl.BlockSpec((1, tq, hg * HEAD_DIM), lambda b, h, i: (b, i, h)),
        scratch_shapes=[pltpu.VMEM((hg, 1, tq), F32), pltpu.VMEM((hg, v_rows, tq), F32),
                        pltpu.VMEM((hg, tq, tq), F32), pltpu.VMEM((hg, tq, tq), F32)],
        compiler_params=_params(("parallel", "parallel", "arbitrary")),
        name="fox_attention",
    )(qt, k, e, vt)


def _dil_body(*refs, dils, span):
    ng = len(dils)
    q_refs, k_refs, v_refs = refs[:ng], refs[ng:2 * ng], refs[2 * ng:3 * ng]
    o_ref, o_scr, lse_scr = refs[3 * ng:]
    nk = DIL_BLOCK
    i = pl.program_id(2)
    qi = lax.broadcasted_iota(jnp.int32, (nk, 2 * nk), 0)
    ki = lax.broadcasted_iota(jnp.int32, (nk, 2 * nk), 1)
    band = (ki >= qi) & (ki <= qi + nk)

    for g, d in enumerate(dils):
        rows = span // d
        nblk = rows // nk
        q_ref, k_ref, v_ref = q_refs[g], k_refs[g], v_refs[g]

        def several(it, carry, g=g, d=d, rows=rows, nblk=nblk, q_ref=q_ref, k_ref=k_ref, v_ref=v_ref):
            where, ss = [], []
            for j in range(DIL_UNROLL):
                idx = it * DIL_UNROLL + j
                r = lax.div(idx, nblk)
                n = lax.rem(idx, nblk)
                row0 = i * rows + n * nk
                prev0 = pl.multiple_of(jnp.maximum(row0 - nk, 0), nk)
                cur0 = pl.multiple_of(row0, nk)
                q = q_ref[0, r, pl.ds(pl.multiple_of(n * nk, nk), nk), :]
                kk = jnp.concatenate([k_ref[0, r, pl.ds(prev0, nk), :], k_ref[0, r, pl.ds(cur0, nk), :]],
                                     axis=0)
                s = lax.dot_general(q, kk, (((1,), (1,)), ((), ())), preferred_element_type=F32)
                ss.append(jnp.where(band & ((ki >= nk) | (row0 > 0)), s, MASK_VALUE))
                where.append((r, n, prev0, cur0))
            ps, ls, lses = [], [], []
            for s in ss:
                m = jnp.max(s, axis=1, keepdims=True)
                p = jnp.exp(s - m)
                l = jnp.sum(p, axis=1, keepdims=True)
                ps.append(p.astype(BF16))
                ls.append(l)
                lses.append(jnp.broadcast_to(m + jnp.log(l), (nk, HEAD_DIM)))
            for (r, n, prev0, cur0), p, l, lse in zip(where, ps, ls, lses):
                vv = jnp.concatenate([v_ref[0, r, pl.ds(prev0, nk), :], v_ref[0, r, pl.ds(cur0, nk), :]],
                                     axis=0)
                o = jnp.dot(p, vv, preferred_element_type=F32) / l
                tok0 = n * (nk * d) + r
                if d == 1:
                    o_scr[g, pl.ds(pl.multiple_of(tok0, nk), nk), :] = o
                    lse_scr[g, pl.ds(pl.multiple_of(tok0, nk), nk), :] = lse
                else:
                    o_scr[g, pl.ds(tok0, nk, stride=d), :] = o
                    lse_scr[g, pl.ds(tok0, nk, stride=d), :] = lse
            return carry

        assert (d * nblk) % DIL_UNROLL == 0
        lax.fori_loop(0, d * nblk // DIL_UNROLL, several, 0)

    lse = [lse_scr[g] for g in range(ng)]
    top = functools.reduce(jnp.maximum, lse)
    w = [jnp.exp(v - top) for v in lse]
    den = functools.reduce(lambda a, b: a + b, w)
    num = functools.reduce(lambda a, b: a + b, [w[g] * o_scr[g] for g in range(ng)])
    o_ref[0] = (num / den).astype(o_ref.dtype)


def _dilated(qs, ks, vs):
    B, d0, L0, W = qs[0].shape
    S = d0 * L0
    dils = tuple(t.shape[1] for t in qs)
    span = _tile(S, DIL_SPAN)
    assert all(span % (d * DIL_BLOCK) == 0 for d in dils)
    q_specs = [pl.BlockSpec((1, d, span // d, HEAD_DIM), lambda b, h, i: (b, 0, i, h)) for d in dils]
    kv_specs = [pl.BlockSpec((1, d, S // d, HEAD_DIM), lambda b, h, i: (b, 0, 0, h)) for d in dils]
    ng = len(dils)
    return pl.pallas_call(
        functools.partial(_dil_body, dils=dils, span=span),
        out_shape=jax.ShapeDtypeStruct((B, S, W), BF16),
        grid=(B, W // HEAD_DIM, S // span),
        in_specs=q_specs + kv_specs + kv_specs,
        out_specs=pl.BlockSpec((1, span, HEAD_DIM), lambda b, h, i: (b, i, h)),
        scratch_shapes=[pltpu.VMEM((ng, span, HEAD_DIM), F32), pltpu.VMEM((ng, span, HEAD_DIM), F32)],
        compiler_params=_params(("parallel", "parallel", "arbitrary")),
        name="dilated_attention",
    )(*qs, *ks, *vs)


def _merge_body(x_ref, g_ref, oa_ref, ob_ref, ga_ref, gb_ref, wa_ref, wb_ref, wo_ref, out_ref, un_ref, *,
                gi, gni):
    y_a = jnp.dot(oa_ref[0], wa_ref[...], preferred_element_type=F32)
    y_b = jnp.dot(ob_ref[0], wb_ref[...], preferred_element_type=F32)
    merged = ga_ref[0].astype(F32) * y_a + gb_ref[0].astype(F32) * y_b
    y = jnp.dot(merged.astype(BF16), wo_ref[...], preferred_element_type=F32)
    x = x_ref[0] + _rms(y, g_ref[gi:gi + 1, :])
    out_ref[0] = x
    un_ref[0] = _rms(x, g_ref[gni:gni + 1, :]).astype(BF16)


def _merge(x, g, o_a, o_b, gates, w_br_a, w_br_b, w_out, gi, gni):
    B, S, D = x.shape
    tm = _tile(S, 256)
    row = lambda w, c=0: pl.BlockSpec((1, tm, w), lambda b, i: (b, i, c))
    full = lambda a: pl.BlockSpec(a.shape, lambda b, i: (0, 0))
    return pl.pallas_call(
        functools.partial(_merge_body, gi=gi, gni=gni),
        out_shape=(jax.ShapeDtypeStruct((B, S, D), F32), jax.ShapeDtypeStruct((B, S, D), BF16)),
        grid=(B, S // tm),
        in_specs=[row(D), full(g), row(o_a.shape[-1]), row(o_b.shape[-1]), row(D, 0), row(D, 1),
                  full(w_br_a), full(w_br_b), full(w_out)],
        out_specs=(row(D), row(D)),
        compiler_params=_params(("parallel", "parallel")),
        name="mixer_merge",
    )(x, g, o_a, o_b, gates, gates, w_br_a, w_br_b, w_out)


def _ple_body(x_ref, u_ref, g_ref, gn_ref, p_ref, wp_ref, wg_ref, out_ref, un_ref, *, gi, gni):
    e = jnp.dot(p_ref[0, 0].astype(BF16), wp_ref[...], preferred_element_type=F32)
    gate = jax.nn.sigmoid(jnp.dot(u_ref[0], wg_ref[...], preferred_element_type=F32))
    x = x_ref[0] + _rms(gate * e, g_ref[gi:gi + 1, :])
    out_ref[0] = x
    un_ref[0] = _rms(x, gn_ref[gni:gni + 1, :]).astype(BF16)


def _ple(x, u, g, gn, p, layer, w_proj, w_gate, gi, gni):
    B, S, D = x.shape
    P = p.shape[-1]
    tm = _tile(S, 512)
    full = lambda a: pl.BlockSpec(a.shape, lambda b, i: (0, 0))
    row = pl.BlockSpec((1, tm, D), lambda b, i: (b, i, 0))
    return pl.pallas_call(
        functools.partial(_ple_body, gi=gi, gni=gni),
        out_shape=(jax.ShapeDtypeStruct((B, S, D), F32), jax.ShapeDtypeStruct((B, S, D), BF16)),
        grid=(B, S // tm),
        in_specs=[row, row, full(g), full(gn),
                  pl.BlockSpec((1, 1, tm, P), lambda b, i: (layer, b, i, 0)),
                  full(w_proj), full(w_gate)],
        out_specs=(row, row),
        compiler_params=_params(("parallel", "parallel")),
        name="ple_gate",
    )(x, u, g, gn, p, w_proj, w_gate)


def _rope_tables(S):
    half = HEAD_DIM // 2
    inv = ROPE_THETA ** (-jnp.arange(half, dtype=F32) * 2.0 / HEAD_DIM)
    ang = jnp.arange(S, dtype=F32)[:, None] * inv[None, :]
    cos, sin = jnp.cos(ang), jnp.sin(ang)
    return jnp.concatenate([cos, cos], axis=1), jnp.concatenate([-sin, sin], axis=1)


def _mixer(x, u, g, w_in, b_f, w_br_a, w_br_b, w_out, cos, sin):
    B, S, D = x.shape
    fox_w = w_br_a.shape[0]
    dil_out_w = w_br_b.shape[0]
    n_groups = len(DIL_PATTERNS)
    dil_w = n_groups * dil_out_w
    n_fox = b_f.shape[0]
    assert fox_w == n_fox * HEAD_DIM
    assert w_in.shape[1] == 3 * fox_w + n_fox + 3 * dil_w + 2 * D
    for window, d in DIL_PATTERNS:
        assert window // d == DIL_BLOCK

    f_lo = 3 * fox_w
    b_lo = f_lo + n_fox
    g_lo = b_lo + 3 * dil_w
    w_fox = w_in[:, :f_lo].astype(BF16)
    w_f = jnp.pad(w_in[:, f_lo:b_lo], ((0, 0), (0, HEAD_DIM - n_fox))).astype(BF16)
    w_dil = w_in[:, b_lo:g_lo].astype(BF16)
    w_gate = w_in[:, g_lo:].astype(BF16)

    qt, k, vt = _fox_proj(u, w_fox, n_fox)
    e = _forget_scan(u, w_f, b_f)
    o_a = _fox(qt, k, e, vt)

    qs, ks, vs = [], [], []
    for gidx, (_, d) in enumerate(DIL_PATTERNS):
        q, kk, v = _dil_proj(u, w_dil, cos, sin, gidx, d, n_groups, dil_out_w)
        qs.append(q), ks.append(kk), vs.append(v)
    o_b = _dilated(qs, ks, vs)

    gates = _gate_proj(u, w_gate)
    return _merge(x, g, o_a, o_b, gates, w_br_a.astype(BF16), w_br_b.astype(BF16), w_out.astype(BF16), 3, 4)


def kernel(x, p, norm_g, ffn1_w_in, ffn1_w_out, mix_w_in, fox_b_f, mix_w_br_a, mix_w_br_b, mix_w_out,
           ffn2_w_in, ffn2_w_out, ple_w_proj, ple_w_gate):
    B, S, D = x.shape
    T = B * S
    depth = norm_g.shape[0]
    cos, sin = _rope_tables(S)
    w1_in, w1_out, w2_in, w2_out = (_to_bf16(w) for w in (ffn1_w_in, ffn1_w_out, ffn2_w_in, ffn2_w_out))
    x = x.reshape(T, D)
    u = _norm(x, norm_g[0], 0)
    for i in range(depth):
        g = norm_g[i]
        gn = norm_g[(i + 1) % depth]
        x, u = _ffn(u, x, g, g, w1_in, w1_out, i, 1, 2)
        x, u = _mixer(x.reshape(B, S, D), u.reshape(B, S, D), g, mix_w_in[i], fox_b_f[i], mix_w_br_a[i],
                      mix_w_br_b[i], mix_w_out[i], cos, sin)
        x, u = _ffn(u.reshape(T, D), x.reshape(T, D), g, g, w2_in, w2_out, i, 5, 6)
        x, u = _ple(x.reshape(B, S, D), u.reshape(B, S, D), g, gn, p, i, ple_w_proj[i].astype(BF16),
                    ple_w_gate[i].astype(BF16), 7, 0)
        x, u = x.reshape(T, D), u.reshape(T, D)
    return x.reshape(B, S, D)
```

```python
import functools
import math

import jax
import jax.numpy as jnp
from jax import lax
from jax.experimental import pallas as pl
from jax.experimental.pallas import tpu as pltpu

HEAD_DIM = 128
DIL_PATTERNS = ((128, 1), (512, 4), (2048, 16))
DIL_BLOCK = 128
DIL_SPAN = 2048
DIL_UNROLL = 16
ROPE_THETA = 10000.0
RMS_EPS = 1e-6
MASK_VALUE = -1e30
LOG2E = math.log2(math.e)
N_PIECES = 3
CAST_BLOCK_ELEMS = 3 << 19
V_ONES_ROWS = 16
VMEM_LIMIT_V7X = 56 * 1024 * 1024

F32 = jnp.float32
BF16 = jnp.bfloat16


def _rms(x, g):
    return x * lax.rsqrt(jnp.mean(x * x, axis=-1, keepdims=True) + RMS_EPS) * g


def _tile(n, want):
    t = min(n, want)
    assert n % t == 0, (n, t)
    return t


def _params(sem, vmem=VMEM_LIMIT_V7X):
    return pltpu.CompilerParams(dimension_semantics=sem, vmem_limit_bytes=vmem)


def _split3(v):
    hi = v.astype(BF16)
    r = v - hi.astype(F32)
    mid = r.astype(BF16)
    lo = (r - mid.astype(F32)).astype(BF16)
    return hi, mid, lo


def _cast_body(w_ref, o_ref):
    o_ref[...] = w_ref[...].astype(BF16)


def _to_bf16(w):
    L, K, N = w.shape
    tk = _tile(K, max(16, 1 << int(math.log2(max(1, CAST_BLOCK_ELEMS // N)))))
    spec = pl.BlockSpec((1, tk, N), lambda l, i: (l, i, 0))
    return pl.pallas_call(
        _cast_body,
        out_shape=jax.ShapeDtypeStruct(w.shape, BF16),
        grid=(L, K // tk),
        in_specs=[spec],
        out_specs=spec,
        compiler_params=_params(("parallel", "parallel")),
        name="weight_cast",
    )(w)


def _norm_body(x_ref, g_ref, u_ref, *, gi):
    u_ref[...] = _rms(x_ref[...], g_ref[gi:gi + 1, :]).astype(BF16)


def _norm(x2, g, gi):
    T, D = x2.shape
    tm = _tile(T, 1024)
    return pl.pallas_call(
        functools.partial(_norm_body, gi=gi),
        out_shape=jax.ShapeDtypeStruct((T, D), BF16),
        grid=(T // tm,),
        in_specs=[pl.BlockSpec((tm, D), lambda i: (i, 0)), pl.BlockSpec(g.shape, lambda i: (0, 0))],
        out_specs=pl.BlockSpec((tm, D), lambda i: (i, 0)),
        compiler_params=_params(("parallel",)),
        name="first_norm",
    )(x2, g)


def _ffn_body(u_ref, x_ref, g_ref, gn_ref, wa_ref, wb_ref, wo_ref, o_ref, un_ref, acc_ref, *, gi, gni, nt):
    i = pl.program_id(0)
    j = pl.program_id(1)

    def swiglu_chunk():
        u = u_ref[...]
        a = jnp.dot(u, wa_ref[0], preferred_element_type=F32)
        b = jnp.dot(u, wb_ref[0], preferred_element_type=F32)
        h = (a * jax.nn.sigmoid(a) * b).astype(BF16)
        return jnp.dot(h, wo_ref[0], preferred_element_type=F32)

    def finish():
        x = x_ref[...] + 0.5 * _rms(acc_ref[...], g_ref[gi:gi + 1, :])
        o_ref[...] = x
        un_ref[...] = _rms(x, gn_ref[gni:gni + 1, :]).astype(BF16)

    @pl.when((j == 0) & (i == 0))
    def _():
        acc_ref[...] = swiglu_chunk()

    @pl.when((j == 0) & (i > 0) & (i < nt))
    def _():
        finish()
        acc_ref[...] = swiglu_chunk()

    @pl.when((j == 0) & (i == nt))
    def _():
        finish()

    @pl.when((j > 0) & (i < nt))
    def _():
        acc_ref[...] += swiglu_chunk()


def _ffn(u2, x2, g, gn, w_in, w_out, layer, gi, gni):
    T, D = x2.shape
    F = w_out.shape[1]
    tm = _tile(T, 512)
    tf = _tile(F, 512)
    nt = T // tm
    nj = F // tf
    def prev_tile(switch):
        switch = max(1, switch)
        return lambda i, j: (jnp.clip(i - 1 + (j >= switch).astype(jnp.int32), 0, nt - 1), 0)

    cur = pl.BlockSpec((tm, D), lambda i, j: (jnp.minimum(i, nt - 1), 0))
    x_prev = pl.BlockSpec((tm, D), prev_tile(nj // 2))
    o_prev = pl.BlockSpec((tm, D), prev_tile(nj // 4))
    un_prev = pl.BlockSpec((tm, D), prev_tile(3 * nj // 4))
    chunk = lambda i, j: jnp.where(i < nt, j, nj - 1)
    return pl.pallas_call(
        functools.partial(_ffn_body, gi=gi, gni=gni, nt=nt),
        out_shape=(jax.ShapeDtypeStruct((T, D), F32), jax.ShapeDtypeStruct((T, D), BF16)),
        grid=(nt + 1, nj),
        in_specs=[
            cur, x_prev,
            pl.BlockSpec(g.shape, lambda i, j: (0, 0)),
            pl.BlockSpec(gn.shape, lambda i, j: (0, 0)),
            pl.BlockSpec((1, D, tf), lambda i, j: (layer, 0, chunk(i, j))),
            pl.BlockSpec((1, D, tf), lambda i, j: (layer, 0, chunk(i, j) + nj)),
            pl.BlockSpec((1, tf, D), lambda i, j: (layer, chunk(i, j), 0)),
        ],
        out_specs=(o_prev, un_prev),
        scratch_shapes=[pltpu.VMEM((tm, D), F32)],
        compiler_params=_params(("arbitrary", "arbitrary")),
        name="ffn",
    )(u2, x2, g, gn, w_in, w_in, w_out)


def _fox_proj_body(u_ref, wq_ref, wk_ref, wv_ref, qt_ref, k_ref, vt_ref, *, heads_per_chunk):
    u = u_ref[0]

    def transposed(t, out_ref):
        for c in range(heads_per_chunk):
            out_ref[0, c, 0:HEAD_DIM, :] = jnp.transpose(t[:, c * HEAD_DIM:(c + 1) * HEAD_DIM]).astype(BF16)

    q = jnp.dot(u, wq_ref[...], preferred_element_type=F32)
    transposed(q * (HEAD_DIM ** -0.5 * LOG2E), qt_ref)
    k_ref[0] = jnp.dot(u, wk_ref[...], preferred_element_type=F32).astype(BF16)
    transposed(jnp.dot(u, wv_ref[...], preferred_element_type=F32), vt_ref)
    for c in range(heads_per_chunk):
        vt_ref[0, c, HEAD_DIM:, :] = jnp.ones((V_ONES_ROWS, u.shape[0]), BF16)


def _fox_proj(u, w_qkv, n_heads):
    B, S, D = u.shape
    fox_w = n_heads * HEAD_DIM
    tm = _tile(S, 1024)
    tn = _tile(fox_w, 512)
    npk = fox_w // tn
    hpc = tn // HEAD_DIM
    t_spec = lambda rows: pl.BlockSpec((1, hpc, rows, tm), lambda b, i, j: (b, j, 0, i))
    t_shape = lambda rows: jax.ShapeDtypeStruct((B, n_heads, rows, S), BF16)
    v_rows = HEAD_DIM + V_ONES_ROWS
    return pl.pallas_call(
        functools.partial(_fox_proj_body, heads_per_chunk=hpc),
        out_shape=(t_shape(HEAD_DIM), jax.ShapeDtypeStruct((B, S, fox_w), BF16), t_shape(v_rows)),
        grid=(B, S // tm, npk),
        in_specs=[pl.BlockSpec((1, tm, D), lambda b, i, j: (b, i, 0)),
                  pl.BlockSpec((D, tn), lambda b, i, j: (0, j)),
                  pl.BlockSpec((D, tn), lambda b, i, j: (0, j + npk)),
                  pl.BlockSpec((D, tn), lambda b, i, j: (0, j + 2 * npk))],
        out_specs=(t_spec(HEAD_DIM), pl.BlockSpec((1, tm, tn), lambda b, i, j: (b, i, j)), t_spec(v_rows)),
        compiler_params=_params(("parallel", "parallel", "arbitrary")),
        name="fox_proj",
    )(u, w_qkv, w_qkv, w_qkv)


def _rope(t, cos, sin):
    return t * cos + pltpu.roll(t, HEAD_DIM // 2, 1) * sin


def _dil_proj_body(u_ref, wq_ref, wk_ref, wv_ref, cos_ref, sin_ref, q_ref, k_ref, v_ref, *scr, d, tm, tn):
    u = u_ref[0]
    heads = tn // HEAD_DIM
    for kind, (w_ref, out_ref) in enumerate(((wq_ref, q_ref), (wk_ref, k_ref), (wv_ref, v_ref))):
        t = jnp.dot(u, w_ref[...], preferred_element_type=F32)
        for c in range(heads):
            cols = slice(c * HEAD_DIM, (c + 1) * HEAD_DIM)
            sl = t[:, cols]
            if kind < 2:
                sl = _rope(sl, cos_ref[...], sin_ref[...])
            if kind == 0:
                sl = sl * HEAD_DIM ** -0.5
            if d == 1:
                out_ref[0, 0, :, cols] = sl.astype(BF16)
            else:
                slot = kind * heads + c
                scr[0][slot] = sl
                for r in range(d):
                    out_ref[0, r, :, cols] = scr[0][slot, pl.ds(r, tm // d, stride=d), :].astype(BF16)


def _dil_proj(u, w_qkv, cos, sin, group, d, n_groups, dil_out_w):
    B, S, D = u.shape
    tm = _tile(S, 1024)
    tn = dil_out_w
    assert tm % (d * 16) == 0
    shape = jax.ShapeDtypeStruct((B, d, S // d, tn), BF16)
    out_spec = pl.BlockSpec((1, d, tm // d, tn), lambda b, i: (b, 0, i, 0))
    w_spec = lambda kind: pl.BlockSpec((D, tn), lambda b, i: (0, kind * n_groups + group))
    return pl.pallas_call(
        functools.partial(_dil_proj_body, d=d, tm=tm, tn=tn),
        out_shape=(shape, shape, shape),
        grid=(B, S // tm),
        in_specs=[pl.BlockSpec((1, tm, D), lambda b, i: (b, i, 0)), w_spec(0), w_spec(1), w_spec(2),
                  pl.BlockSpec((tm, HEAD_DIM), lambda b, i: (i, 0)),
                  pl.BlockSpec((tm, HEAD_DIM), lambda b, i: (i, 0))],
        out_specs=(out_spec, out_spec, out_spec),
        scratch_shapes=[] if d == 1 else [pltpu.VMEM((3 * tn // HEAD_DIM, tm, HEAD_DIM), F32)],
        compiler_params=_params(("parallel", "parallel")),
        name="dil_proj",
    )(u, w_qkv, w_qkv, w_qkv, cos, sin)


def _gate_proj_body(u_ref, w_ref, o_ref):
    acc = jnp.dot(u_ref[0], w_ref[...], preferred_element_type=F32)
    o_ref[0] = jax.nn.sigmoid(acc).astype(BF16)


def _gate_proj(u, w_g):
    B, S, D = u.shape
    N = w_g.shape[1]
    tm = _tile(S, 1024)
    tn = _tile(N, 1024)
    return pl.pallas_call(
        _gate_proj_body,
        out_shape=jax.ShapeDtypeStruct((B, S, N), BF16),
        grid=(B, S // tm, N // tn),
        in_specs=[pl.BlockSpec((1, tm, D), lambda b, i, j: (b, i, 0)),
                  pl.BlockSpec((D, tn), lambda b, i, j: (0, j))],
        out_specs=pl.BlockSpec((1, tm, tn), lambda b, i, j: (b, i, j)),
        compiler_params=_params(("parallel", "parallel", "arbitrary")),
        name="gate_proj",
    )(u, w_g)


def _scan_body(u_ref, wf_ref, b_ref, e_ref, carry_ref, *, chunk, n_heads):
    tm = u_ref.shape[1]

    @pl.when(pl.program_id(1) == 0)
    def _():
        carry_ref[...] = jnp.zeros(carry_ref.shape, F32)

    row = lax.broadcasted_iota(jnp.int32, (chunk, chunk), 0)
    col = lax.broadcasted_iota(jnp.int32, (chunk, chunk), 1)
    tri = jnp.where(col <= row, 1.0, 0.0).astype(BF16)
    pr = lax.broadcasted_iota(jnp.int32, (HEAD_DIM, HEAD_DIM), 0)
    pc = lax.broadcasted_iota(jnp.int32, (HEAD_DIM, HEAD_DIM), 1)
    place = [jnp.where((pc == N_PIECES * pr + k) & (pr < n_heads), 1.0, 0.0).astype(BF16)
             for k in range(N_PIECES)]

    t = jnp.dot(u_ref[0], wf_ref[...], preferred_element_type=F32) + b_ref[...]
    ls = jnp.minimum(t, 0.0) - jnp.log1p(jnp.exp(-jnp.abs(t)))
    carry = carry_ref[...]
    for c in range(tm // chunk):
        cs = carry
        for piece in _split3(ls[c * chunk:(c + 1) * chunk]):
            cs = cs + jnp.dot(tri, piece, preferred_element_type=F32)
        e = jnp.zeros((chunk, HEAD_DIM), F32)
        for piece, pmat in zip(_split3(cs * (-LOG2E)), place):
            e = e + jnp.dot(piece, pmat, preferred_element_type=F32)
        e_ref[0, c * chunk:(c + 1) * chunk, :] = e.astype(BF16)
        carry = cs[chunk - 1:chunk, :]
    carry_ref[...] = carry


def _forget_scan(u, w_f, b_f):
    B, S, D = u.shape
    H = b_f.shape[0]
    assert N_PIECES * H <= HEAD_DIM
    tm = _tile(S, 1024)
    chunk = _tile(tm, 256)
    return pl.pallas_call(
        functools.partial(_scan_body, chunk=chunk, n_heads=H),
        out_shape=jax.ShapeDtypeStruct((B, S, HEAD_DIM), BF16),
        grid=(B, S // tm),
        in_specs=[pl.BlockSpec((1, tm, D), lambda b, i: (b, i, 0)),
                  pl.BlockSpec(w_f.shape, lambda b, i: (0, 0)),
                  pl.BlockSpec((1, HEAD_DIM), lambda b, i: (0, 0))],
        out_specs=pl.BlockSpec((1, tm, HEAD_DIM), lambda b, i: (b, i, 0)),
        scratch_shapes=[pltpu.VMEM((1, HEAD_DIM), F32)],
        compiler_params=_params(("parallel", "arbitrary")),
        name="forget_scan",
    )(u, w_f, jnp.pad(b_f, (0, HEAD_DIM - H)).reshape(1, HEAD_DIM))


def _fox_body(qt_ref, k_ref, e_ref, vt_ref, o_ref, m_ref, acc_ref, sa_ref, sb_ref, *, tq, hg):
    h0 = pl.program_id(1) * hg
    qi = pl.program_id(2)
    r = lax.broadcasted_iota(jnp.int32, (HEAD_DIM, tq), 0)
    qts = []
    for hh in range(hg):
        lo = N_PIECES * (h0 + hh)
        sel = jnp.where((r >= lo) & (r < lo + N_PIECES), 1.0, 0.0).astype(BF16)
        qts.append(jnp.concatenate([qt_ref[0, hh], sel], axis=0))

    m_ref[...] = jnp.full(m_ref.shape, MASK_VALUE, F32)
    acc_ref[...] = jnp.zeros(acc_ref.shape, F32)

    def scores(j, dst):
        k0 = pl.multiple_of(j * tq, tq)
        e = e_ref[0, pl.ds(k0, tq), :]
        for hh in range(hg):
            cols = slice(hh * HEAD_DIM, (hh + 1) * HEAD_DIM)
            kk = jnp.concatenate([k_ref[0, pl.ds(k0, tq), cols], e], axis=1)
            dst[hh] = jnp.dot(kk, qts[hh], preferred_element_type=F32)

    def softmax(src, mask):
        ps, alphas = [], []
        for hh in range(hg):
            s = src[hh] if mask is None else jnp.where(mask, src[hh], MASK_VALUE)
            m_prev = m_ref[hh]
            m_new = jnp.maximum(m_prev, jnp.max(s, axis=0, keepdims=True))
            alphas.append(jnp.exp2(m_prev - m_new))
            ps.append(jnp.exp2((s - m_new).astype(BF16)))
            m_ref[hh] = m_new
        return ps, alphas

    def values(j, ps, alphas):
        k0 = pl.multiple_of(j * tq, tq)
        for hh in range(hg):
            pv = jnp.dot(vt_ref[0, hh, :, pl.ds(k0, tq)], ps[hh], preferred_element_type=F32)
            acc_ref[hh] = alphas[hh] * acc_ref[hh] + pv

    def step(j, src, dst):
        ps, alphas = softmax(src, None)
        scores(j + 1, dst)
        values(j, ps, alphas)

    def last(src):
        key = lax.broadcasted_iota(jnp.int32, (tq, tq), 0)
        qry = lax.broadcasted_iota(jnp.int32, (tq, tq), 1)
        ps, alphas = softmax(src, key <= qry)
        values(qi, ps, alphas)

    scores(0, sa_ref)

    def two_steps(t, carry):
        step(2 * t, sa_ref, sb_ref)
        step(2 * t + 1, sb_ref, sa_ref)
        return carry

    lax.fori_loop(0, lax.div(qi, 2), two_steps, 0)

    @pl.when(lax.rem(qi, 2) == 0)
    def _():
        last(sa_ref)

    @pl.when(lax.rem(qi, 2) == 1)
    def _():
        step(qi - 1, sa_ref, sb_ref)
        last(sb_ref)

    for hh in range(hg):
        o = acc_ref[hh, 0:HEAD_DIM, :] / acc_ref[hh, HEAD_DIM:HEAD_DIM + 1, :]
        o_ref[0, :, hh * HEAD_DIM:(hh + 1) * HEAD_DIM] = jnp.transpose(o).astype(o_ref.dtype)


def _fox(qt, k, e, vt):
    B, H, _, S = qt.shape
    v_rows = vt.shape[2]
    tq = _tile(S, 512)
    hg = _tile(H, 4)
    return pl.pallas_call(
        functools.partial(_fox_body, tq=tq, hg=hg),
        out_shape=jax.ShapeDtypeStruct((B, S, H * HEAD_DIM), BF16),
        grid=(B, H // hg, S // tq),
        in_specs=[
            pl.BlockSpec((1, hg, HEAD_DIM, tq), lambda b, h, i: (b, h, 0, i)),
            pl.BlockSpec((1, S, hg * HEAD_DIM), lambda b, h, i: (b, 0, h)),
            pl.BlockSpec((1, S, HEAD_DIM), lambda b, h, i: (b, 0, 0)),
            pl.BlockSpec((1, hg, v_rows, S), lambda b, h, i: (b, h, 0, 0)),
        ],
        out_specs=pl.BlockSpec((1, tq, hg * HEAD_DIM), lambda b, h, i: (b, i, h)),
        scratch_shapes=[pltpu.VMEM((hg, 1, tq), F32), pltpu.VMEM((hg, v_rows, tq), F32),
                        pltpu.VMEM((hg, tq, tq), F32), pltpu.VMEM((hg, tq, tq), F32)],
        compiler_params=_params(("parallel", "parallel", "arbitrary")),
        name="fox_attention",
    )(qt, k, e, vt)


def _dil_body(*refs, dils, span):
    ng = len(dils)
    q_refs, k_refs, v_refs = refs[:ng], refs[ng:2 * ng], refs[2 * ng:3 * ng]
    o_ref, o_scr, lse_scr = refs[3 * ng:]
    nk = DIL_BLOCK
    i = pl.program_id(2)
    qi = lax.broadcasted_iota(jnp.int32, (nk, 2 * nk), 0)
    ki = lax.broadcasted_iota(jnp.int32, (nk, 2 * nk), 1)
    band = (ki >= qi) & (ki <= qi + nk)

    for g, d in enumerate(dils):
        rows = span // d
        nblk = rows // nk
        q_ref, k_ref, v_ref = q_refs[g], k_refs[g], v_refs[g]

        def several(it, carry, g=g, d=d, rows=rows, nblk=nblk, q_ref=q_ref, k_ref=k_ref, v_ref=v_ref):
            where, ss = [], []
            for j in range(DIL_UNROLL):
                idx = it * DIL_UNROLL + j
                r = lax.div(idx, nblk)
                n = lax.rem(idx, nblk)
                row0 = i * rows + n * nk
                prev0 = pl.multiple_of(jnp.maximum(row0 - nk, 0), nk)
                cur0 = pl.multiple_of(row0, nk)
                q = q_ref[0, r, pl.ds(pl.multiple_of(n * nk, nk), nk), :]
                kk = jnp.concatenate([k_ref[0, r, pl.ds(prev0, nk), :], k_ref[0, r, pl.ds(cur0, nk), :]],
                                     axis=0)
                s = lax.dot_general(q, kk, (((1,), (1,)), ((), ())), preferred_element_type=F32)
                ss.append(jnp.where(band & ((ki >= nk) | (row0 > 0)), s, MASK_VALUE))
                where.append((r, n, prev0, cur0))
            ps, ls, lses = [], [], []
            for s in ss:
                m = jnp.max(s, axis=1, keepdims=True)
                p = jnp.exp(s - m)
                l = jnp.sum(p, axis=1, keepdims=True)
                ps.append(p.astype(BF16))
                ls.append(l)
                lses.append(jnp.broadcast_to(m + jnp.log(l), (nk, HEAD_DIM)))
            for (r, n, prev0, cur0), p, l, lse in zip(where, ps, ls, lses):
                vv = jnp.concatenate([v_ref[0, r, pl.ds(prev0, nk), :], v_ref[0, r, pl.ds(cur0, nk), :]],
                                     axis=0)
                o = jnp.dot(p, vv, preferred_element_type=F32) / l
                tok0 = n * (nk * d) + r
                if d == 1:
                    o_scr[g, pl.ds(pl.multiple_of(tok0, nk), nk), :] = o
                    lse_scr[g, pl.ds(pl.multiple_of(tok0, nk), nk), :] = lse
                else:
                    o_scr[g, pl.ds(tok0, nk, stride=d), :] = o
                    lse_scr[g, pl.ds(tok0, nk, stride=d), :] = lse
            return carry

        assert (d * nblk) % DIL_UNROLL == 0
        lax.fori_loop(0, d * nblk // DIL_UNROLL, several, 0)

    lse = [lse_scr[g] for g in range(ng)]
    top = functools.reduce(jnp.maximum, lse)
    w = [jnp.exp(v - top) for v in lse]
    den = functools.reduce(lambda a, b: a + b, w)
    num = functools.reduce(lambda a, b: a + b, [w[g] * o_scr[g] for g in range(ng)])
    o_ref[0] = (num / den).astype(o_ref.dtype)


def _dilated(qs, ks, vs):
    B, d0, L0, W = qs[0].shape
    S = d0 * L0
    dils = tuple(t.shape[1] for t in qs)
    span = _tile(S, DIL_SPAN)
    assert all(span % (d * DIL_BLOCK) == 0 for d in dils)
    q_specs = [pl.BlockSpec((1, d, span // d, HEAD_DIM), lambda b, h, i: (b, 0, i, h)) for d in dils]
    kv_specs = [pl.BlockSpec((1, d, S // d, HEAD_DIM), lambda b, h, i: (b, 0, 0, h)) for d in dils]
    ng = len(dils)
    return pl.pallas_call(
        functools.partial(_dil_body, dils=dils, span=span),
        out_shape=jax.ShapeDtypeStruct((B, S, W), BF16),
        grid=(B, W // HEAD_DIM, S // span),
        in_specs=q_specs + kv_specs + kv_specs,
        out_specs=pl.BlockSpec((1, span, HEAD_DIM), lambda b, h, i: (b, i, h)),
        scratch_shapes=[pltpu.VMEM((ng, span, HEAD_DIM), F32), pltpu.VMEM((ng, span, HEAD_DIM), F32)],
        compiler_params=_params(("parallel", "parallel", "arbitrary")),
        name="dilated_attention",
    )(*qs, *ks, *vs)


def _merge_body(x_ref, g_ref, oa_ref, ob_ref, ga_ref, gb_ref, wa_ref, wb_ref, wo_ref, out_ref, un_ref, *,
                gi, gni):
    y_a = jnp.dot(oa_ref[0], wa_ref[...], preferred_element_type=F32)
    y_b = jnp.dot(ob_ref[0], wb_ref[...], preferred_element_type=F32)
    merged = ga_ref[0].astype(F32) * y_a + gb_ref[0].astype(F32) * y_b
    y = jnp.dot(merged.astype(BF16), wo_ref[...], preferred_element_type=F32)
    x = x_ref[0] + _rms(y, g_ref[gi:gi + 1, :])
    out_ref[0] = x
    un_ref[0] = _rms(x, g_ref[gni:gni + 1, :]).astype(BF16)


def _merge(x, g, o_a, o_b, gates, w_br_a, w_br_b, w_out, gi, gni):
    B, S, D = x.shape
    tm = _tile(S, 256)
    row = lambda w, c=0: pl.BlockSpec((1, tm, w), lambda b, i: (b, i, c))
    full = lambda a: pl.BlockSpec(a.shape, lambda b, i: (0, 0))
    return pl.pallas_call(
        functools.partial(_merge_body, gi=gi, gni=gni),
        out_shape=(jax.ShapeDtypeStruct((B, S, D), F32), jax.ShapeDtypeStruct((B, S, D), BF16)),
        grid=(B, S // tm),
        in_specs=[row(D), full(g), row(o_a.shape[-1]), row(o_b.shape[-1]), row(D, 0), row(D, 1),
                  full(w_br_a), full(w_br_b), full(w_out)],
        out_specs=(row(D), row(D)),
        compiler_params=_params(("parallel", "parallel")),
        name="mixer_merge",
    )(x, g, o_a, o_b, gates, gates, w_br_a, w_br_b, w_out)


def _ple_body(x_ref, u_ref, g_ref, gn_ref, p_ref, wp_ref, wg_ref, out_ref, un_ref, *, gi, gni):
    e = jnp.dot(p_ref[0, 0].astype(BF16), wp_ref[...], preferred_element_type=F32)
    gate = jax.nn.sigmoid(jnp.dot(u_ref[0], wg_ref[...], preferred_element_type=F32))
    x = x_ref[0] + _rms(gate * e, g_ref[gi:gi + 1, :])
    out_ref[0] = x
    un_ref[0] = _rms(x, gn_ref[gni:gni + 1, :]).astype(BF16)


def _ple(x, u, g, gn, p, layer, w_proj, w_gate, gi, gni):
    B, S, D = x.shape
    P = p.shape[-1]
    tm = _tile(S, 512)
    full = lambda a: pl.BlockSpec(a.shape, lambda b, i: (0, 0))
    row = pl.BlockSpec((1, tm, D), lambda b, i: (b, i, 0))
    return pl.pallas_call(
        functools.partial(_ple_body, gi=gi, gni=gni),
        out_shape=(jax.ShapeDtypeStruct((B, S, D), F32), jax.ShapeDtypeStruct((B, S, D), BF16)),
        grid=(B, S // tm),
        in_specs=[row, row, full(g), full(gn),
                  pl.BlockSpec((1, 1, tm, P), lambda b, i: (layer, b, i, 0)),
                  full(w_proj), full(w_gate)],
        out_specs=(row, row),
        compiler_params=_params(("parallel", "parallel")),
        name="ple_gate",
    )(x, u, g, gn, p, w_proj, w_gate)


def _rope_tables(S):
    half = HEAD_DIM // 2
    inv = ROPE_THETA ** (-jnp.arange(half, dtype=F32) * 2.0 / HEAD_DIM)
    ang = jnp.arange(S, dtype=F32)[:, None] * inv[None, :]
    cos, sin = jnp.cos(ang), jnp.sin(ang)
    return jnp.concatenate([cos, cos], axis=1), jnp.concatenate([-sin, sin], axis=1)


def _mixer(x, u, g, w_in, b_f, w_br_a, w_br_b, w_out, cos, sin):
    B, S, D = x.shape
    fox_w = w_br_a.shape[0]
    dil_out_w = w_br_b.shape[0]
    n_groups = len(DIL_PATTERNS)
    dil_w = n_groups * dil_out_w
    n_fox = b_f.shape[0]
    assert fox_w == n_fox * HEAD_DIM
    assert w_in.shape[1] == 3 * fox_w + n_fox + 3 * dil_w + 2 * D
    for window, d in DIL_PATTERNS:
        assert window // d == DIL_BLOCK

    f_lo = 3 * fox_w
    b_lo = f_lo + n_fox
    g_lo = b_lo + 3 * dil_w
    w_fox = w_in[:, :f_lo].astype(BF16)
    w_f = jnp.pad(w_in[:, f_lo:b_lo], ((0, 0), (0, HEAD_DIM - n_fox))).astype(BF16)
    w_dil = w_in[:, b_lo:g_lo].astype(BF16)
    w_gate = w_in[:, g_lo:].astype(BF16)

    qt, k, vt = _fox_proj(u, w_fox, n_fox)
    e = _forget_scan(u, w_f, b_f)
    o_a = _fox(qt, k, e, vt)

    qs, ks, vs = [], [], []
    for gidx, (_, d) in enumerate(DIL_PATTERNS):
        q, kk, v = _dil_proj(u, w_dil, cos, sin, gidx, d, n_groups, dil_out_w)
        qs.append(q), ks.append(kk), vs.append(v)
    o_b = _dilated(qs, ks, vs)

    gates = _gate_proj(u, w_gate)
    return _merge(x, g, o_a, o_b, gates, w_br_a.astype(BF16), w_br_b.astype(BF16), w_out.astype(BF16), 3, 4)


def kernel(x, p, norm_g, ffn1_w_in, ffn1_w_out, mix_w_in, fox_b_f, mix_w_br_a, mix_w_br_b, mix_w_out,
           ffn2_w_in, ffn2_w_out, ple_w_proj, ple_w_gate):
    B, S, D = x.shape
    T = B * S
    depth = norm_g.shape[0]
    cos, sin = _rope_tables(S)
    w1_in, w1_out, w2_in, w2_out = (_to_bf16(w) for w in (ffn1_w_in, ffn1_w_out, ffn2_w_in, ffn2_w_out))
    x = x.reshape(T, D)
    u = _norm(x, norm_g[0], 0)
    for i in range(depth):
        g = norm_g[i]
        gn = norm_g[(i + 1) % depth]
        x, u = _ffn(u, x, g, g, w1_in, w1_out, i, 1, 2)
        x, u = _mixer(x.reshape(B, S, D), u.reshape(B, S, D), g, mix_w_in[i], fox_b_f[i], mix_w_br_a[i],
                      mix_w_br_b[i], mix_w_out[i], cos, sin)
        x, u = _ffn(u.reshape(T, D), x.reshape(T, D), g, g, w2_in, w2_out, i, 5, 6)
        x, u = _ple(x.reshape(B, S, D), u.reshape(B, S, D), g, gn, p, i, ple_w_proj[i].astype(BF16),
                    ple_w_gate[i].astype(BF16), 7, 0)
        x, u = x.reshape(T, D), u.reshape(T, D)
    return x.reshape(B, S, D)
```

```python
import functools
import math

import jax
import jax.numpy as jnp
from jax import lax
from jax.experimental import pallas as pl
from jax.experimental.pallas import tpu as pltpu

HEAD_DIM = 128
DIL_PATTERNS = ((128, 1), (512, 4), (2048, 16))
DIL_BLOCK = 128
DIL_SPAN = 2048
DIL_UNROLL = 16
ROPE_THETA = 10000.0
RMS_EPS = 1e-6
MASK_VALUE = -1e30
LOG2E = math.log2(math.e)
N_PIECES = 3
CAST_BLOCK_ELEMS = 3 << 19
V_ONES_ROWS = 16
VMEM_LIMIT_V7X = 56 * 1024 * 1024

F32 = jnp.float32
BF16 = jnp.bfloat16


def _rms(x, g):
    return x * lax.rsqrt(jnp.mean(x * x, axis=-1, keepdims=True) + RMS_EPS) * g


def _tile(n, want):
    t = min(n, want)
    assert n % t == 0, (n, t)
    return t


def _params(sem, vmem=VMEM_LIMIT_V7X):
    return pltpu.CompilerParams(dimension_semantics=sem, vmem_limit_bytes=vmem)


def _split3(v):
    hi = v.astype(BF16)
    r = v - hi.astype(F32)
    mid = r.astype(BF16)
    lo = (r - mid.astype(F32)).astype(BF16)
    return hi, mid, lo


def _cast_body(w_ref, o_ref):
    o_ref[...] = w_ref[...].astype(BF16)


def _to_bf16(w):
    L, K, N = w.shape
    tk = _tile(K, max(16, 1 << int(math.log2(max(1, CAST_BLOCK_ELEMS // N)))))
    spec = pl.BlockSpec((1, tk, N), lambda l, i: (l, i, 0))
    return pl.pallas_call(
        _cast_body,
        out_shape=jax.ShapeDtypeStruct(w.shape, BF16),
        grid=(L, K // tk),
        in_specs=[spec],
        out_specs=spec,
        compiler_params=_params(("parallel", "parallel")),
        name="weight_cast",
    )(w)


def _norm_body(x_ref, g_ref, u_ref, *, gi):
    u_ref[...] = _rms(x_ref[...], g_ref[gi:gi + 1, :]).astype(BF16)


def _norm(x2, g, gi):
    T, D = x2.shape
    tm = _tile(T, 1024)
    return pl.pallas_call(
        functools.partial(_norm_body, gi=gi),
        out_shape=jax.ShapeDtypeStruct((T, D), BF16),
        grid=(T // tm,),
        in_specs=[pl.BlockSpec((tm, D), lambda i: (i, 0)), pl.BlockSpec(g.shape, lambda i: (0, 0))],
        out_specs=pl.BlockSpec((tm, D), lambda i: (i, 0)),
        compiler_params=_params(("parallel",)),
        name="first_norm",
    )(x2, g)


def _ffn_body(u_ref, x_ref, g_ref, gn_ref, wa_ref, wb_ref, wo_ref, o_ref, un_ref, acc_ref, *, gi, gni, nt):
    i = pl.program_id(0)
    j = pl.program_id(1)

    def swiglu_chunk():
        u = u_ref[...]
        a = jnp.dot(u, wa_ref[0], preferred_element_type=F32)
        b = jnp.dot(u, wb_ref[0], preferred_element_type=F32)
        h = (a * jax.nn.sigmoid(a) * b).astype(BF16)
        return jnp.dot(h, wo_ref[0], preferred_element_type=F32)

    def finish():
        x = x_ref[...] + 0.5 * _rms(acc_ref[...], g_ref[gi:gi + 1, :])
        o_ref[...] = x
        un_ref[...] = _rms(x, gn_ref[gni:gni + 1, :]).astype(BF16)

    @pl.when((j == 0) & (i == 0))
    def _():
        acc_ref[...] = swiglu_chunk()

    @pl.when((j == 0) & (i > 0) & (i < nt))
    def _():
        finish()
        acc_ref[...] = swiglu_chunk()

    @pl.when((j == 0) & (i == nt))
    def _():
        finish()

    @pl.when((j > 0) & (i < nt))
    def _():
        acc_ref[...] += swiglu_chunk()


def _ffn(u2, x2, g, gn, w_in, w_out, layer, gi, gni):
    T, D = x2.shape
    F = w_out.shape[1]
    tm = _tile(T, 512)
    tf = _tile(F, 512)
    nt = T // tm
    nj = F // tf
    cur = pl.BlockSpec((tm, D), lambda i, j: (jnp.minimum(i, nt - 1), 0))
    prev = pl.BlockSpec((tm, D), lambda i, j: (jnp.maximum(i - 1, 0), 0))
    chunk = lambda i, j: jnp.where(i < nt, j, nj - 1)
    return pl.pallas_call(
        functools.partial(_ffn_body, gi=gi, gni=gni, nt=nt),
        out_shape=(jax.ShapeDtypeStruct((T, D), F32), jax.ShapeDtypeStruct((T, D), BF16)),
        grid=(nt + 1, nj),
        in_specs=[
            cur, prev,
            pl.BlockSpec(g.shape, lambda i, j: (0, 0)),
            pl.BlockSpec(gn.shape, lambda i, j: (0, 0)),
            pl.BlockSpec((1, D, tf), lambda i, j: (layer, 0, chunk(i, j))),
            pl.BlockSpec((1, D, tf), lambda i, j: (layer, 0, chunk(i, j) + nj)),
            pl.BlockSpec((1, tf, D), lambda i, j: (layer, chunk(i, j), 0)),
        ],
        out_specs=(prev, prev),
        scratch_shapes=[pltpu.VMEM((tm, D), F32)],
        compiler_params=_params(("arbitrary", "arbitrary")),
        name="ffn",
    )(u2, x2, g, gn, w_in, w_in, w_out)


def _fox_proj_body(u_ref, wq_ref, wk_ref, wv_ref, qt_ref, k_ref, vt_ref, *, heads_per_chunk):
    u = u_ref[0]

    def transposed(t, out_ref):
        for c in range(heads_per_chunk):
            out_ref[0, c, 0:HEAD_DIM, :] = jnp.transpose(t[:, c * HEAD_DIM:(c + 1) * HEAD_DIM]).astype(BF16)

    q = jnp.dot(u, wq_ref[...], preferred_element_type=F32)
    transposed(q * (HEAD_DIM ** -0.5 * LOG2E), qt_ref)
    k_ref[0] = jnp.dot(u, wk_ref[...], preferred_element_type=F32).astype(BF16)
    transposed(jnp.dot(u, wv_ref[...], preferred_element_type=F32), vt_ref)
    for c in range(heads_per_chunk):
        vt_ref[0, c, HEAD_DIM:, :] = jnp.ones((V_ONES_ROWS, u.shape[0]), BF16)


def _fox_proj(u, w_qkv, n_heads):
    B, S, D = u.shape
    fox_w = n_heads * HEAD_DIM
    tm = _tile(S, 1024)
    tn = _tile(fox_w, 512)
    npk = fox_w // tn
    hpc = tn // HEAD_DIM
    t_spec = lambda rows: pl.BlockSpec((1, hpc, rows, tm), lambda b, i, j: (b, j, 0, i))
    t_shape = lambda rows: jax.ShapeDtypeStruct((B, n_heads, rows, S), BF16)
    v_rows = HEAD_DIM + V_ONES_ROWS
    return pl.pallas_call(
        functools.partial(_fox_proj_body, heads_per_chunk=hpc),
        out_shape=(t_shape(HEAD_DIM), jax.ShapeDtypeStruct((B, S, fox_w), BF16), t_shape(v_rows)),
        grid=(B, S // tm, npk),
        in_specs=[pl.BlockSpec((1, tm, D), lambda b, i, j: (b, i, 0)),
                  pl.BlockSpec((D, tn), lambda b, i, j: (0, j)),
                  pl.BlockSpec((D, tn), lambda b, i, j: (0, j + npk)),
                  pl.BlockSpec((D, tn), lambda b, i, j: (0, j + 2 * npk))],
        out_specs=(t_spec(HEAD_DIM), pl.BlockSpec((1, tm, tn), lambda b, i, j: (b, i, j)), t_spec(v_rows)),
        compiler_params=_params(("parallel", "parallel", "arbitrary")),
        name="fox_proj",
    )(u, w_qkv, w_qkv, w_qkv)


def _rope(t, cos, sin):
    return t * cos + pltpu.roll(t, HEAD_DIM // 2, 1) * sin


def _dil_proj_body(u_ref, wq_ref, wk_ref, wv_ref, cos_ref, sin_ref, q_ref, k_ref, v_ref, *scr, d, tm, tn):
    u = u_ref[0]
    heads = tn // HEAD_DIM
    for kind, (w_ref, out_ref) in enumerate(((wq_ref, q_ref), (wk_ref, k_ref), (wv_ref, v_ref))):
        t = jnp.dot(u, w_ref[...], preferred_element_type=F32)
        for c in range(heads):
            cols = slice(c * HEAD_DIM, (c + 1) * HEAD_DIM)
            sl = t[:, cols]
            if kind < 2:
                sl = _rope(sl, cos_ref[...], sin_ref[...])
            if kind == 0:
                sl = sl * HEAD_DIM ** -0.5
            if d == 1:
                out_ref[0, 0, :, cols] = sl.astype(BF16)
            else:
                slot = kind * heads + c
                scr[0][slot] = sl
                for r in range(d):
                    out_ref[0, r, :, cols] = scr[0][slot, pl.ds(r, tm // d, stride=d), :].astype(BF16)


def _dil_proj(u, w_qkv, cos, sin, group, d, n_groups, dil_out_w):
    B, S, D = u.shape
    tm = _tile(S, 1024)
    tn = dil_out_w
    assert tm % (d * 16) == 0
    shape = jax.ShapeDtypeStruct((B, d, S // d, tn), BF16)
    out_spec = pl.BlockSpec((1, d, tm // d, tn), lambda b, i: (b, 0, i, 0))
    w_spec = lambda kind: pl.BlockSpec((D, tn), lambda b, i: (0, kind * n_groups + group))
    return pl.pallas_call(
        functools.partial(_dil_proj_body, d=d, tm=tm, tn=tn),
        out_shape=(shape, shape, shape),
        grid=(B, S // tm),
        in_specs=[pl.BlockSpec((1, tm, D), lambda b, i: (b, i, 0)), w_spec(0), w_spec(1), w_spec(2),
                  pl.BlockSpec((tm, HEAD_DIM), lambda b, i: (i, 0)),
                  pl.BlockSpec((tm, HEAD_DIM), lambda b, i: (i, 0))],
        out_specs=(out_spec, out_spec, out_spec),
        scratch_shapes=[] if d == 1 else [pltpu.VMEM((3 * tn // HEAD_DIM, tm, HEAD_DIM), F32)],
        compiler_params=_params(("parallel", "parallel")),
        name="dil_proj",
    )(u, w_qkv, w_qkv, w_qkv, cos, sin)


def _gate_proj_body(u_ref, w_ref, o_ref):
    acc = jnp.dot(u_ref[0], w_ref[...], preferred_element_type=F32)
    o_ref[0] = jax.nn.sigmoid(acc).astype(BF16)


def _gate_proj(u, w_g):
    B, S, D = u.shape
    N = w_g.shape[1]
    tm = _tile(S, 1024)
    tn = _tile(N, 1024)
    return pl.pallas_call(
        _gate_proj_body,
        out_shape=jax.ShapeDtypeStruct((B, S, N), BF16),
        grid=(B, S // tm, N // tn),
        in_specs=[pl.BlockSpec((1, tm, D), lambda b, i, j: (b, i, 0)),
                  pl.BlockSpec((D, tn), lambda b, i, j: (0, j))],
        out_specs=pl.BlockSpec((1, tm, tn), lambda b, i, j: (b, i, j)),
        compiler_params=_params(("parallel", "parallel", "arbitrary")),
        name="gate_proj",
    )(u, w_g)


def _scan_body(u_ref, wf_ref, b_ref, e_ref, carry_ref, *, chunk, n_heads):
    tm = u_ref.shape[1]

    @pl.when(pl.program_id(1) == 0)
    def _():
        carry_ref[...] = jnp.zeros(carry_ref.shape, F32)

    row = lax.broadcasted_iota(jnp.int32, (chunk, chunk), 0)
    col = lax.broadcasted_iota(jnp.int32, (chunk, chunk), 1)
    tri = jnp.where(col <= row, 1.0, 0.0).astype(BF16)
    pr = lax.broadcasted_iota(jnp.int32, (HEAD_DIM, HEAD_DIM), 0)
    pc = lax.broadcasted_iota(jnp.int32, (HEAD_DIM, HEAD_DIM), 1)
    place = [jnp.where((pc == N_PIECES * pr + k) & (pr < n_heads), 1.0, 0.0).astype(BF16)
             for k in range(N_PIECES)]

    t = jnp.dot(u_ref[0], wf_ref[...], preferred_element_type=F32) + b_ref[...]
    ls = jnp.minimum(t, 0.0) - jnp.log1p(jnp.exp(-jnp.abs(t)))
    carry = carry_ref[...]
    for c in range(tm // chunk):
        cs = carry
        for piece in _split3(ls[c * chunk:(c + 1) * chunk]):
            cs = cs + jnp.dot(tri, piece, preferred_element_type=F32)
        e = jnp.zeros((chunk, HEAD_DIM), F32)
        for piece, pmat in zip(_split3(cs * (-LOG2E)), place):
            e = e + jnp.dot(piece, pmat, preferred_element_type=F32)
        e_ref[0, c * chunk:(c + 1) * chunk, :] = e.astype(BF16)
        carry = cs[chunk - 1:chunk, :]
    carry_ref[...] = carry


def _forget_scan(u, w_f, b_f):
    B, S, D = u.shape
    H = b_f.shape[0]
    assert N_PIECES * H <= HEAD_DIM
    tm = _tile(S, 1024)
    chunk = _tile(tm, 256)
    return pl.pallas_call(
        functools.partial(_scan_body, chunk=chunk, n_heads=H),
        out_shape=jax.ShapeDtypeStruct((B, S, HEAD_DIM), BF16),
        grid=(B, S // tm),
        in_specs=[pl.BlockSpec((1, tm, D), lambda b, i: (b, i, 0)),
                  pl.BlockSpec(w_f.shape, lambda b, i: (0, 0)),
                  pl.BlockSpec((1, HEAD_DIM), lambda b, i: (0, 0))],
        out_specs=pl.BlockSpec((1, tm, HEAD_DIM), lambda b, i: (b, i, 0)),
        scratch_shapes=[pltpu.VMEM((1, HEAD_DIM), F32)],
        compiler_params=_params(("parallel", "arbitrary")),
        name="forget_scan",
    )(u, w_f, jnp.pad(b_f, (0, HEAD_DIM - H)).reshape(1, HEAD_DIM))


def _fox_body(qt_ref, k_ref, e_ref, vt_ref, o_ref, m_ref, acc_ref, sa_ref, sb_ref, *, tq, hg):
    h0 = pl.program_id(1) * hg
    qi = pl.program_id(2)
    r = lax.broadcasted_iota(jnp.int32, (HEAD_DIM, tq), 0)
    qts = []
    for hh in range(hg):
        lo = N_PIECES * (h0 + hh)
        sel = jnp.where((r >= lo) & (r < lo + N_PIECES), 1.0, 0.0).astype(BF16)
        qts.append(jnp.concatenate([qt_ref[0, hh], sel], axis=0))

    m_ref[...] = jnp.full(m_ref.shape, MASK_VALUE, F32)
    acc_ref[...] = jnp.zeros(acc_ref.shape, F32)

    def score(j, dst, hh):
        k0 = pl.multiple_of(j * tq, tq)
        cols = slice(hh * HEAD_DIM, (hh + 1) * HEAD_DIM)
        kk = jnp.concatenate([k_ref[0, pl.ds(k0, tq), cols], e_ref[0, pl.ds(k0, tq), :]], axis=1)
        dst[hh] = jnp.dot(kk, qts[hh], preferred_element_type=F32)

    def scores(j, dst):
        for hh in range(hg):
            score(j, dst, hh)

    def softmax(src, mask):
        ps, alphas = [], []
        for hh in range(hg):
            s = src[hh] if mask is None else jnp.where(mask, src[hh], MASK_VALUE)
            m_prev = m_ref[hh]
            m_new = jnp.maximum(m_prev, jnp.max(s, axis=0, keepdims=True))
            alphas.append(jnp.exp2(m_prev - m_new))
            ps.append(jnp.exp2((s - m_new).astype(BF16)))
            m_ref[hh] = m_new
        return ps, alphas

    def value(j, ps, alphas, hh):
        k0 = pl.multiple_of(j * tq, tq)
        pv = jnp.dot(vt_ref[0, hh, :, pl.ds(k0, tq)], ps[hh], preferred_element_type=F32)
        acc_ref[hh] = alphas[hh] * acc_ref[hh] + pv

    def values(j, ps, alphas):
        for hh in range(hg):
            value(j, ps, alphas, hh)

    def step(j, src, dst):
        ps, alphas = softmax(src, None)
        lead = min(1, hg)
        for hh in range(lead):
            score(j + 1, dst, hh)
        for hh in range(hg):
            value(j, ps, alphas, hh)
            if hh + lead < hg:
                score(j + 1, dst, hh + lead)

    def last(src):
        key = lax.broadcasted_iota(jnp.int32, (tq, tq), 0)
        qry = lax.broadcasted_iota(jnp.int32, (tq, tq), 1)
        ps, alphas = softmax(src, key <= qry)
        values(qi, ps, alphas)

    scores(0, sa_ref)

    def two_steps(t, carry):
        step(2 * t, sa_ref, sb_ref)
        step(2 * t + 1, sb_ref, sa_ref)
        return carry

    lax.fori_loop(0, lax.div(qi, 2), two_steps, 0)

    @pl.when(lax.rem(qi, 2) == 0)
    def _():
        last(sa_ref)

    @pl.when(lax.rem(qi, 2) == 1)
    def _():
        step(qi - 1, sa_ref, sb_ref)
        last(sb_ref)

    for hh in range(hg):
        o = acc_ref[hh, 0:HEAD_DIM, :] / acc_ref[hh, HEAD_DIM:HEAD_DIM + 1, :]
        o_ref[0, :, hh * HEAD_DIM:(hh + 1) * HEAD_DIM] = jnp.transpose(o).astype(o_ref.dtype)


def _fox(qt, k, e, vt):
    B, H, _, S = qt.shape
    v_rows = vt.shape[2]
    tq = _tile(S, 512)
    hg = _tile(H, 4)
    return pl.pallas_call(
        functools.partial(_fox_body, tq=tq, hg=hg),
        out_shape=jax.ShapeDtypeStruct((B, S, H * HEAD_DIM), BF16),
        grid=(B, H // hg, S // tq),
        in_specs=[
            pl.BlockSpec((1, hg, HEAD_DIM, tq), lambda b, h, i: (b, h, 0, i)),
            pl.BlockSpec((1, S, hg * HEAD_DIM), lambda b, h, i: (b, 0, h)),
            pl.BlockSpec((1, S, HEAD_DIM), lambda b, h, i: (b, 0, 0)),
            pl.BlockSpec((1, hg, v_rows, S), lambda b, h, i: (b, h, 0, 0)),
        ],
        out_specs=pl.BlockSpec((1, tq, hg * HEAD_DIM), lambda b, h, i: (b, i, h)),
        scratch_shapes=[pltpu.VMEM((hg, 1, tq), F32), pltpu.VMEM((hg, v_rows, tq), F32),
                        pltpu.VMEM((hg, tq, tq), F32), pltpu.VMEM((hg, tq, tq), F32)],
        compiler_params=_params(("parallel", "parallel", "arbitrary")),
        name="fox_attention",
    )(qt, k, e, vt)


def _dil_body(*refs, dils, span):
    ng = len(dils)
    q_refs, k_refs, v_refs = refs[:ng], refs[ng:2 * ng], refs[2 * ng:3 * ng]
    o_ref, o_scr, lse_scr = refs[3 * ng:]
    nk = DIL_BLOCK
    i = pl.program_id(2)
    qi = lax.broadcasted_iota(jnp.int32, (nk, 2 * nk), 0)
    ki = lax.broadcasted_iota(jnp.int32, (nk, 2 * nk), 1)
    band = (ki >= qi) & (ki <= qi + nk)

    for g, d in enumerate(dils):
        rows = span // d
        nblk = rows // nk
        q_ref, k_ref, v_ref = q_refs[g], k_refs[g], v_refs[g]

        def several(it, carry, g=g, d=d, rows=rows, nblk=nblk, q_ref=q_ref, k_ref=k_ref, v_ref=v_ref):
            where, ss = [], []
            for j in range(DIL_UNROLL):
                idx = it * DIL_UNROLL + j
                r = lax.div(idx, nblk)
                n = lax.rem(idx, nblk)
                row0 = i * rows + n * nk
                prev0 = pl.multiple_of(jnp.maximum(row0 - nk, 0), nk)
                cur0 = pl.multiple_of(row0, nk)
                q = q_ref[0, r, pl.ds(pl.multiple_of(n * nk, nk), nk), :]
                kk = jnp.concatenate([k_ref[0, r, pl.ds(prev0, nk), :], k_ref[0, r, pl.ds(cur0, nk), :]],
                                     axis=0)
                s = lax.dot_general(q, kk, (((1,), (1,)), ((), ())), preferred_element_type=F32)
                ss.append(jnp.where(band & ((ki >= nk) | (row0 > 0)), s, MASK_VALUE))
                where.append((r, n, prev0, cur0))
            ps, ls, lses = [], [], []
            for s in ss:
                m = jnp.max(s, axis=1, keepdims=True)
                p = jnp.exp(s - m)
                l = jnp.sum(p, axis=1, keepdims=True)
                ps.append(p.astype(BF16))
                ls.append(l)
                lses.append(jnp.broadcast_to(m + jnp.log(l), (nk, HEAD_DIM)))
            for (r, n, prev0, cur0), p, l, lse in zip(where, ps, ls, lses):
                vv = jnp.concatenate([v_ref[0, r, pl.ds(prev0, nk), :], v_ref[0, r, pl.ds(cur0, nk), :]],
                                     axis=0)
                o = jnp.dot(p, vv, preferred_element_type=F32) / l
                tok0 = n * (nk * d) + r
                if d == 1:
                    o_scr[g, pl.ds(pl.multiple_of(tok0, nk), nk), :] = o
                    lse_scr[g, pl.ds(pl.multiple_of(tok0, nk), nk), :] = lse
                else:
                    o_scr[g, pl.ds(tok0, nk, stride=d), :] = o
                    lse_scr[g, pl.ds(tok0, nk, stride=d), :] = lse
            return carry

        assert (d * nblk) % DIL_UNROLL == 0
        lax.fori_loop(0, d * nblk // DIL_UNROLL, several, 0)

    lse = [lse_scr[g] for g in range(ng)]
    top = functools.reduce(jnp.maximum, lse)
    w = [jnp.exp(v - top) for v in lse]
    den = functools.reduce(lambda a, b: a + b, w)
    num = functools.reduce(lambda a, b: a + b, [w[g] * o_scr[g] for g in range(ng)])
    o_ref[0] = (num / den).astype(o_ref.dtype)


def _dilated(qs, ks, vs):
    B, d0, L0, W = qs[0].shape
    S = d0 * L0
    dils = tuple(t.shape[1] for t in qs)
    span = _tile(S, DIL_SPAN)
    assert all(span % (d * DIL_BLOCK) == 0 for d in dils)
    q_specs = [pl.BlockSpec((1, d, span // d, HEAD_DIM), lambda b, h, i: (b, 0, i, h)) for d in dils]
    kv_specs = [pl.BlockSpec((1, d, S // d, HEAD_DIM), lambda b, h, i: (b, 0, 0, h)) for d in dils]
    ng = len(dils)
    return pl.pallas_call(
        functools.partial(_dil_body, dils=dils, span=span),
        out_shape=jax.ShapeDtypeStruct((B, S, W), BF16),
        grid=(B, W // HEAD_DIM, S // span),
        in_specs=q_specs + kv_specs + kv_specs,
        out_specs=pl.BlockSpec((1, span, HEAD_DIM), lambda b, h, i: (b, i, h)),
        scratch_shapes=[pltpu.VMEM((ng, span, HEAD_DIM), F32), pltpu.VMEM((ng, span, HEAD_DIM), F32)],
        compiler_params=_params(("parallel", "parallel", "arbitrary")),
        name="dilated_attention",
    )(*qs, *ks, *vs)


def _merge_body(x_ref, g_ref, oa_ref, ob_ref, ga_ref, gb_ref, wa_ref, wb_ref, wo_ref, out_ref, un_ref, *,
                gi, gni):
    y_a = jnp.dot(oa_ref[0], wa_ref[...], preferred_element_type=F32)
    y_b = jnp.dot(ob_ref[0], wb_ref[...], preferred_element_type=F32)
    merged = ga_ref[0].astype(F32) * y_a + gb_ref[0].astype(F32) * y_b
    y = jnp.dot(merged.astype(BF16), wo_ref[...], preferred_element_type=F32)
    x = x_ref[0] + _rms(y, g_ref[gi:gi + 1, :])
    out_ref[0] = x
    un_ref[0] = _rms(x, g_ref[gni:gni + 1, :]).astype(BF16)


def _merge(x, g, o_a, o_b, gates, w_br_a, w_br_b, w_out, gi, gni):
    B, S, D = x.shape
    tm = _tile(S, 256)
    row = lambda w, c=0: pl.BlockSpec((1, tm, w), lambda b, i: (b, i, c))
    full = lambda a: pl.BlockSpec(a.shape, lambda b, i: (0, 0))
    return pl.pallas_call(
        functools.partial(_merge_body, gi=gi, gni=gni),
        out_shape=(jax.ShapeDtypeStruct((B, S, D), F32), jax.ShapeDtypeStruct((B, S, D), BF16)),
        grid=(B, S // tm),
        in_specs=[row(D), full(g), row(o_a.shape[-1]), row(o_b.shape[-1]), row(D, 0), row(D, 1),
                  full(w_br_a), full(w_br_b), full(w_out)],
        out_specs=(row(D), row(D)),
        compiler_params=_params(("parallel", "parallel")),
        name="mixer_merge",
    )(x, g, o_a, o_b, gates, gates, w_br_a, w_br_b, w_out)


def _ple_body(x_ref, u_ref, g_ref, gn_ref, p_ref, wp_ref, wg_ref, out_ref, un_ref, *, gi, gni):
    e = jnp.dot(p_ref[0, 0].astype(BF16), wp_ref[...], preferred_element_type=F32)
    gate = jax.nn.sigmoid(jnp.dot(u_ref[0], wg_ref[...], preferred_element_type=F32))
    x = x_ref[0] + _rms(gate * e, g_ref[gi:gi + 1, :])
    out_ref[0] = x
    un_ref[0] = _rms(x, gn_ref[gni:gni + 1, :]).astype(BF16)


def _ple(x, u, g, gn, p, layer, w_proj, w_gate, gi, gni):
    B, S, D = x.shape
    P = p.shape[-1]
    tm = _tile(S, 512)
    full = lambda a: pl.BlockSpec(a.shape, lambda b, i: (0, 0))
    row = pl.BlockSpec((1, tm, D), lambda b, i: (b, i, 0))
    return pl.pallas_call(
        functools.partial(_ple_body, gi=gi, gni=gni),
        out_shape=(jax.ShapeDtypeStruct((B, S, D), F32), jax.ShapeDtypeStruct((B, S, D), BF16)),
        grid=(B, S // tm),
        in_specs=[row, row, full(g), full(gn),
                  pl.BlockSpec((1, 1, tm, P), lambda b, i: (layer, b, i, 0)),
                  full(w_proj), full(w_gate)],
        out_specs=(row, row),
        compiler_params=_params(("parallel", "parallel")),
        name="ple_gate",
    )(x, u, g, gn, p, w_proj, w_gate)


def _rope_tables(S):
    half = HEAD_DIM // 2
    inv = ROPE_THETA ** (-jnp.arange(half, dtype=F32) * 2.0 / HEAD_DIM)
    ang = jnp.arange(S, dtype=F32)[:, None] * inv[None, :]
    cos, sin = jnp.cos(ang), jnp.sin(ang)
    return jnp.concatenate([cos, cos], axis=1), jnp.concatenate([-sin, sin], axis=1)


def _mixer(x, u, g, w_in, b_f, w_br_a, w_br_b, w_out, cos, sin):
    B, S, D = x.shape
    fox_w = w_br_a.shape[0]
    dil_out_w = w_br_b.shape[0]
    n_groups = len(DIL_PATTERNS)
    dil_w = n_groups * dil_out_w
    n_fox = b_f.shape[0]
    assert fox_w == n_fox * HEAD_DIM
    assert w_in.shape[1] == 3 * fox_w + n_fox + 3 * dil_w + 2 * D
    for window, d in DIL_PATTERNS:
        assert window // d == DIL_BLOCK

    f_lo = 3 * fox_w
    b_lo = f_lo + n_fox
    g_lo = b_lo + 3 * dil_w
    w_fox = w_in[:, :f_lo].astype(BF16)
    w_f = jnp.pad(w_in[:, f_lo:b_lo], ((0, 0), (0, HEAD_DIM - n_fox))).astype(BF16)
    w_dil = w_in[:, b_lo:g_lo].astype(BF16)
    w_gate = w_in[:, g_lo:].astype(BF16)

    qt, k, vt = _fox_proj(u, w_fox, n_fox)
    e = _forget_scan(u, w_f, b_f)
    o_a = _fox(qt, k, e, vt)

    qs, ks, vs = [], [], []
    for gidx, (_, d) in enumerate(DIL_PATTERNS):
        q, kk, v = _dil_proj(u, w_dil, cos, sin, gidx, d, n_groups, dil_out_w)
        qs.append(q), ks.append(kk), vs.append(v)
    o_b = _dilated(qs, ks, vs)

    gates = _gate_proj(u, w_gate)
    return _merge(x, g, o_a, o_b, gates, w_br_a.astype(BF16), w_br_b.astype(BF16), w_out.astype(BF16), 3, 4)


def kernel(x, p, norm_g, ffn1_w_in, ffn1_w_out, mix_w_in, fox_b_f, mix_w_br_a, mix_w_br_b, mix_w_out,
           ffn2_w_in, ffn2_w_out, ple_w_proj, ple_w_gate):
    B, S, D = x.shape
    T = B * S
    depth = norm_g.shape[0]
    cos, sin = _rope_tables(S)
    w1_in, w1_out, w2_in, w2_out = (_to_bf16(w) for w in (ffn1_w_in, ffn1_w_out, ffn2_w_in, ffn2_w_out))
    x = x.reshape(T, D)
    u = _norm(x, norm_g[0], 0)
    for i in range(depth):
        g = norm_g[i]
        gn = norm_g[(i + 1) % depth]
        x, u = _ffn(u, x, g, g, w1_in, w1_out, i, 1, 2)
        x, u = _mixer(x.reshape(B, S, D), u.reshape(B, S, D), g, mix_w_in[i], fox_b_f[i], mix_w_br_a[i],
                      mix_w_br_b[i], mix_w_out[i], cos, sin)
        x, u = _ffn(u.reshape(T, D), x.reshape(T, D), g, g, w2_in, w2_out, i, 5, 6)
        x, u = _ple(x.reshape(B, S, D), u.reshape(B, S, D), g, gn, p, i, ple_w_proj[i].astype(BF16),
                    ple_w_gate[i].astype(BF16), 7, 0)
        x, u = x.reshape(T, D), u.reshape(T, D)
    return x.reshape(B, S, D)
```

```python
import functools
import math

import jax
import jax.numpy as jnp
from jax import lax
from jax.experimental import pallas as pl
from jax.experimental.pallas import tpu as pltpu

HEAD_DIM = 128
DIL_PATTERNS = ((128, 1), (512, 4), (2048, 16))
DIL_BLOCK = 128
DIL_SPAN = 2048
DIL_UNROLL = 16
ROPE_THETA = 10000.0
RMS_EPS = 1e-6
MASK_VALUE = -1e30
LOG2E = math.log2(math.e)
N_PIECES = 3
CAST_BLOCK_ELEMS = 3 << 19
V_ONES_ROWS = 16
VMEM_LIMIT_V7X = 56 * 1024 * 1024

F32 = jnp.float32
BF16 = jnp.bfloat16


def _rms(x, g):
    return x * lax.rsqrt(jnp.mean(x * x, axis=-1, keepdims=True) + RMS_EPS) * g


def _tile(n, want):
    t = min(n, want)
    assert n % t == 0, (n, t)
    return t


def _params(sem, vmem=VMEM_LIMIT_V7X):
    return pltpu.CompilerParams(dimension_semantics=sem, vmem_limit_bytes=vmem)


def _split3(v):
    hi = v.astype(BF16)
    r = v - hi.astype(F32)
    mid = r.astype(BF16)
    lo = (r - mid.astype(F32)).astype(BF16)
    return hi, mid, lo


def _cast_body(w_ref, o_ref):
    o_ref[...] = w_ref[...].astype(BF16)


def _to_bf16(w):
    L, K, N = w.shape
    tk = _tile(K, max(16, 1 << int(math.log2(max(1, CAST_BLOCK_ELEMS // N)))))
    spec = pl.BlockSpec((1, tk, N), lambda l, i: (l, i, 0))
    return pl.pallas_call(
        _cast_body,
        out_shape=jax.ShapeDtypeStruct(w.shape, BF16),
        grid=(L, K // tk),
        in_specs=[spec],
        out_specs=spec,
        compiler_params=_params(("parallel", "parallel")),
        name="weight_cast",
    )(w)


def _norm_body(x_ref, g_ref, u_ref, *, gi):
    u_ref[...] = _rms(x_ref[...], g_ref[gi:gi + 1, :]).astype(BF16)


def _norm(x2, g, gi):
    T, D = x2.shape
    tm = _tile(T, 1024)
    return pl.pallas_call(
        functools.partial(_norm_body, gi=gi),
        out_shape=jax.ShapeDtypeStruct((T, D), BF16),
        grid=(T // tm,),
        in_specs=[pl.BlockSpec((tm, D), lambda i: (i, 0)), pl.BlockSpec(g.shape, lambda i: (0, 0))],
        out_specs=pl.BlockSpec((tm, D), lambda i: (i, 0)),
        compiler_params=_params(("parallel",)),
        name="first_norm",
    )(x2, g)


def _ffn_body(u_ref, x_ref, g_ref, gn_ref, wa_ref, wb_ref, wo_ref, o_ref, un_ref, acc_ref, *, gi, gni, nt):
    i = pl.program_id(0)
    j = pl.program_id(1)

    def swiglu_chunk():
        u = u_ref[...]
        a = jnp.dot(u, wa_ref[0], preferred_element_type=F32)
        b = jnp.dot(u, wb_ref[0], preferred_element_type=F32)
        h = (a * jax.nn.sigmoid(a) * b).astype(BF16)
        return jnp.dot(h, wo_ref[0], preferred_element_type=F32)

    def finish():
        x = x_ref[...] + 0.5 * _rms(acc_ref[...], g_ref[gi:gi + 1, :])
        o_ref[...] = x
        un_ref[...] = _rms(x, gn_ref[gni:gni + 1, :]).astype(BF16)

    @pl.when((j == 0) & (i == 0))
    def _():
        acc_ref[...] = swiglu_chunk()

    @pl.when((j == 0) & (i > 0) & (i < nt))
    def _():
        finish()
        acc_ref[...] = swiglu_chunk()

    @pl.when((j == 0) & (i == nt))
    def _():
        finish()

    @pl.when((j > 0) & (i < nt))
    def _():
        acc_ref[...] += swiglu_chunk()


def _ffn(u2, x2, g, gn, w_in, w_out, layer, gi, gni):
    T, D = x2.shape
    F = w_out.shape[1]
    tm = _tile(T, 512)
    tf = _tile(F, 512)
    nt = T // tm
    nj = F // tf
    cur = pl.BlockSpec((tm, D), lambda i, j: (jnp.minimum(i, nt - 1), 0))
    prev = pl.BlockSpec((tm, D), lambda i, j: (jnp.maximum(i - 1, 0), 0))
    chunk = lambda i, j: jnp.where(i < nt, j, nj - 1)
    return pl.pallas_call(
        functools.partial(_ffn_body, gi=gi, gni=gni, nt=nt),
        out_shape=(jax.ShapeDtypeStruct((T, D), F32), jax.ShapeDtypeStruct((T, D), BF16)),
        grid=(nt + 1, nj),
        in_specs=[
            cur, prev,
            pl.BlockSpec(g.shape, lambda i, j: (0, 0)),
            pl.BlockSpec(gn.shape, lambda i, j: (0, 0)),
            pl.BlockSpec((1, D, tf), lambda i, j: (layer, 0, chunk(i, j))),
            pl.BlockSpec((1, D, tf), lambda i, j: (layer, 0, chunk(i, j) + nj)),
            pl.BlockSpec((1, tf, D), lambda i, j: (layer, chunk(i, j), 0)),
        ],
        out_specs=(prev, prev),
        scratch_shapes=[pltpu.VMEM((tm, D), F32)],
        compiler_params=_params(("arbitrary", "arbitrary")),
        name="ffn",
    )(u2, x2, g, gn, w_in, w_in, w_out)


def _fox_proj_body(u_ref, wq_ref, wk_ref, wv_ref, qt_ref, k_ref, vt_ref, *, heads_per_chunk):
    u = u_ref[0]

    def transposed(t, out_ref):
        for c in range(heads_per_chunk):
            out_ref[0, c, 0:HEAD_DIM, :] = jnp.transpose(t[:, c * HEAD_DIM:(c + 1) * HEAD_DIM]).astype(BF16)

    q = jnp.dot(u, wq_ref[...], preferred_element_type=F32)
    transposed(q * (HEAD_DIM ** -0.5 * LOG2E), qt_ref)
    transposed(jnp.dot(u, wv_ref[...], preferred_element_type=F32), vt_ref)
    for c in range(heads_per_chunk):
        vt_ref[0, c, HEAD_DIM:, :] = jnp.ones((V_ONES_ROWS, u.shape[0]), BF16)
    k_ref[0] = jnp.dot(u, wk_ref[...], preferred_element_type=F32).astype(BF16)


def _fox_proj(u, w_qkv, n_heads):
    B, S, D = u.shape
    fox_w = n_heads * HEAD_DIM
    tm = _tile(S, 1024)
    tn = _tile(fox_w, 512)
    npk = fox_w // tn
    hpc = tn // HEAD_DIM
    t_spec = lambda rows: pl.BlockSpec((1, hpc, rows, tm), lambda b, i, j: (b, j, 0, i))
    t_shape = lambda rows: jax.ShapeDtypeStruct((B, n_heads, rows, S), BF16)
    v_rows = HEAD_DIM + V_ONES_ROWS
    return pl.pallas_call(
        functools.partial(_fox_proj_body, heads_per_chunk=hpc),
        out_shape=(t_shape(HEAD_DIM), jax.ShapeDtypeStruct((B, S, fox_w), BF16), t_shape(v_rows)),
        grid=(B, S // tm, npk),
        in_specs=[pl.BlockSpec((1, tm, D), lambda b, i, j: (b, i, 0)),
                  pl.BlockSpec((D, tn), lambda b, i, j: (0, j)),
                  pl.BlockSpec((D, tn), lambda b, i, j: (0, j + npk)),
                  pl.BlockSpec((D, tn), lambda b, i, j: (0, j + 2 * npk))],
        out_specs=(t_spec(HEAD_DIM), pl.BlockSpec((1, tm, tn), lambda b, i, j: (b, i, j)), t_spec(v_rows)),
        compiler_params=_params(("parallel", "parallel", "arbitrary")),
        name="fox_proj",
    )(u, w_qkv, w_qkv, w_qkv)


def _rope(t, cos, sin):
    return t * cos + pltpu.roll(t, HEAD_DIM // 2, 1) * sin


def _dil_proj_body(u_ref, wq_ref, wk_ref, wv_ref, cos_ref, sin_ref, q_ref, k_ref, v_ref, *scr, d, tm, tn):
    u = u_ref[0]
    heads = tn // HEAD_DIM
    for kind, (w_ref, out_ref) in enumerate(((wq_ref, q_ref), (wk_ref, k_ref), (wv_ref, v_ref))):
        t = jnp.dot(u, w_ref[...], preferred_element_type=F32)
        for c in range(heads):
            cols = slice(c * HEAD_DIM, (c + 1) * HEAD_DIM)
            sl = t[:, cols]
            if kind < 2:
                sl = _rope(sl, cos_ref[...], sin_ref[...])
            if kind == 0:
                sl = sl * HEAD_DIM ** -0.5
            if d == 1:
                out_ref[0, 0, :, cols] = sl.astype(BF16)
            else:
                slot = kind * heads + c
                scr[0][slot] = sl
                for r in range(d):
                    out_ref[0, r, :, cols] = scr[0][slot, pl.ds(r, tm // d, stride=d), :].astype(BF16)


def _dil_proj(u, w_qkv, cos, sin, group, d, n_groups, dil_out_w):
    B, S, D = u.shape
    tm = _tile(S, 1024)
    tn = dil_out_w
    assert tm % (d * 16) == 0
    shape = jax.ShapeDtypeStruct((B, d, S // d, tn), BF16)
    out_spec = pl.BlockSpec((1, d, tm // d, tn), lambda b, i: (b, 0, i, 0))
    w_spec = lambda kind: pl.BlockSpec((D, tn), lambda b, i: (0, kind * n_groups + group))
    return pl.pallas_call(
        functools.partial(_dil_proj_body, d=d, tm=tm, tn=tn),
        out_shape=(shape, shape, shape),
        grid=(B, S // tm),
        in_specs=[pl.BlockSpec((1, tm, D), lambda b, i: (b, i, 0)), w_spec(0), w_spec(1), w_spec(2),
                  pl.BlockSpec((tm, HEAD_DIM), lambda b, i: (i, 0)),
                  pl.BlockSpec((tm, HEAD_DIM), lambda b, i: (i, 0))],
        out_specs=(out_spec, out_spec, out_spec),
        scratch_shapes=[] if d == 1 else [pltpu.VMEM((3 * tn // HEAD_DIM, tm, HEAD_DIM), F32)],
        compiler_params=_params(("parallel", "parallel")),
        name="dil_proj",
    )(u, w_qkv, w_qkv, w_qkv, cos, sin)


def _gate_proj_body(u_ref, w_ref, o_ref):
    acc = jnp.dot(u_ref[0], w_ref[...], preferred_element_type=F32)
    o_ref[0] = jax.nn.sigmoid(acc).astype(BF16)


def _gate_proj(u, w_g):
    B, S, D = u.shape
    N = w_g.shape[1]
    tm = _tile(S, 1024)
    tn = _tile(N, 1024)
    return pl.pallas_call(
        _gate_proj_body,
        out_shape=jax.ShapeDtypeStruct((B, S, N), BF16),
        grid=(B, S // tm, N // tn),
        in_specs=[pl.BlockSpec((1, tm, D), lambda b, i, j: (b, i, 0)),
                  pl.BlockSpec((D, tn), lambda b, i, j: (0, j))],
        out_specs=pl.BlockSpec((1, tm, tn), lambda b, i, j: (b, i, j)),
        compiler_params=_params(("parallel", "parallel", "arbitrary")),
        name="gate_proj",
    )(u, w_g)


def _scan_body(u_ref, wf_ref, b_ref, e_ref, carry_ref, *, chunk, n_heads):
    tm = u_ref.shape[1]

    @pl.when(pl.program_id(1) == 0)
    def _():
        carry_ref[...] = jnp.zeros(carry_ref.shape, F32)

    row = lax.broadcasted_iota(jnp.int32, (chunk, chunk), 0)
    col = lax.broadcasted_iota(jnp.int32, (chunk, chunk), 1)
    tri = jnp.where(col <= row, 1.0, 0.0).astype(BF16)
    pr = lax.broadcasted_iota(jnp.int32, (HEAD_DIM, HEAD_DIM), 0)
    pc = lax.broadcasted_iota(jnp.int32, (HEAD_DIM, HEAD_DIM), 1)
    place = [jnp.where((pc == N_PIECES * pr + k) & (pr < n_heads), 1.0, 0.0).astype(BF16)
             for k in range(N_PIECES)]

    t = jnp.dot(u_ref[0], wf_ref[...], preferred_element_type=F32) + b_ref[...]
    ls = jnp.minimum(t, 0.0) - jnp.log1p(jnp.exp(-jnp.abs(t)))
    carry = carry_ref[...]
    for c in range(tm // chunk):
        cs = carry
        for piece in _split3(ls[c * chunk:(c + 1) * chunk]):
            cs = cs + jnp.dot(tri, piece, preferred_element_type=F32)
        e = jnp.zeros((chunk, HEAD_DIM), F32)
        for piece, pmat in zip(_split3(cs * (-LOG2E)), place):
            e = e + jnp.dot(piece, pmat, preferred_element_type=F32)
        e_ref[0, c * chunk:(c + 1) * chunk, :] = e.astype(BF16)
        carry = cs[chunk - 1:chunk, :]
    carry_ref[...] = carry


def _forget_scan(u, w_f, b_f):
    B, S, D = u.shape
    H = b_f.shape[0]
    assert N_PIECES * H <= HEAD_DIM
    tm = _tile(S, 1024)
    chunk = _tile(tm, 256)
    return pl.pallas_call(
        functools.partial(_scan_body, chunk=chunk, n_heads=H),
        out_shape=jax.ShapeDtypeStruct((B, S, HEAD_DIM), BF16),
        grid=(B, S // tm),
        in_specs=[pl.BlockSpec((1, tm, D), lambda b, i: (b, i, 0)),
                  pl.BlockSpec(w_f.shape, lambda b, i: (0, 0)),
                  pl.BlockSpec((1, HEAD_DIM), lambda b, i: (0, 0))],
        out_specs=pl.BlockSpec((1, tm, HEAD_DIM), lambda b, i: (b, i, 0)),
        scratch_shapes=[pltpu.VMEM((1, HEAD_DIM), F32)],
        compiler_params=_params(("parallel", "arbitrary")),
        name="forget_scan",
    )(u, w_f, jnp.pad(b_f, (0, HEAD_DIM - H)).reshape(1, HEAD_DIM))


def _fox_body(qt_ref, k_ref, e_ref, vt_ref, o_ref, m_ref, acc_ref, sa_ref, sb_ref, *, tq, hg):
    h0 = pl.program_id(1) * hg
    qi = pl.program_id(2)
    r = lax.broadcasted_iota(jnp.int32, (HEAD_DIM, tq), 0)
    qts = []
    for hh in range(hg):
        lo = N_PIECES * (h0 + hh)
        sel = jnp.where((r >= lo) & (r < lo + N_PIECES), 1.0, 0.0).astype(BF16)
        qts.append(jnp.concatenate([qt_ref[0, hh], sel], axis=0))

    m_ref[...] = jnp.full(m_ref.shape, MASK_VALUE, F32)
    acc_ref[...] = jnp.zeros(acc_ref.shape, F32)

    def score(j, dst, hh):
        k0 = pl.multiple_of(j * tq, tq)
        cols = slice(hh * HEAD_DIM, (hh + 1) * HEAD_DIM)
        kk = jnp.concatenate([k_ref[0, pl.ds(k0, tq), cols], e_ref[0, pl.ds(k0, tq), :]], axis=1)
        dst[hh] = jnp.dot(kk, qts[hh], preferred_element_type=F32)

    def scores(j, dst):
        for hh in range(hg):
            score(j, dst, hh)

    def softmax(src, mask):
        ps, alphas = [], []
        for hh in range(hg):
            s = src[hh] if mask is None else jnp.where(mask, src[hh], MASK_VALUE)
            m_prev = m_ref[hh]
            m_new = jnp.maximum(m_prev, jnp.max(s, axis=0, keepdims=True))
            alphas.append(jnp.exp2(m_prev - m_new))
            ps.append(jnp.exp2((s - m_new).astype(BF16)))
            m_ref[hh] = m_new
        return ps, alphas

    def value(j, ps, alphas, hh):
        k0 = pl.multiple_of(j * tq, tq)
        pv = jnp.dot(vt_ref[0, hh, :, pl.ds(k0, tq)], ps[hh], preferred_element_type=F32)
        acc_ref[hh] = alphas[hh] * acc_ref[hh] + pv

    def values(j, ps, alphas):
        for hh in range(hg):
            value(j, ps, alphas, hh)

    def step(j, src, dst):
        ps, alphas = softmax(src, None)
        score(j + 1, dst, 0)
        for hh in range(hg):
            value(j, ps, alphas, hh)
            if hh + 1 < hg:
                score(j + 1, dst, hh + 1)

    def last(src):
        key = lax.broadcasted_iota(jnp.int32, (tq, tq), 0)
        qry = lax.broadcasted_iota(jnp.int32, (tq, tq), 1)
        ps, alphas = softmax(src, key <= qry)
        values(qi, ps, alphas)

    scores(0, sa_ref)

    def two_steps(t, carry):
        step(2 * t, sa_ref, sb_ref)
        step(2 * t + 1, sb_ref, sa_ref)
        return carry

    lax.fori_loop(0, lax.div(qi, 2), two_steps, 0)

    @pl.when(lax.rem(qi, 2) == 0)
    def _():
        last(sa_ref)

    @pl.when(lax.rem(qi, 2) == 1)
    def _():
        step(qi - 1, sa_ref, sb_ref)
        last(sb_ref)

    for hh in range(hg):
        o = acc_ref[hh, 0:HEAD_DIM, :] / acc_ref[hh, HEAD_DIM:HEAD_DIM + 1, :]
        o_ref[0, :, hh * HEAD_DIM:(hh + 1) * HEAD_DIM] = jnp.transpose(o).astype(o_ref.dtype)


def _fox(qt, k, e, vt):
    B, H, _, S = qt.shape
    v_rows = vt.shape[2]
    tq = _tile(S, 512)
    hg = _tile(H, 4)
    return pl.pallas_call(
        functools.partial(_fox_body, tq=tq, hg=hg),
        out_shape=jax.ShapeDtypeStruct((B, S, H * HEAD_DIM), BF16),
        grid=(B, H // hg, S // tq),
        in_specs=[
            pl.BlockSpec((1, hg, HEAD_DIM, tq), lambda b, h, i: (b, h, 0, i)),
            pl.BlockSpec((1, S, hg * HEAD_DIM), lambda b, h, i: (b, 0, h)),
            pl.BlockSpec((1, S, HEAD_DIM), lambda b, h, i: (b, 0, 0)),
            pl.BlockSpec((1, hg, v_rows, S), lambda b, h, i: (b, h, 0, 0)),
        ],
        out_specs=pl.BlockSpec((1, tq, hg * HEAD_DIM), lambda b, h, i: (b, i, h)),
        scratch_shapes=[pltpu.VMEM((hg, 1, tq), F32), pltpu.VMEM((hg, v_rows, tq), F32),
                        pltpu.VMEM((hg, tq, tq), F32), pltpu.VMEM((hg, tq, tq), F32)],
        compiler_params=_params(("parallel", "parallel", "arbitrary")),
        name="fox_attention",
    )(qt, k, e, vt)


def _dil_body(*refs, dils, span):
    ng = len(dils)
    q_refs, k_refs, v_refs = refs[:ng], refs[ng:2 * ng], refs[2 * ng:3 * ng]
    o_ref, o_scr, lse_scr = refs[3 * ng:]
    nk = DIL_BLOCK
    i = pl.program_id(2)
    qi = lax.broadcasted_iota(jnp.int32, (nk, 2 * nk), 0)
    ki = lax.broadcasted_iota(jnp.int32, (nk, 2 * nk), 1)
    band = (ki >= qi) & (ki <= qi + nk)
    ones = jnp.ones((2 * nk, HEAD_DIM), BF16)

    for g, d in enumerate(dils):
        rows = span // d
        nblk = rows // nk
        q_ref, k_ref, v_ref = q_refs[g], k_refs[g], v_refs[g]

        def several(it, carry, g=g, d=d, rows=rows, nblk=nblk, q_ref=q_ref, k_ref=k_ref, v_ref=v_ref):
            where, ss = [], []
            for j in range(DIL_UNROLL):
                idx = it * DIL_UNROLL + j
                r = lax.div(idx, nblk)
                n = lax.rem(idx, nblk)
                row0 = i * rows + n * nk
                prev0 = pl.multiple_of(jnp.maximum(row0 - nk, 0), nk)
                cur0 = pl.multiple_of(row0, nk)
                q = q_ref[0, r, pl.ds(pl.multiple_of(n * nk, nk), nk), :]
                kk = jnp.concatenate([k_ref[0, r, pl.ds(prev0, nk), :], k_ref[0, r, pl.ds(cur0, nk), :]],
                                     axis=0)
                s = lax.dot_general(q, kk, (((1,), (1,)), ((), ())), preferred_element_type=F32)
                ss.append(jnp.where(band & ((ki >= nk) | (row0 > 0)), s, MASK_VALUE))
                where.append((r, n, prev0, cur0))
            ps, ms = [], []
            for s in ss:
                m = jnp.max(s, axis=1, keepdims=True)
                ps.append(jnp.exp((s - m).astype(BF16)))
                ms.append(m)
            for (r, n, prev0, cur0), p, m in zip(where, ps, ms):
                vv = jnp.concatenate([v_ref[0, r, pl.ds(prev0, nk), :], v_ref[0, r, pl.ds(cur0, nk), :]],
                                     axis=0)
                ov = jnp.dot(p, jnp.concatenate([vv, ones], axis=1), preferred_element_type=F32)
                l = ov[:, HEAD_DIM:]
                o = ov[:, :HEAD_DIM] / l
                lse = m + jnp.log(l)
                tok0 = n * (nk * d) + r
                if d == 1:
                    o_scr[g, pl.ds(pl.multiple_of(tok0, nk), nk), :] = o
                    lse_scr[g, pl.ds(pl.multiple_of(tok0, nk), nk), :] = lse
                else:
                    o_scr[g, pl.ds(tok0, nk, stride=d), :] = o
                    lse_scr[g, pl.ds(tok0, nk, stride=d), :] = lse
            return carry

        assert (d * nblk) % DIL_UNROLL == 0
        lax.fori_loop(0, d * nblk // DIL_UNROLL, several, 0)

    lse = [lse_scr[g] for g in range(ng)]
    top = functools.reduce(jnp.maximum, lse)
    w = [jnp.exp(v - top) for v in lse]
    den = functools.reduce(lambda a, b: a + b, w)
    num = functools.reduce(lambda a, b: a + b, [w[g] * o_scr[g] for g in range(ng)])
    o_ref[0] = (num / den).astype(o_ref.dtype)


def _dilated(qs, ks, vs):
    B, d0, L0, W = qs[0].shape
    S = d0 * L0
    dils = tuple(t.shape[1] for t in qs)
    span = _tile(S, DIL_SPAN)
    assert all(span % (d * DIL_BLOCK) == 0 for d in dils)
    q_specs = [pl.BlockSpec((1, d, span // d, HEAD_DIM), lambda b, h, i: (b, 0, i, h)) for d in dils]
    kv_specs = [pl.BlockSpec((1, d, S // d, HEAD_DIM), lambda b, h, i: (b, 0, 0, h)) for d in dils]
    ng = len(dils)
    return pl.pallas_call(
        functools.partial(_dil_body, dils=dils, span=span),
        out_shape=jax.ShapeDtypeStruct((B, S, W), BF16),
        grid=(B, W // HEAD_DIM, S // span),
        in_specs=q_specs + kv_specs + kv_specs,
        out_specs=pl.BlockSpec((1, span, HEAD_DIM), lambda b, h, i: (b, i, h)),
        scratch_shapes=[pltpu.VMEM((ng, span, HEAD_DIM), F32), pltpu.VMEM((ng, span, HEAD_DIM), F32)],
        compiler_params=_params(("parallel", "parallel", "arbitrary")),
        name="dilated_attention",
    )(*qs, *ks, *vs)


def _merge_body(x_ref, g_ref, oa_ref, ob_ref, ga_ref, gb_ref, wa_ref, wb_ref, wo_ref, out_ref, un_ref, *,
                gi, gni):
    y_a = jnp.dot(oa_ref[0], wa_ref[...], preferred_element_type=F32)
    y_b = jnp.dot(ob_ref[0], wb_ref[...], preferred_element_type=F32)
    merged = ga_ref[0].astype(F32) * y_a + gb_ref[0].astype(F32) * y_b
    y = jnp.dot(merged.astype(BF16), wo_ref[...], preferred_element_type=F32)
    x = x_ref[0] + _rms(y, g_ref[gi:gi + 1, :])
    out_ref[0] = x
    un_ref[0] = _rms(x, g_ref[gni:gni + 1, :]).astype(BF16)


def _merge(x, g, o_a, o_b, gates, w_br_a, w_br_b, w_out, gi, gni):
    B, S, D = x.shape
    tm = _tile(S, 256)
    row = lambda w, c=0: pl.BlockSpec((1, tm, w), lambda b, i: (b, i, c))
    full = lambda a: pl.BlockSpec(a.shape, lambda b, i: (0, 0))
    return pl.pallas_call(
        functools.partial(_merge_body, gi=gi, gni=gni),
        out_shape=(jax.ShapeDtypeStruct((B, S, D), F32), jax.ShapeDtypeStruct((B, S, D), BF16)),
        grid=(B, S // tm),
        in_specs=[row(D), full(g), row(o_a.shape[-1]), row(o_b.shape[-1]), row(D, 0), row(D, 1),
                  full(w_br_a), full(w_br_b), full(w_out)],
        out_specs=(row(D), row(D)),
        compiler_params=_params(("parallel", "parallel")),
        name="mixer_merge",
    )(x, g, o_a, o_b, gates, gates, w_br_a, w_br_b, w_out)


def _ple_body(x_ref, u_ref, g_ref, gn_ref, p_ref, wp_ref, wg_ref, out_ref, un_ref, *, gi, gni):
    gate = jax.nn.sigmoid(jnp.dot(u_ref[0], wg_ref[...], preferred_element_type=F32))
    e = jnp.dot(p_ref[0, 0].astype(BF16), wp_ref[...], preferred_element_type=F32)
    x = x_ref[0] + _rms(gate * e, g_ref[gi:gi + 1, :])
    out_ref[0] = x
    un_ref[0] = _rms(x, gn_ref[gni:gni + 1, :]).astype(BF16)


def _ple(x, u, g, gn, p, layer, w_proj, w_gate, gi, gni):
    B, S, D = x.shape
    P = p.shape[-1]
    tm = _tile(S, 512)
    full = lambda a: pl.BlockSpec(a.shape, lambda b, i: (0, 0))
    row = pl.BlockSpec((1, tm, D), lambda b, i: (b, i, 0))
    return pl.pallas_call(
        functools.partial(_ple_body, gi=gi, gni=gni),
        out_shape=(jax.ShapeDtypeStruct((B, S, D), F32), jax.ShapeDtypeStruct((B, S, D), BF16)),
        grid=(B, S // tm),
        in_specs=[row, row, full(g), full(gn),
                  pl.BlockSpec((1, 1, tm, P), lambda b, i: (layer, b, i, 0)),
                  full(w_proj), full(w_gate)],
        out_specs=(row, row),
        compiler_params=_params(("parallel", "parallel")),
        name="ple_gate",
    )(x, u, g, gn, p, w_proj, w_gate)


def _rope_tables(S):
    half = HEAD_DIM // 2
    inv = ROPE_THETA ** (-jnp.arange(half, dtype=F32) * 2.0 / HEAD_DIM)
    ang = jnp.arange(S, dtype=F32)[:, None] * inv[None, :]
    cos, sin = jnp.cos(ang), jnp.sin(ang)
    return jnp.concatenate([cos, cos], axis=1), jnp.concatenate([-sin, sin], axis=1)


def _mixer(x, u, g, w_in, b_f, w_br_a, w_br_b, w_out, cos, sin):
    B, S, D = x.shape
    fox_w = w_br_a.shape[0]
    dil_out_w = w_br_b.shape[0]
    n_groups = len(DIL_PATTERNS)
    dil_w = n_groups * dil_out_w
    n_fox = b_f.shape[0]
    assert fox_w == n_fox * HEAD_DIM
    assert w_in.shape[1] == 3 * fox_w + n_fox + 3 * dil_w + 2 * D
    for window, d in DIL_PATTERNS:
        assert window // d == DIL_BLOCK

    f_lo = 3 * fox_w
    b_lo = f_lo + n_fox
    g_lo = b_lo + 3 * dil_w
    w_fox = w_in[:, :f_lo].astype(BF16)
    w_f = jnp.pad(w_in[:, f_lo:b_lo], ((0, 0), (0, HEAD_DIM - n_fox))).astype(BF16)
    w_dil = w_in[:, b_lo:g_lo].astype(BF16)
    w_gate = w_in[:, g_lo:].astype(BF16)

    qt, k, vt = _fox_proj(u, w_fox, n_fox)
    e = _forget_scan(u, w_f, b_f)
    o_a = _fox(qt, k, e, vt)

    qs, ks, vs = [], [], []
    for gidx, (_, d) in enumerate(DIL_PATTERNS):
        q, kk, v = _dil_proj(u, w_dil, cos, sin, gidx, d, n_groups, dil_out_w)
        qs.append(q), ks.append(kk), vs.append(v)
    o_b = _dilated(qs, ks, vs)

    gates = _gate_proj(u, w_gate)
    return _merge(x, g, o_a, o_b, gates, w_br_a.astype(BF16), w_br_b.astype(BF16), w_out.astype(BF16), 3, 4)


def kernel(x, p, norm_g, ffn1_w_in, ffn1_w_out, mix_w_in, fox_b_f, mix_w_br_a, mix_w_br_b, mix_w_out,
           ffn2_w_in, ffn2_w_out, ple_w_proj, ple_w_gate):
    B, S, D = x.shape
    T = B * S
    depth = norm_g.shape[0]
    cos, sin = _rope_tables(S)
    w1_in, w1_out, w2_in, w2_out = (_to_bf16(w) for w in (ffn1_w_in, ffn1_w_out, ffn2_w_in, ffn2_w_out))
    x = x.reshape(T, D)
    u = _norm(x, norm_g[0], 0)
    for i in range(depth):
        g = norm_g[i]
        gn = norm_g[(i + 1) % depth]
        x, u = _ffn(u, x, g, g, w1_in, w1_out, i, 1, 2)
        x, u = _mixer(x.reshape(B, S, D), u.reshape(B, S, D), g, mix_w_in[i], fox_b_f[i], mix_w_br_a[i],
                      mix_w_br_b[i], mix_w_out[i], cos, sin)
        x, u = _ffn(u.reshape(T, D), x.reshape(T, D), g, g, w2_in, w2_out, i, 5, 6)
        x, u = _ple(x.reshape(B, S, D), u.reshape(B, S, D), g, gn, p, i, ple_w_proj[i].astype(BF16),
                    ple_w_gate[i].astype(BF16), 7, 0)
        x, u = x.reshape(T, D), u.reshape(T, D)
    return x.reshape(B, S, D)
```

```python
import functools
import math

import jax
import jax.numpy as jnp
from jax import lax
from jax.experimental import pallas as pl
from jax.experimental.pallas import tpu as pltpu

HEAD_DIM = 128
DIL_PATTERNS = ((128, 1), (512, 4), (2048, 16))
DIL_BLOCK = 128
DIL_SPAN = 2048
DIL_UNROLL = 16
ROPE_THETA = 10000.0
RMS_EPS = 1e-6
MASK_VALUE = -1e30
LOG2E = math.log2(math.e)
N_PIECES = 3
CAST_BLOCK_ELEMS = 3 << 20
V_ONES_ROWS = 16
VMEM_LIMIT_V7X = 56 * 1024 * 1024

F32 = jnp.float32
BF16 = jnp.bfloat16


def _rms(x, g):
    return x * lax.rsqrt(jnp.mean(x * x, axis=-1, keepdims=True) + RMS_EPS) * g


def _tile(n, want):
    t = min(n, want)
    assert n % t == 0, (n, t)
    return t


def _params(sem, vmem=VMEM_LIMIT_V7X):
    return pltpu.CompilerParams(dimension_semantics=sem, vmem_limit_bytes=vmem)


def _split3(v):
    hi = v.astype(BF16)
    r = v - hi.astype(F32)
    mid = r.astype(BF16)
    lo = (r - mid.astype(F32)).astype(BF16)
    return hi, mid, lo


def _cast_body(w_ref, o_ref):
    o_ref[...] = w_ref[...].astype(BF16)


def _to_bf16(w):
    L, K, N = w.shape
    tk = min(K, max(16, 1 << int(math.log2(max(1, CAST_BLOCK_ELEMS // N)))))
    while K % tk:
        tk //= 2
    spec = pl.BlockSpec((1, tk, N), lambda l, i: (l, i, 0))
    return pl.pallas_call(
        _cast_body,
        out_shape=jax.ShapeDtypeStruct(w.shape, BF16),
        grid=(L, K // tk),
        in_specs=[spec],
        out_specs=spec,
        compiler_params=_params(("parallel", "parallel")),
        name="weight_cast",
    )(w)


def _norm_body(x_ref, g_ref, u_ref, *, gi):
    u_ref[...] = _rms(x_ref[...], g_ref[gi:gi + 1, :]).astype(BF16)


def _norm(x2, g, gi):
    T, D = x2.shape
    tm = _tile(T, 1024)
    return pl.pallas_call(
        functools.partial(_norm_body, gi=gi),
        out_shape=jax.ShapeDtypeStruct((T, D), BF16),
        grid=(T // tm,),
        in_specs=[pl.BlockSpec((tm, D), lambda i: (i, 0)), pl.BlockSpec(g.shape, lambda i: (0, 0))],
        out_specs=pl.BlockSpec((tm, D), lambda i: (i, 0)),
        compiler_params=_params(("parallel",)),
        name="first_norm",
    )(x2, g)


def _ffn_body(u_ref, x_ref, g_ref, gn_ref, wa_ref, wb_ref, wo_ref, o_ref, un_ref, acc_ref, *, gi, gni, nt):
    i = pl.program_id(0)
    j = pl.program_id(1)

    def swiglu_chunk():
        u = u_ref[...]
        a = jnp.dot(u, wa_ref[0], preferred_element_type=F32)
        b = jnp.dot(u, wb_ref[0], preferred_element_type=F32)
        h = (a * jax.nn.sigmoid(a) * b).astype(BF16)
        return jnp.dot(h, wo_ref[0], preferred_element_type=F32)

    def finish():
        x = x_ref[...] + 0.5 * _rms(acc_ref[...], g_ref[gi:gi + 1, :])
        o_ref[...] = x
        un_ref[...] = _rms(x, gn_ref[gni:gni + 1, :]).astype(BF16)

    @pl.when((j == 0) & (i == 0))
    def _():
        acc_ref[...] = swiglu_chunk()

    @pl.when((j == 0) & (i > 0) & (i < nt))
    def _():
        finish()
        acc_ref[...] = swiglu_chunk()

    @pl.when((j == 0) & (i == nt))
    def _():
        finish()

    @pl.when((j > 0) & (i < nt))
    def _():
        acc_ref[...] += swiglu_chunk()


def _ffn(u2, x2, g, gn, w_in, w_out, layer, gi, gni):
    T, D = x2.shape
    F = w_out.shape[1]
    tm = _tile(T, 512)
    tf = _tile(F, 512)
    nt = T // tm
    nj = F // tf
    cur = pl.BlockSpec((tm, D), lambda i, j: (jnp.minimum(i, nt - 1), 0))
    prev = pl.BlockSpec((tm, D), lambda i, j: (jnp.maximum(i - 1, 0), 0))
    chunk = lambda i, j: jnp.where(i < nt, j, nj - 1)
    return pl.pallas_call(
        functools.partial(_ffn_body, gi=gi, gni=gni, nt=nt),
        out_shape=(jax.ShapeDtypeStruct((T, D), F32), jax.ShapeDtypeStruct((T, D), BF16)),
        grid=(nt + 1, nj),
        in_specs=[
            cur, prev,
            pl.BlockSpec(g.shape, lambda i, j: (0, 0)),
            pl.BlockSpec(gn.shape, lambda i, j: (0, 0)),
            pl.BlockSpec((1, D, tf), lambda i, j: (layer, 0, chunk(i, j))),
            pl.BlockSpec((1, D, tf), lambda i, j: (layer, 0, chunk(i, j) + nj)),
            pl.BlockSpec((1, tf, D), lambda i, j: (layer, chunk(i, j), 0)),
        ],
        out_specs=(prev, prev),
        scratch_shapes=[pltpu.VMEM((tm, D), F32)],
        compiler_params=_params(("arbitrary", "arbitrary")),
        name="ffn",
    )(u2, x2, g, gn, w_in, w_in, w_out)


def _fox_proj_body(u_ref, wq_ref, wk_ref, wv_ref, qt_ref, k_ref, vt_ref, *, heads_per_chunk):
    u = u_ref[0]

    def transposed(t, out_ref):
        for c in range(heads_per_chunk):
            out_ref[0, c, 0:HEAD_DIM, :] = jnp.transpose(t[:, c * HEAD_DIM:(c + 1) * HEAD_DIM]).astype(BF16)

    q = jnp.dot(u, wq_ref[...], preferred_element_type=F32)
    transposed(q * (HEAD_DIM ** -0.5 * LOG2E), qt_ref)
    transposed(jnp.dot(u, wv_ref[...], preferred_element_type=F32), vt_ref)
    for c in range(heads_per_chunk):
        vt_ref[0, c, HEAD_DIM:, :] = jnp.ones((V_ONES_ROWS, u.shape[0]), BF16)
    k_ref[0] = jnp.dot(u, wk_ref[...], preferred_element_type=F32).astype(BF16)


def _fox_proj(u, w_qkv, n_heads):
    B, S, D = u.shape
    fox_w = n_heads * HEAD_DIM
    tm = _tile(S, 1024)
    tn = _tile(fox_w, 512)
    npk = fox_w // tn
    hpc = tn // HEAD_DIM
    t_spec = lambda rows: pl.BlockSpec((1, hpc, rows, tm), lambda b, i, j: (b, j, 0, i))
    t_shape = lambda rows: jax.ShapeDtypeStruct((B, n_heads, rows, S), BF16)
    v_rows = HEAD_DIM + V_ONES_ROWS
    return pl.pallas_call(
        functools.partial(_fox_proj_body, heads_per_chunk=hpc),
        out_shape=(t_shape(HEAD_DIM), jax.ShapeDtypeStruct((B, S, fox_w), BF16), t_shape(v_rows)),
        grid=(B, S // tm, npk),
        in_specs=[pl.BlockSpec((1, tm, D), lambda b, i, j: (b, i, 0)),
                  pl.BlockSpec((D, tn), lambda b, i, j: (0, j)),
                  pl.BlockSpec((D, tn), lambda b, i, j: (0, j + npk)),
                  pl.BlockSpec((D, tn), lambda b, i, j: (0, j + 2 * npk))],
        out_specs=(t_spec(HEAD_DIM), pl.BlockSpec((1, tm, tn), lambda b, i, j: (b, i, j)), t_spec(v_rows)),
        compiler_params=_params(("parallel", "parallel", "arbitrary")),
        name="fox_proj",
    )(u, w_qkv, w_qkv, w_qkv)


def _rope(t, cos, sin):
    return t * cos + pltpu.roll(t, HEAD_DIM // 2, 1) * sin


def _dil_proj_body(u_ref, wq_ref, wk_ref, wv_ref, cos_ref, sin_ref, q_ref, k_ref, v_ref, *scr, d, tm, tn):
    u = u_ref[0]
    heads = tn // HEAD_DIM
    for kind, (w_ref, out_ref) in enumerate(((wq_ref, q_ref), (wk_ref, k_ref), (wv_ref, v_ref))):
        t = jnp.dot(u, w_ref[...], preferred_element_type=F32)
        for c in range(heads):
            cols = slice(c * HEAD_DIM, (c + 1) * HEAD_DIM)
            sl = t[:, cols]
            if kind < 2:
                sl = _rope(sl, cos_ref[...], sin_ref[...])
            if kind == 0:
                sl = sl * HEAD_DIM ** -0.5
            if d == 1:
                out_ref[0, 0, :, cols] = sl.astype(BF16)
            else:
                slot = kind * heads + c
                scr[0][slot] = sl
                for r in range(d):
                    out_ref[0, r, :, cols] = scr[0][slot, pl.ds(r, tm // d, stride=d), :].astype(BF16)


def _dil_proj(u, w_qkv, cos, sin, group, d, n_groups, dil_out_w):
    B, S, D = u.shape
    tm = _tile(S, 1024)
    tn = dil_out_w
    assert tm % (d * 16) == 0
    shape = jax.ShapeDtypeStruct((B, d, S // d, tn), BF16)
    out_spec = pl.BlockSpec((1, d, tm // d, tn), lambda b, i: (b, 0, i, 0))
    w_spec = lambda kind: pl.BlockSpec((D, tn), lambda b, i: (0, kind * n_groups + group))
    return pl.pallas_call(
        functools.partial(_dil_proj_body, d=d, tm=tm, tn=tn),
        out_shape=(shape, shape, shape),
        grid=(B, S // tm),
        in_specs=[pl.BlockSpec((1, tm, D), lambda b, i: (b, i, 0)), w_spec(0), w_spec(1), w_spec(2),
                  pl.BlockSpec((tm, HEAD_DIM), lambda b, i: (i, 0)),
                  pl.BlockSpec((tm, HEAD_DIM), lambda b, i: (i, 0))],
        out_specs=(out_spec, out_spec, out_spec),
        scratch_shapes=[] if d == 1 else [pltpu.VMEM((3 * tn // HEAD_DIM, tm, HEAD_DIM), F32)],
        compiler_params=_params(("parallel", "parallel")),
        name="dil_proj",
    )(u, w_qkv, w_qkv, w_qkv, cos, sin)


def _gate_proj_body(u_ref, w_ref, o_ref):
    acc = jnp.dot(u_ref[0], w_ref[...], preferred_element_type=F32)
    o_ref[0] = jax.nn.sigmoid(acc).astype(BF16)


def _gate_proj(u, w_g):
    B, S, D = u.shape
    N = w_g.shape[1]
    tm = _tile(S, 1024)
    tn = _tile(N, 2048)
    return pl.pallas_call(
        _gate_proj_body,
        out_shape=jax.ShapeDtypeStruct((B, S, N), BF16),
        grid=(B, S // tm, N // tn),
        in_specs=[pl.BlockSpec((1, tm, D), lambda b, i, j: (b, i, 0)),
                  pl.BlockSpec((D, tn), lambda b, i, j: (0, j))],
        out_specs=pl.BlockSpec((1, tm, tn), lambda b, i, j: (b, i, j)),
        compiler_params=_params(("parallel", "parallel", "arbitrary")),
        name="gate_proj",
    )(u, w_g)


def _scan_body(u_ref, wf_ref, b_ref, e_ref, carry_ref, *, chunk, n_heads):
    tm = u_ref.shape[1]

    @pl.when(pl.program_id(1) == 0)
    def _():
        carry_ref[...] = jnp.zeros(carry_ref.shape, F32)

    row = lax.broadcasted_iota(jnp.int32, (chunk, chunk), 0)
    col = lax.broadcasted_iota(jnp.int32, (chunk, chunk), 1)
    tri = jnp.where(col <= row, 1.0, 0.0).astype(BF16)
    pr = lax.broadcasted_iota(jnp.int32, (HEAD_DIM, HEAD_DIM), 0)
    pc = lax.broadcasted_iota(jnp.int32, (HEAD_DIM, HEAD_DIM), 1)
    place = [jnp.where((pc == N_PIECES * pr + k) & (pr < n_heads), 1.0, 0.0).astype(BF16)
             for k in range(N_PIECES)]

    t = jnp.dot(u_ref[0], wf_ref[...], preferred_element_type=F32) + b_ref[...]
    ls = jnp.minimum(t, 0.0) - jnp.log1p(jnp.exp(-jnp.abs(t)))
    carry = carry_ref[...]
    for c in range(tm // chunk):
        cs = carry
        for piece in _split3(ls[c * chunk:(c + 1) * chunk]):
            cs = cs + jnp.dot(tri, piece, preferred_element_type=F32)
        e = jnp.zeros((chunk, HEAD_DIM), F32)
        for piece, pmat in zip(_split3(cs * (-LOG2E)), place):
            e = e + jnp.dot(piece, pmat, preferred_element_type=F32)
        e_ref[0, c * chunk:(c + 1) * chunk, :] = e.astype(BF16)
        carry = cs[chunk - 1:chunk, :]
    carry_ref[...] = carry


def _forget_scan(u, w_f, b_f):
    B, S, D = u.shape
    H = b_f.shape[0]
    assert N_PIECES * H <= HEAD_DIM
    tm = _tile(S, 1024)
    chunk = _tile(tm, 256)
    return pl.pallas_call(
        functools.partial(_scan_body, chunk=chunk, n_heads=H),
        out_shape=jax.ShapeDtypeStruct((B, S, HEAD_DIM), BF16),
        grid=(B, S // tm),
        in_specs=[pl.BlockSpec((1, tm, D), lambda b, i: (b, i, 0)),
                  pl.BlockSpec(w_f.shape, lambda b, i: (0, 0)),
                  pl.BlockSpec((1, HEAD_DIM), lambda b, i: (0, 0))],
        out_specs=pl.BlockSpec((1, tm, HEAD_DIM), lambda b, i: (b, i, 0)),
        scratch_shapes=[pltpu.VMEM((1, HEAD_DIM), F32)],
        compiler_params=_params(("parallel", "arbitrary")),
        name="forget_scan",
    )(u, w_f, jnp.pad(b_f, (0, HEAD_DIM - H)).reshape(1, HEAD_DIM))


def _fox_body(qt_ref, k_ref, e_ref, vt_ref, o_ref, m_ref, acc_ref, sa_ref, sb_ref, *, tq, hg):
    h0 = pl.program_id(1) * hg
    qi = pl.program_id(2)
    r = lax.broadcasted_iota(jnp.int32, (HEAD_DIM, tq), 0)
    qts = []
    for hh in range(hg):
        lo = N_PIECES * (h0 + hh)
        sel = jnp.where((r >= lo) & (r < lo + N_PIECES), 1.0, 0.0).astype(BF16)
        qts.append(jnp.concatenate([qt_ref[0, hh], sel], axis=0))

    m_ref[...] = jnp.full(m_ref.shape, MASK_VALUE, F32)
    acc_ref[...] = jnp.zeros(acc_ref.shape, F32)

    def score(j, dst, hh):
        k0 = pl.multiple_of(j * tq, tq)
        cols = slice(hh * HEAD_DIM, (hh + 1) * HEAD_DIM)
        kk = jnp.concatenate([k_ref[0, pl.ds(k0, tq), cols], e_ref[0, pl.ds(k0, tq), :]], axis=1)
        dst[hh] = jnp.dot(kk, qts[hh], preferred_element_type=F32)

    def scores(j, dst):
        for hh in range(hg):
            score(j, dst, hh)

    def softmax(src, mask):
        ps, alphas = [], []
        for hh in range(hg):
            s = src[hh] if mask is None else jnp.where(mask, src[hh], MASK_VALUE)
            m_prev = m_ref[hh]
            m_new = jnp.maximum(m_prev, jnp.max(s, axis=0, keepdims=True))
            alphas.append(jnp.exp2(m_prev - m_new))
            ps.append(jnp.exp2((s - m_new).astype(BF16)))
            m_ref[hh] = m_new
        return ps, alphas

    def value(j, ps, alphas, hh):
        k0 = pl.multiple_of(j * tq, tq)
        pv = jnp.dot(vt_ref[0, hh, :, pl.ds(k0, tq)], ps[hh], preferred_element_type=F32)
        acc_ref[hh] = alphas[hh] * acc_ref[hh] + pv

    def values(j, ps, alphas):
        for hh in range(hg):
            value(j, ps, alphas, hh)

    def step(j, src, dst):
        ps, alphas = softmax(src, None)
        score(j + 1, dst, 0)
        for hh in range(hg):
            value(j, ps, alphas, hh)
            if hh + 1 < hg:
                score(j + 1, dst, hh + 1)

    def last(src):
        key = lax.broadcasted_iota(jnp.int32, (tq, tq), 0)
        qry = lax.broadcasted_iota(jnp.int32, (tq, tq), 1)
        ps, alphas = softmax(src, key <= qry)
        values(qi, ps, alphas)

    scores(0, sa_ref)

    def two_steps(t, carry):
        step(2 * t, sa_ref, sb_ref)
        step(2 * t + 1, sb_ref, sa_ref)
        return carry

    lax.fori_loop(0, lax.div(qi, 2), two_steps, 0)

    @pl.when(lax.rem(qi, 2) == 0)
    def _():
        last(sa_ref)

    @pl.when(lax.rem(qi, 2) == 1)
    def _():
        step(qi - 1, sa_ref, sb_ref)
        last(sb_ref)

    for hh in range(hg):
        o = acc_ref[hh, 0:HEAD_DIM, :] / acc_ref[hh, HEAD_DIM:HEAD_DIM + 1, :]
        o_ref[0, :, hh * HEAD_DIM:(hh + 1) * HEAD_DIM] = jnp.transpose(o).astype(o_ref.dtype)


def _fox(qt, k, e, vt):
    B, H, _, S = qt.shape
    v_rows = vt.shape[2]
    tq = _tile(S, 512)
    hg = _tile(H, 4)
    return pl.pallas_call(
        functools.partial(_fox_body, tq=tq, hg=hg),
        out_shape=jax.ShapeDtypeStruct((B, S, H * HEAD_DIM), BF16),
        grid=(B, H // hg, S // tq),
        in_specs=[
            pl.BlockSpec((1, hg, HEAD_DIM, tq), lambda b, h, i: (b, h, 0, i)),
            pl.BlockSpec((1, S, hg * HEAD_DIM), lambda b, h, i: (b, 0, h)),
            pl.BlockSpec((1, S, HEAD_DIM), lambda b, h, i: (b, 0, 0)),
            pl.BlockSpec((1, hg, v_rows, S), lambda b, h, i: (b, h, 0, 0)),
        ],
        out_specs=pl.BlockSpec((1, tq, hg * HEAD_DIM), lambda b, h, i: (b, i, h)),
        scratch_shapes=[pltpu.VMEM((hg, 1, tq), F32), pltpu.VMEM((hg, v_rows, tq), F32),
                        pltpu.VMEM((hg, tq, tq), F32), pltpu.VMEM((hg, tq, tq), F32)],
        compiler_params=_params(("parallel", "parallel", "arbitrary")),
        name="fox_attention",
    )(qt, k, e, vt)


def _dil_body(*refs, dils, span):
    ng = len(dils)
    q_refs, k_refs, v_refs = refs[:ng], refs[ng:2 * ng], refs[2 * ng:3 * ng]
    o_ref, o_scr, lse_scr = refs[3 * ng:]
    nk = DIL_BLOCK
    i = pl.program_id(2)
    qi = lax.broadcasted_iota(jnp.int32, (nk, 2 * nk), 0)
    ki = lax.broadcasted_iota(jnp.int32, (nk, 2 * nk), 1)
    band = (ki >= qi) & (ki <= qi + nk)
    ones = jnp.ones((2 * nk, HEAD_DIM), BF16)

    for g, d in enumerate(dils):
        rows = span // d
        nblk = rows // nk
        q_ref, k_ref, v_ref = q_refs[g], k_refs[g], v_refs[g]

        def several(it, carry, g=g, d=d, rows=rows, nblk=nblk, q_ref=q_ref, k_ref=k_ref, v_ref=v_ref):
            where, ss = [], []
            for j in range(DIL_UNROLL):
                idx = it * DIL_UNROLL + j
                r = lax.div(idx, nblk)
                n = lax.rem(idx, nblk)
                row0 = i * rows + n * nk
                prev0 = pl.multiple_of(jnp.maximum(row0 - nk, 0), nk)
                cur0 = pl.multiple_of(row0, nk)
                q = q_ref[0, r, pl.ds(pl.multiple_of(n * nk, nk), nk), :]
                kk = jnp.concatenate([k_ref[0, r, pl.ds(prev0, nk), :], k_ref[0, r, pl.ds(cur0, nk), :]],
                                     axis=0)
                s = lax.dot_general(q, kk, (((1,), (1,)), ((), ())), preferred_element_type=F32)
                ss.append(jnp.where(band & ((ki >= nk) | (row0 > 0)), s, MASK_VALUE))
                where.append((r, n, prev0, cur0))
            ps, ms = [], []
            for s in ss:
                m = jnp.max(s, axis=1, keepdims=True)
                ps.append(jnp.exp((s - m).astype(BF16)))
                ms.append(m)
            for (r, n, prev0, cur0), p, m in zip(where, ps, ms):
                vv = jnp.concatenate([v_ref[0, r, pl.ds(prev0, nk), :], v_ref[0, r, pl.ds(cur0, nk), :]],
                                     axis=0)
                ov = jnp.dot(p, jnp.concatenate([vv, ones], axis=1), preferred_element_type=F32)
                l = ov[:, HEAD_DIM:]
                o = ov[:, :HEAD_DIM] / l
                lse = m + jnp.log(l)
                tok0 = n * (nk * d) + r
                if d == 1:
                    o_scr[g, pl.ds(pl.multiple_of(tok0, nk), nk), :] = o
                    lse_scr[g, pl.ds(pl.multiple_of(tok0, nk), nk), :] = lse
                else:
                    o_scr[g, pl.ds(tok0, nk, stride=d), :] = o
                    lse_scr[g, pl.ds(tok0, nk, stride=d), :] = lse
            return carry

        assert (d * nblk) % DIL_UNROLL == 0
        lax.fori_loop(0, d * nblk // DIL_UNROLL, several, 0)

    lse = [lse_scr[g] for g in range(ng)]
    top = functools.reduce(jnp.maximum, lse)
    w = [jnp.exp(v - top) for v in lse]
    den = functools.reduce(lambda a, b: a + b, w)
    num = functools.reduce(lambda a, b: a + b, [w[g] * o_scr[g] for g in range(ng)])
    o_ref[0] = (num / den).astype(o_ref.dtype)


def _dilated(qs, ks, vs):
    B, d0, L0, W = qs[0].shape
    S = d0 * L0
    dils = tuple(t.shape[1] for t in qs)
    span = _tile(S, DIL_SPAN)
    assert all(span % (d * DIL_BLOCK) == 0 for d in dils)
    q_specs = [pl.BlockSpec((1, d, span // d, HEAD_DIM), lambda b, h, i: (b, 0, i, h)) for d in dils]
    kv_specs = [pl.BlockSpec((1, d, S // d, HEAD_DIM), lambda b, h, i: (b, 0, 0, h)) for d in dils]
    ng = len(dils)
    return pl.pallas_call(
        functools.partial(_dil_body, dils=dils, span=span),
        out_shape=jax.ShapeDtypeStruct((B, S, W), BF16),
        grid=(B, W // HEAD_DIM, S // span),
        in_specs=q_specs + kv_specs + kv_specs,
        out_specs=pl.BlockSpec((1, span, HEAD_DIM), lambda b, h, i: (b, i, h)),
        scratch_shapes=[pltpu.VMEM((ng, span, HEAD_DIM), F32), pltpu.VMEM((ng, span, HEAD_DIM), F32)],
        compiler_params=_params(("parallel", "parallel", "arbitrary")),
        name="dilated_attention",
    )(*qs, *ks, *vs)


def _merge_body(x_ref, g_ref, oa_ref, ob_ref, ga_ref, gb_ref, wa_ref, wb_ref, wo_ref, out_ref, un_ref, *,
                gi, gni):
    y_a = jnp.dot(oa_ref[0], wa_ref[...], preferred_element_type=F32)
    y_b = jnp.dot(ob_ref[0], wb_ref[...], preferred_element_type=F32)
    merged = ga_ref[0].astype(F32) * y_a + gb_ref[0].astype(F32) * y_b
    y = jnp.dot(merged.astype(BF16), wo_ref[...], preferred_element_type=F32)
    x = x_ref[0] + _rms(y, g_ref[gi:gi + 1, :])
    out_ref[0] = x
    un_ref[0] = _rms(x, g_ref[gni:gni + 1, :]).astype(BF16)


def _merge(x, g, o_a, o_b, gates, w_br_a, w_br_b, w_out, gi, gni):
    B, S, D = x.shape
    tm = _tile(S, 512)
    row = lambda w, c=0: pl.BlockSpec((1, tm, w), lambda b, i: (b, i, c))
    full = lambda a: pl.BlockSpec(a.shape, lambda b, i: (0, 0), pipeline_mode=pl.Buffered(1))
    return pl.pallas_call(
        functools.partial(_merge_body, gi=gi, gni=gni),
        out_shape=(jax.ShapeDtypeStruct((B, S, D), F32), jax.ShapeDtypeStruct((B, S, D), BF16)),
        grid=(B, S // tm),
        in_specs=[row(D), full(g), row(o_a.shape[-1]), row(o_b.shape[-1]), row(D, 0), row(D, 1),
                  full(w_br_a), full(w_br_b), full(w_out)],
        out_specs=(row(D), row(D)),
        compiler_params=_params(("parallel", "parallel")),
        name="mixer_merge",
    )(x, g, o_a, o_b, gates, gates, w_br_a, w_br_b, w_out)


def _ple_body(x_ref, u_ref, g_ref, gn_ref, p_ref, wp_ref, wg_ref, out_ref, un_ref, *, gi, gni):
    gate = jax.nn.sigmoid(jnp.dot(u_ref[0], wg_ref[...], preferred_element_type=F32))
    e = jnp.dot(p_ref[0, 0].astype(BF16), wp_ref[...], preferred_element_type=F32)
    x = x_ref[0] + _rms(gate * e, g_ref[gi:gi + 1, :])
    out_ref[0] = x
    un_ref[0] = _rms(x, gn_ref[gni:gni + 1, :]).astype(BF16)


def _ple(x, u, g, gn, p, layer, w_proj, w_gate, gi, gni):
    B, S, D = x.shape
    P = p.shape[-1]
    tm = _tile(S, 512)
    full = lambda a: pl.BlockSpec(a.shape, lambda b, i: (0, 0))
    row = pl.BlockSpec((1, tm, D), lambda b, i: (b, i, 0))
    return pl.pallas_call(
        functools.partial(_ple_body, gi=gi, gni=gni),
        out_shape=(jax.ShapeDtypeStruct((B, S, D), F32), jax.ShapeDtypeStruct((B, S, D), BF16)),
        grid=(B, S // tm),
        in_specs=[row, row, full(g), full(gn),
                  pl.BlockSpec((1, 1, tm, P), lambda b, i: (layer, b, i, 0)),
                  full(w_proj), full(w_gate)],
        out_specs=(row, row),
        compiler_params=_params(("parallel", "parallel")),
        name="ple_gate",
    )(x, u, g, gn, p, w_proj, w_gate)


def _rope_tables(S):
    half = HEAD_DIM // 2
    inv = ROPE_THETA ** (-jnp.arange(half, dtype=F32) * 2.0 / HEAD_DIM)
    ang = jnp.arange(S, dtype=F32)[:, None] * inv[None, :]
    cos, sin = jnp.cos(ang), jnp.sin(ang)
    return jnp.concatenate([cos, cos], axis=1), jnp.concatenate([-sin, sin], axis=1)


def _mixer(x, u, g, w_in, b_f, w_br_a, w_br_b, w_out, cos, sin):
    B, S, D = x.shape
    fox_w = w_br_a.shape[0]
    dil_out_w = w_br_b.shape[0]
    n_groups = len(DIL_PATTERNS)
    dil_w = n_groups * dil_out_w
    n_fox = b_f.shape[0]
    assert fox_w == n_fox * HEAD_DIM
    assert w_in.shape[1] == 3 * fox_w + n_fox + 3 * dil_w + 2 * D
    for window, d in DIL_PATTERNS:
        assert window // d == DIL_BLOCK

    f_lo = 3 * fox_w
    b_lo = f_lo + n_fox
    g_lo = b_lo + 3 * dil_w
    w_fox = w_in[:, :f_lo].astype(BF16)
    w_f = jnp.pad(w_in[:, f_lo:b_lo], ((0, 0), (0, HEAD_DIM - n_fox))).astype(BF16)
    w_dil = w_in[:, b_lo:g_lo].astype(BF16)
    w_gate = w_in[:, g_lo:].astype(BF16)

    qt, k, vt = _fox_proj(u, w_fox, n_fox)
    e = _forget_scan(u, w_f, b_f)
    o_a = _fox(qt, k, e, vt)

    qs, ks, vs = [], [], []
    for gidx, (_, d) in enumerate(DIL_PATTERNS):
        q, kk, v = _dil_proj(u, w_dil, cos, sin, gidx, d, n_groups, dil_out_w)
        qs.append(q), ks.append(kk), vs.append(v)
    o_b = _dilated(qs, ks, vs)

    gates = _gate_proj(u, w_gate)
    return _merge(x, g, o_a, o_b, gates, w_br_a.astype(BF16), w_br_b.astype(BF16), w_out.astype(BF16), 3, 4)


def kernel(x, p, norm_g, ffn1_w_in, ffn1_w_out, mix_w_in, fox_b_f, mix_w_br_a, mix_w_br_b, mix_w_out,
           ffn2_w_in, ffn2_w_out, ple_w_proj, ple_w_gate):
    B, S, D = x.shape
    T = B * S
    depth = norm_g.shape[0]
    cos, sin = _rope_tables(S)
    w1_in, w1_out, w2_in, w2_out = (_to_bf16(w) for w in (ffn1_w_in, ffn1_w_out, ffn2_w_in, ffn2_w_out))
    x = x.reshape(T, D)
    u = _norm(x, norm_g[0], 0)
    for i in range(depth):
        g = norm_g[i]
        gn = norm_g[(i + 1) % depth]
        x, u = _ffn(u, x, g, g, w1_in, w1_out, i, 1, 2)
        x, u = _mixer(x.reshape(B, S, D), u.reshape(B, S, D), g, mix_w_in[i], fox_b_f[i], mix_w_br_a[i],
                      mix_w_br_b[i], mix_w_out[i], cos, sin)
        x, u = _ffn(u.reshape(T, D), x.reshape(T, D), g, g, w2_in, w2_out, i, 5, 6)
        x, u = _ple(x.reshape(B, S, D), u.reshape(B, S, D), g, gn, p, i, ple_w_proj[i].astype(BF16),
                    ple_w_gate[i].astype(BF16), 7, 0)
        x, u = x.reshape(T, D), u.reshape(T, D)
    return x.reshape(B, S, D)
```

```python
import functools
import math

import jax
import jax.numpy as jnp
from jax import lax
from jax.experimental import pallas as pl
from jax.experimental.pallas import tpu as pltpu

HEAD_DIM = 128
DIL_PATTERNS = ((128, 1), (512, 4), (2048, 16))
DIL_BLOCK = 128
DIL_SPAN = 2048
DIL_LEAD = 4
DIL_UNROLL = 16
ROPE_THETA = 10000.0
RMS_EPS = 1e-6
MASK_VALUE = -1e30
LOG2E = math.log2(math.e)
N_PIECES = 3
CAST_BLOCK_ELEMS = 3 << 20
V_ONES_ROWS = 16
VMEM_LIMIT_V7X = 56 * 1024 * 1024

F32 = jnp.float32
BF16 = jnp.bfloat16


def _rms(x, g):
    return x * lax.rsqrt(jnp.mean(x * x, axis=-1, keepdims=True) + RMS_EPS) * g


def _tile(n, want):
    t = min(n, want)
    assert n % t == 0, (n, t)
    return t


def _params(sem, vmem=VMEM_LIMIT_V7X):
    return pltpu.CompilerParams(dimension_semantics=sem, vmem_limit_bytes=vmem)


def _split3(v):
    hi = v.astype(BF16)
    r = v - hi.astype(F32)
    mid = r.astype(BF16)
    lo = (r - mid.astype(F32)).astype(BF16)
    return hi, mid, lo


def _cast_body(w_ref, o_ref):
    o_ref[...] = w_ref[...].astype(BF16)


def _to_bf16(w):
    L, K, N = w.shape
    tk = min(K, max(16, 1 << int(math.log2(max(1, CAST_BLOCK_ELEMS // N)))))
    while K % tk:
        tk //= 2
    spec = pl.BlockSpec((1, tk, N), lambda l, i: (l, i, 0))
    return pl.pallas_call(
        _cast_body,
        out_shape=jax.ShapeDtypeStruct(w.shape, BF16),
        grid=(L, K // tk),
        in_specs=[spec],
        out_specs=spec,
        compiler_params=_params(("parallel", "parallel")),
        name="weight_cast",
    )(w)


def _norm_body(x_ref, g_ref, u_ref, *, gi):
    u_ref[...] = _rms(x_ref[...], g_ref[gi:gi + 1, :]).astype(BF16)


def _norm(x2, g, gi):
    T, D = x2.shape
    tm = _tile(T, 1024)
    return pl.pallas_call(
        functools.partial(_norm_body, gi=gi),
        out_shape=jax.ShapeDtypeStruct((T, D), BF16),
        grid=(T // tm,),
        in_specs=[pl.BlockSpec((tm, D), lambda i: (i, 0)), pl.BlockSpec(g.shape, lambda i: (0, 0))],
        out_specs=pl.BlockSpec((tm, D), lambda i: (i, 0)),
        compiler_params=_params(("parallel",)),
        name="first_norm",
    )(x2, g)


def _ffn_body(u_ref, x_ref, g_ref, gn_ref, wa_ref, wb_ref, wo_ref, o_ref, un_ref, acc_ref, *, gi, gni, nt):
    i = pl.program_id(0)
    j = pl.program_id(1)

    def swiglu_chunk():
        u = u_ref[...]
        a = jnp.dot(u, wa_ref[0], preferred_element_type=F32)
        b = jnp.dot(u, wb_ref[0], preferred_element_type=F32)
        h = (a * jax.nn.sigmoid(a) * b).astype(BF16)
        return jnp.dot(h, wo_ref[0], preferred_element_type=F32)

    def finish():
        x = x_ref[...] + 0.5 * _rms(acc_ref[...], g_ref[gi:gi + 1, :])
        o_ref[...] = x
        un_ref[...] = _rms(x, gn_ref[gni:gni + 1, :]).astype(BF16)

    @pl.when((j == 0) & (i == 0))
    def _():
        acc_ref[...] = swiglu_chunk()

    @pl.when((j == 0) & (i > 0) & (i < nt))
    def _():
        finish()
        acc_ref[...] = swiglu_chunk()

    @pl.when((j == 0) & (i == nt))
    def _():
        finish()

    @pl.when((j > 0) & (i < nt))
    def _():
        acc_ref[...] += swiglu_chunk()


def _ffn(u2, x2, g, gn, w_in, w_out, layer, gi, gni):
    T, D = x2.shape
    F = w_out.shape[1]
    tm = _tile(T, 512)
    tf = _tile(F, 512)
    nt = T // tm
    nj = F // tf
    cur = pl.BlockSpec((tm, D), lambda i, j: (jnp.minimum(i, nt - 1), 0))
    prev = pl.BlockSpec((tm, D), lambda i, j: (jnp.maximum(i - 1, 0), 0))
    chunk = lambda i, j: jnp.where(i < nt, j, nj - 1)
    return pl.pallas_call(
        functools.partial(_ffn_body, gi=gi, gni=gni, nt=nt),
        out_shape=(jax.ShapeDtypeStruct((T, D), F32), jax.ShapeDtypeStruct((T, D), BF16)),
        grid=(nt + 1, nj),
        in_specs=[
            cur, prev,
            pl.BlockSpec(g.shape, lambda i, j: (0, 0)),
            pl.BlockSpec(gn.shape, lambda i, j: (0, 0)),
            pl.BlockSpec((1, D, tf), lambda i, j: (layer, 0, chunk(i, j))),
            pl.BlockSpec((1, D, tf), lambda i, j: (layer, 0, chunk(i, j) + nj)),
            pl.BlockSpec((1, tf, D), lambda i, j: (layer, chunk(i, j), 0)),
        ],
        out_specs=(prev, prev),
        scratch_shapes=[pltpu.VMEM((tm, D), F32)],
        compiler_params=_params(("arbitrary", "arbitrary")),
        name="ffn",
    )(u2, x2, g, gn, w_in, w_in, w_out)


def _fox_proj_body(u_ref, wq_ref, wk_ref, wv_ref, qt_ref, k_ref, vt_ref, *, heads_per_chunk):
    u = u_ref[0]

    def transposed(t, out_ref):
        for c in range(heads_per_chunk):
            out_ref[0, c, 0:HEAD_DIM, :] = jnp.transpose(t[:, c * HEAD_DIM:(c + 1) * HEAD_DIM]).astype(BF16)

    q = jnp.dot(u, wq_ref[...], preferred_element_type=F32)
    transposed(q * (HEAD_DIM ** -0.5 * LOG2E), qt_ref)
    transposed(jnp.dot(u, wv_ref[...], preferred_element_type=F32), vt_ref)
    for c in range(heads_per_chunk):
        vt_ref[0, c, HEAD_DIM:, :] = jnp.ones((V_ONES_ROWS, u.shape[0]), BF16)
    k_ref[0] = jnp.dot(u, wk_ref[...], preferred_element_type=F32).astype(BF16)


def _fox_proj(u, w_qkv, n_heads):
    B, S, D = u.shape
    fox_w = n_heads * HEAD_DIM
    tm = _tile(S, 1024)
    tn = _tile(fox_w, 512)
    npk = fox_w // tn
    hpc = tn // HEAD_DIM
    t_spec = lambda rows: pl.BlockSpec((1, hpc, rows, tm), lambda b, i, j: (b, j, 0, i))
    t_shape = lambda rows: jax.ShapeDtypeStruct((B, n_heads, rows, S), BF16)
    v_rows = HEAD_DIM + V_ONES_ROWS
    return pl.pallas_call(
        functools.partial(_fox_proj_body, heads_per_chunk=hpc),
        out_shape=(t_shape(HEAD_DIM), jax.ShapeDtypeStruct((B, S, fox_w), BF16), t_shape(v_rows)),
        grid=(B, S // tm, npk),
        in_specs=[pl.BlockSpec((1, tm, D), lambda b, i, j: (b, i, 0)),
                  pl.BlockSpec((D, tn), lambda b, i, j: (0, j)),
                  pl.BlockSpec((D, tn), lambda b, i, j: (0, j + npk)),
                  pl.BlockSpec((D, tn), lambda b, i, j: (0, j + 2 * npk))],
        out_specs=(t_spec(HEAD_DIM), pl.BlockSpec((1, tm, tn), lambda b, i, j: (b, i, j)), t_spec(v_rows)),
        compiler_params=_params(("parallel", "parallel", "arbitrary")),
        name="fox_proj",
    )(u, w_qkv, w_qkv, w_qkv)


def _rope(t, cos, sin):
    return t * cos + pltpu.roll(t, HEAD_DIM // 2, 1) * sin


def _dil_proj_body(u_ref, wq_ref, wk_ref, wv_ref, cos_ref, sin_ref, q_ref, k_ref, v_ref, *scr, d, tm, tn):
    u = u_ref[0]
    heads = tn // HEAD_DIM
    for kind, (w_ref, out_ref) in enumerate(((wq_ref, q_ref), (wk_ref, k_ref), (wv_ref, v_ref))):
        t = jnp.dot(u, w_ref[...], preferred_element_type=F32)
        for c in range(heads):
            cols = slice(c * HEAD_DIM, (c + 1) * HEAD_DIM)
            sl = t[:, cols]
            if kind < 2:
                sl = _rope(sl, cos_ref[...], sin_ref[...])
            if kind == 0:
                sl = sl * HEAD_DIM ** -0.5
            if d == 1:
                out_ref[0, 0, :, cols] = sl.astype(BF16)
            else:
                slot = kind * heads + c
                scr[0][slot] = sl
                for r in range(d):
                    out_ref[0, r, :, cols] = scr[0][slot, pl.ds(r, tm // d, stride=d), :].astype(BF16)


def _dil_proj(u, w_qkv, cos, sin, group, d, n_groups, dil_out_w):
    B, S, D = u.shape
    tm = _tile(S, 1024)
    tn = dil_out_w
    assert tm % (d * 16) == 0
    shape = jax.ShapeDtypeStruct((B, d, S // d, tn), BF16)
    out_spec = pl.BlockSpec((1, d, tm // d, tn), lambda b, i: (b, 0, i, 0))
    w_spec = lambda kind: pl.BlockSpec((D, tn), lambda b, i: (0, kind * n_groups + group))
    return pl.pallas_call(
        functools.partial(_dil_proj_body, d=d, tm=tm, tn=tn),
        out_shape=(shape, shape, shape),
        grid=(B, S // tm),
        in_specs=[pl.BlockSpec((1, tm, D), lambda b, i: (b, i, 0)), w_spec(0), w_spec(1), w_spec(2),
                  pl.BlockSpec((tm, HEAD_DIM), lambda b, i: (i, 0)),
                  pl.BlockSpec((tm, HEAD_DIM), lambda b, i: (i, 0))],
        out_specs=(out_spec, out_spec, out_spec),
        scratch_shapes=[] if d == 1 else [pltpu.VMEM((3 * tn // HEAD_DIM, tm, HEAD_DIM), F32)],
        compiler_params=_params(("parallel", "parallel")),
        name="dil_proj",
    )(u, w_qkv, w_qkv, w_qkv, cos, sin)


def _gate_proj_body(u_ref, w_ref, o_ref):
    acc = jnp.dot(u_ref[0], w_ref[...], preferred_element_type=F32)
    o_ref[0] = jax.nn.sigmoid(acc).astype(BF16)


def _gate_proj(u, w_g):
    B, S, D = u.shape
    N = w_g.shape[1]
    tm = _tile(S, 1024)
    tn = _tile(N, 2048)
    return pl.pallas_call(
        _gate_proj_body,
        out_shape=jax.ShapeDtypeStruct((B, S, N), BF16),
        grid=(B, S // tm, N // tn),
        in_specs=[pl.BlockSpec((1, tm, D), lambda b, i, j: (b, i, 0)),
                  pl.BlockSpec((D, tn), lambda b, i, j: (0, j))],
        out_specs=pl.BlockSpec((1, tm, tn), lambda b, i, j: (b, i, j)),
        compiler_params=_params(("parallel", "parallel", "arbitrary")),
        name="gate_proj",
    )(u, w_g)


def _scan_body(u_ref, wf_ref, b_ref, e_ref, carry_ref, *, chunk, n_heads):
    tm = u_ref.shape[1]

    @pl.when(pl.program_id(1) == 0)
    def _():
        carry_ref[...] = jnp.zeros(carry_ref.shape, F32)

    row = lax.broadcasted_iota(jnp.int32, (chunk, chunk), 0)
    col = lax.broadcasted_iota(jnp.int32, (chunk, chunk), 1)
    tri = jnp.where(col <= row, 1.0, 0.0).astype(BF16)
    pr = lax.broadcasted_iota(jnp.int32, (HEAD_DIM, HEAD_DIM), 0)
    pc = lax.broadcasted_iota(jnp.int32, (HEAD_DIM, HEAD_DIM), 1)
    place = [jnp.where((pc == N_PIECES * pr + k) & (pr < n_heads), 1.0, 0.0).astype(BF16)
             for k in range(N_PIECES)]

    t = jnp.dot(u_ref[0], wf_ref[...], preferred_element_type=F32) + b_ref[...]
    ls = jnp.minimum(t, 0.0) - jnp.log1p(jnp.exp(-jnp.abs(t)))
    carry = carry_ref[...]
    for c in range(tm // chunk):
        cs = carry
        for piece in _split3(ls[c * chunk:(c + 1) * chunk]):
            cs = cs + jnp.dot(tri, piece, preferred_element_type=F32)
        e = jnp.zeros((chunk, HEAD_DIM), F32)
        for piece, pmat in zip(_split3(cs * (-LOG2E)), place):
            e = e + jnp.dot(piece, pmat, preferred_element_type=F32)
        e_ref[0, c * chunk:(c + 1) * chunk, :] = e.astype(BF16)
        carry = cs[chunk - 1:chunk, :]
    carry_ref[...] = carry


def _forget_scan(u, w_f, b_f):
    B, S, D = u.shape
    H = b_f.shape[0]
    assert N_PIECES * H <= HEAD_DIM
    tm = _tile(S, 1024)
    chunk = _tile(tm, 256)
    return pl.pallas_call(
        functools.partial(_scan_body, chunk=chunk, n_heads=H),
        out_shape=jax.ShapeDtypeStruct((B, S, HEAD_DIM), BF16),
        grid=(B, S // tm),
        in_specs=[pl.BlockSpec((1, tm, D), lambda b, i: (b, i, 0)),
                  pl.BlockSpec(w_f.shape, lambda b, i: (0, 0)),
                  pl.BlockSpec((1, HEAD_DIM), lambda b, i: (0, 0))],
        out_specs=pl.BlockSpec((1, tm, HEAD_DIM), lambda b, i: (b, i, 0)),
        scratch_shapes=[pltpu.VMEM((1, HEAD_DIM), F32)],
        compiler_params=_params(("parallel", "arbitrary")),
        name="forget_scan",
    )(u, w_f, jnp.pad(b_f, (0, HEAD_DIM - H)).reshape(1, HEAD_DIM))


def _fox_body(qt_ref, k_ref, e_ref, vt_ref, o_ref, m_ref, acc_ref, sa_ref, sb_ref, *, tq, hg):
    h0 = pl.program_id(1) * hg
    qi = pl.program_id(2)
    r = lax.broadcasted_iota(jnp.int32, (HEAD_DIM, tq), 0)
    qts = []
    for hh in range(hg):
        lo = N_PIECES * (h0 + hh)
        sel = jnp.where((r >= lo) & (r < lo + N_PIECES), 1.0, 0.0).astype(BF16)
        qts.append(jnp.concatenate([qt_ref[0, hh], sel], axis=0))

    m_ref[...] = jnp.full(m_ref.shape, MASK_VALUE, F32)
    acc_ref[...] = jnp.zeros(acc_ref.shape, F32)

    def score(j, dst, hh):
        k0 = pl.multiple_of(j * tq, tq)
        cols = slice(hh * HEAD_DIM, (hh + 1) * HEAD_DIM)
        kk = jnp.concatenate([k_ref[0, pl.ds(k0, tq), cols], e_ref[0, pl.ds(k0, tq), :]], axis=1)
        dst[hh] = jnp.dot(kk, qts[hh], preferred_element_type=F32)

    def scores(j, dst):
        for hh in range(hg):
            score(j, dst, hh)

    def softmax(src, mask):
        ps, alphas = [], []
        for hh in range(hg):
            s = src[hh] if mask is None else jnp.where(mask, src[hh], MASK_VALUE)
            m_prev = m_ref[hh]
            m_new = jnp.maximum(m_prev, jnp.max(s, axis=0, keepdims=True))
            alphas.append(jnp.exp2(m_prev - m_new))
            ps.append(jnp.exp2((s - m_new).astype(BF16)))
            m_ref[hh] = m_new
        return ps, alphas

    def value(j, ps, alphas, hh):
        k0 = pl.multiple_of(j * tq, tq)
        pv = jnp.dot(vt_ref[0, hh, :, pl.ds(k0, tq)], ps[hh], preferred_element_type=F32)
        acc_ref[hh] = alphas[hh] * acc_ref[hh] + pv

    def values(j, ps, alphas):
        for hh in range(hg):
            value(j, ps, alphas, hh)

    def step(j, src, dst):
        ps, alphas = softmax(src, None)
        score(j + 1, dst, 0)
        for hh in range(hg):
            value(j, ps, alphas, hh)
            if hh + 1 < hg:
                score(j + 1, dst, hh + 1)

    def last(src):
        key = lax.broadcasted_iota(jnp.int32, (tq, tq), 0)
        qry = lax.broadcasted_iota(jnp.int32, (tq, tq), 1)
        ps, alphas = softmax(src, key <= qry)
        values(qi, ps, alphas)

    scores(0, sa_ref)

    def two_steps(t, carry):
        step(2 * t, sa_ref, sb_ref)
        step(2 * t + 1, sb_ref, sa_ref)
        return carry

    lax.fori_loop(0, lax.div(qi, 2), two_steps, 0)

    @pl.when(lax.rem(qi, 2) == 0)
    def _():
        last(sa_ref)

    @pl.when(lax.rem(qi, 2) == 1)
    def _():
        step(qi - 1, sa_ref, sb_ref)
        last(sb_ref)

    for hh in range(hg):
        o = acc_ref[hh, 0:HEAD_DIM, :] / acc_ref[hh, HEAD_DIM:HEAD_DIM + 1, :]
        o_ref[0, :, hh * HEAD_DIM:(hh + 1) * HEAD_DIM] = jnp.transpose(o).astype(o_ref.dtype)


def _fox(qt, k, e, vt):
    B, H, _, S = qt.shape
    v_rows = vt.shape[2]
    tq = _tile(S, 512)
    hg = _tile(H, 4)
    return pl.pallas_call(
        functools.partial(_fox_body, tq=tq, hg=hg),
        out_shape=jax.ShapeDtypeStruct((B, S, H * HEAD_DIM), BF16),
        grid=(B, H // hg, S // tq),
        in_specs=[
            pl.BlockSpec((1, hg, HEAD_DIM, tq), lambda b, h, i: (b, h, 0, i)),
            pl.BlockSpec((1, S, hg * HEAD_DIM), lambda b, h, i: (b, 0, h)),
            pl.BlockSpec((1, S, HEAD_DIM), lambda b, h, i: (b, 0, 0)),
            pl.BlockSpec((1, hg, v_rows, S), lambda b, h, i: (b, h, 0, 0)),
        ],
        out_specs=pl.BlockSpec((1, tq, hg * HEAD_DIM), lambda b, h, i: (b, i, h)),
        scratch_shapes=[pltpu.VMEM((hg, 1, tq), F32), pltpu.VMEM((hg, v_rows, tq), F32),
                        pltpu.VMEM((hg, tq, tq), F32), pltpu.VMEM((hg, tq, tq), F32)],
        compiler_params=_params(("parallel", "parallel", "arbitrary")),
        name="fox_attention",
    )(qt, k, e, vt)


def _dil_body(*refs, dils, span):
    ng = len(dils)
    q_refs, k_refs, v_refs = refs[:ng], refs[ng:2 * ng], refs[2 * ng:3 * ng]
    o_ref, o_scr, lse_scr = refs[3 * ng:]
    nk = DIL_BLOCK
    i = pl.program_id(2)
    qi = lax.broadcasted_iota(jnp.int32, (nk, 2 * nk), 0)
    ki = lax.broadcasted_iota(jnp.int32, (nk, 2 * nk), 1)
    band = (ki >= qi) & (ki <= qi + nk)
    ones = jnp.ones((2 * nk, HEAD_DIM), BF16)

    for g, d in enumerate(dils):
        rows = span // d
        nblk = rows // nk
        q_ref, k_ref, v_ref = q_refs[g], k_refs[g], v_refs[g]

        def several(it, carry, g=g, d=d, rows=rows, nblk=nblk, q_ref=q_ref, k_ref=k_ref, v_ref=v_ref):
            def score(j):
                idx = it * DIL_UNROLL + j
                r = lax.div(idx, nblk)
                n = lax.rem(idx, nblk)
                row0 = i * rows + n * nk
                prev0 = pl.multiple_of(jnp.maximum(row0 - nk, 0), nk)
                cur0 = pl.multiple_of(row0, nk)
                q = q_ref[0, r, pl.ds(pl.multiple_of(n * nk, nk), nk), :]
                kk = jnp.concatenate([k_ref[0, r, pl.ds(prev0, nk), :], k_ref[0, r, pl.ds(cur0, nk), :]],
                                     axis=0)
                s = lax.dot_general(q, kk, (((1,), (1,)), ((), ())), preferred_element_type=F32)
                return jnp.where(band & ((ki >= nk) | (row0 > 0)), s, MASK_VALUE), (r, n, prev0, cur0)

            def value(s, where):
                r, n, prev0, cur0 = where
                m = jnp.max(s, axis=1, keepdims=True)
                p = jnp.exp((s - m).astype(BF16))
                vv = jnp.concatenate([v_ref[0, r, pl.ds(prev0, nk), :], v_ref[0, r, pl.ds(cur0, nk), :]],
                                     axis=0)
                ov = jnp.dot(p, jnp.concatenate([vv, ones], axis=1), preferred_element_type=F32)
                l = ov[:, HEAD_DIM:]
                o = ov[:, :HEAD_DIM] / l
                lse = m + jnp.log(l)
                tok0 = n * (nk * d) + r
                if d == 1:
                    o_scr[g, pl.ds(pl.multiple_of(tok0, nk), nk), :] = o
                    lse_scr[g, pl.ds(pl.multiple_of(tok0, nk), nk), :] = lse
                else:
                    o_scr[g, pl.ds(tok0, nk, stride=d), :] = o
                    lse_scr[g, pl.ds(tok0, nk, stride=d), :] = lse

            pending = [score(j) for j in range(min(DIL_LEAD, DIL_UNROLL))]
            for j in range(DIL_UNROLL):
                if j + DIL_LEAD < DIL_UNROLL:
                    pending.append(score(j + DIL_LEAD))
                value(*pending.pop(0))
            return carry

        assert (d * nblk) % DIL_UNROLL == 0
        lax.fori_loop(0, d * nblk // DIL_UNROLL, several, 0)

    lse = [lse_scr[g] for g in range(ng)]
    top = functools.reduce(jnp.maximum, lse)
    w = [jnp.exp(v - top) for v in lse]
    den = functools.reduce(lambda a, b: a + b, w)
    num = functools.reduce(lambda a, b: a + b, [w[g] * o_scr[g] for g in range(ng)])
    o_ref[0] = (num / den).astype(o_ref.dtype)


def _dilated(qs, ks, vs):
    B, d0, L0, W = qs[0].shape
    S = d0 * L0
    dils = tuple(t.shape[1] for t in qs)
    span = _tile(S, DIL_SPAN)
    assert all(span % (d * DIL_BLOCK) == 0 for d in dils)
    q_specs = [pl.BlockSpec((1, d, span // d, HEAD_DIM), lambda b, h, i: (b, 0, i, h)) for d in dils]
    kv_specs = [pl.BlockSpec((1, d, S // d, HEAD_DIM), lambda b, h, i: (b, 0, 0, h)) for d in dils]
    ng = len(dils)
    return pl.pallas_call(
        functools.partial(_dil_body, dils=dils, span=span),
        out_shape=jax.ShapeDtypeStruct((B, S, W), BF16),
        grid=(B, W // HEAD_DIM, S // span),
        in_specs=q_specs + kv_specs + kv_specs,
        out_specs=pl.BlockSpec((1, span, HEAD_DIM), lambda b, h, i: (b, i, h)),
        scratch_shapes=[pltpu.VMEM((ng, span, HEAD_DIM), F32), pltpu.VMEM((ng, span, HEAD_DIM), F32)],
        compiler_params=_params(("parallel", "parallel", "arbitrary")),
        name="dilated_attention",
    )(*qs, *ks, *vs)


def _merge_body(x_ref, g_ref, oa_ref, ob_ref, ga_ref, gb_ref, wa_ref, wb_ref, wo_ref, out_ref, un_ref, *,
                gi, gni):
    y_a = jnp.dot(oa_ref[0], wa_ref[...], preferred_element_type=F32)
    y_b = jnp.dot(ob_ref[0], wb_ref[...], preferred_element_type=F32)
    merged = ga_ref[0].astype(F32) * y_a + gb_ref[0].astype(F32) * y_b
    y = jnp.dot(merged.astype(BF16), wo_ref[...], preferred_element_type=F32)
    x = x_ref[0] + _rms(y, g_ref[gi:gi + 1, :])
    out_ref[0] = x
    un_ref[0] = _rms(x, g_ref[gni:gni + 1, :]).astype(BF16)


def _merge(x, g, o_a, o_b, gates, w_br_a, w_br_b, w_out, gi, gni):
    B, S, D = x.shape
    tm = _tile(S, 512)
    row = lambda w, c=0: pl.BlockSpec((1, tm, w), lambda b, i: (b, i, c))
    full = lambda a: pl.BlockSpec(a.shape, lambda b, i: (0, 0), pipeline_mode=pl.Buffered(1))
    return pl.pallas_call(
        functools.partial(_merge_body, gi=gi, gni=gni),
        out_shape=(jax.ShapeDtypeStruct((B, S, D), F32), jax.ShapeDtypeStruct((B, S, D), BF16)),
        grid=(B, S // tm),
        in_specs=[row(D), full(g), row(o_a.shape[-1]), row(o_b.shape[-1]), row(D, 0), row(D, 1),
                  full(w_br_a), full(w_br_b), full(w_out)],
        out_specs=(row(D), row(D)),
        compiler_params=_params(("parallel", "parallel")),
        name="mixer_merge",
    )(x, g, o_a, o_b, gates, gates, w_br_a, w_br_b, w_out)


def _ple_body(x_ref, u_ref, g_ref, gn_ref, p_ref, wp_ref, wg_ref, out_ref, un_ref, *, gi, gni):
    gate = jax.nn.sigmoid(jnp.dot(u_ref[0], wg_ref[...], preferred_element_type=F32))
    e = jnp.dot(p_ref[0, 0].astype(BF16), wp_ref[...], preferred_element_type=F32)
    x = x_ref[0] + _rms(gate * e, g_ref[gi:gi + 1, :])
    out_ref[0] = x
    un_ref[0] = _rms(x, gn_ref[gni:gni + 1, :]).astype(BF16)


def _ple(x, u, g, gn, p, layer, w_proj, w_gate, gi, gni):
    B, S, D = x.shape
    P = p.shape[-1]
    tm = _tile(S, 512)
    full = lambda a: pl.BlockSpec(a.shape, lambda b, i: (0, 0))
    row = pl.BlockSpec((1, tm, D), lambda b, i: (b, i, 0))
    return pl.pallas_call(
        functools.partial(_ple_body, gi=gi, gni=gni),
        out_shape=(jax.ShapeDtypeStruct((B, S, D), F32), jax.ShapeDtypeStruct((B, S, D), BF16)),
        grid=(B, S // tm),
        in_specs=[row, row, full(g), full(gn),
                  pl.BlockSpec((1, 1, tm, P), lambda b, i: (layer, b, i, 0)),
                  full(w_proj), full(w_gate)],
        out_specs=(row, row),
        compiler_params=_params(("parallel", "parallel")),
        name="ple_gate",
    )(x, u, g, gn, p, w_proj, w_gate)


def _rope_tables(S):
    half = HEAD_DIM // 2
    inv = ROPE_THETA ** (-jnp.arange(half, dtype=F32) * 2.0 / HEAD_DIM)
    ang = jnp.arange(S, dtype=F32)[:, None] * inv[None, :]
    cos, sin = jnp.cos(ang), jnp.sin(ang)
    return jnp.concatenate([cos, cos], axis=1), jnp.concatenate([-sin, sin], axis=1)


def _mixer(x, u, g, w_in, b_f, w_br_a, w_br_b, w_out, cos, sin):
    B, S, D = x.shape
    fox_w = w_br_a.shape[0]
    dil_out_w = w_br_b.shape[0]
    n_groups = len(DIL_PATTERNS)
    dil_w = n_groups * dil_out_w
    n_fox = b_f.shape[0]
    assert fox_w == n_fox * HEAD_DIM
    assert w_in.shape[1] == 3 * fox_w + n_fox + 3 * dil_w + 2 * D
    for window, d in DIL_PATTERNS:
        assert window // d == DIL_BLOCK

    f_lo = 3 * fox_w
    b_lo = f_lo + n_fox
    g_lo = b_lo + 3 * dil_w
    w_fox = w_in[:, :f_lo].astype(BF16)
    w_f = jnp.pad(w_in[:, f_lo:b_lo], ((0, 0), (0, HEAD_DIM - n_fox))).astype(BF16)
    w_dil = w_in[:, b_lo:g_lo].astype(BF16)
    w_gate = w_in[:, g_lo:].astype(BF16)

    qt, k, vt = _fox_proj(u, w_fox, n_fox)
    e = _forget_scan(u, w_f, b_f)
    o_a = _fox(qt, k, e, vt)

    qs, ks, vs = [], [], []
    for gidx, (_, d) in enumerate(DIL_PATTERNS):
        q, kk, v = _dil_proj(u, w_dil, cos, sin, gidx, d, n_groups, dil_out_w)
        qs.append(q), ks.append(kk), vs.append(v)
    o_b = _dilated(qs, ks, vs)

    gates = _gate_proj(u, w_gate)
    return _merge(x, g, o_a, o_b, gates, w_br_a.astype(BF16), w_br_b.astype(BF16), w_out.astype(BF16), 3, 4)


def kernel(x, p, norm_g, ffn1_w_in, ffn1_w_out, mix_w_in, fox_b_f, mix_w_br_a, mix_w_br_b, mix_w_out,
           ffn2_w_in, ffn2_w_out, ple_w_proj, ple_w_gate):
    B, S, D = x.shape
    T = B * S
    depth = norm_g.shape[0]
    cos, sin = _rope_tables(S)
    w1_in, w1_out, w2_in, w2_out = (_to_bf16(w) for w in (ffn1_w_in, ffn1_w_out, ffn2_w_in, ffn2_w_out))
    x = x.reshape(T, D)
    u = _norm(x, norm_g[0], 0)
    for i in range(depth):
        g = norm_g[i]
        gn = norm_g[(i + 1) % depth]
        x, u = _ffn(u, x, g, g, w1_in, w1_out, i, 1, 2)
        x, u = _mixer(x.reshape(B, S, D), u.reshape(B, S, D), g, mix_w_in[i], fox_b_f[i], mix_w_br_a[i],
                      mix_w_br_b[i], mix_w_out[i], cos, sin)
        x, u = _ffn(u.reshape(T, D), x.reshape(T, D), g, g, w2_in, w2_out, i, 5, 6)
        x, u = _ple(x.reshape(B, S, D), u.reshape(B, S, D), g, gn, p, i, ple_w_proj[i].astype(BF16),
                    ple_w_gate[i].astype(BF16), 7, 0)
        x, u = x.reshape(T, D), u.reshape(T, D)
    return x.reshape(B, S, D)
```

```python
import functools
import math

import jax
import jax.numpy as jnp
from jax import lax
from jax.experimental import pallas as pl
from jax.experimental.pallas import tpu as pltpu

HEAD_DIM = 128
DIL_PATTERNS = ((128, 1), (512, 4), (2048, 16))
DIL_BLOCK = 128
DIL_SPAN = 2048
DIL_UNROLL = 16
ROPE_THETA = 10000.0
RMS_EPS = 1e-6
MASK_VALUE = -1e30
LOG2E = math.log2(math.e)
N_PIECES = 3
CAST_BLOCK_ELEMS = 3 << 20
V_ONES_ROWS = 16
VMEM_LIMIT_V7X = 56 * 1024 * 1024

F32 = jnp.float32
BF16 = jnp.bfloat16


def _rms(x, g):
    return x * lax.rsqrt(jnp.mean(x * x, axis=-1, keepdims=True) + RMS_EPS) * g


def _tile(n, want):
    t = min(n, want)
    assert n % t == 0, (n, t)
    return t


def _params(sem, vmem=VMEM_LIMIT_V7X):
    return pltpu.CompilerParams(dimension_semantics=sem, vmem_limit_bytes=vmem)


def _split3(v):
    hi = v.astype(BF16)
    r = v - hi.astype(F32)
    mid = r.astype(BF16)
    lo = (r - mid.astype(F32)).astype(BF16)
    return hi, mid, lo


def _cast_body(w_ref, o_ref):
    o_ref[...] = w_ref[...].astype(BF16)


def _to_bf16(w):
    L, K, N = w.shape
    tk = min(K, max(16, 1 << int(math.log2(max(1, CAST_BLOCK_ELEMS // N)))))
    while K % tk:
        tk //= 2
    spec = pl.BlockSpec((1, tk, N), lambda l, i: (l, i, 0))
    return pl.pallas_call(
        _cast_body,
        out_shape=jax.ShapeDtypeStruct(w.shape, BF16),
        grid=(L, K // tk),
        in_specs=[spec],
        out_specs=spec,
        compiler_params=_params(("parallel", "parallel")),
        name="weight_cast",
    )(w)


def _norm_body(x_ref, g_ref, u_ref, *, gi):
    u_ref[...] = _rms(x_ref[...], g_ref[gi:gi + 1, :]).astype(BF16)


def _norm(x2, g, gi):
    T, D = x2.shape
    tm = _tile(T, 1024)
    return pl.pallas_call(
        functools.partial(_norm_body, gi=gi),
        out_shape=jax.ShapeDtypeStruct((T, D), BF16),
        grid=(T // tm,),
        in_specs=[pl.BlockSpec((tm, D), lambda i: (i, 0)), pl.BlockSpec(g.shape, lambda i: (0, 0))],
        out_specs=pl.BlockSpec((tm, D), lambda i: (i, 0)),
        compiler_params=_params(("parallel",)),
        name="first_norm",
    )(x2, g)


def _ffn_body(u_ref, x_hbm, g_ref, gn_ref, wa_ref, wb_ref, wo_ref, o_hbm, un_hbm, acc_ref, x_buf, un_buf, sems,
              *, gi, gni, nt, tm, fetch_step):
    i = pl.program_id(0)
    j = pl.program_id(1)

    def rows(t):
        return pl.ds(pl.multiple_of(t * tm, tm), tm)

    fetch_x = lambda t: pltpu.make_async_copy(x_hbm.at[rows(t)], x_buf, sems.at[0])
    store_x = lambda t: pltpu.make_async_copy(x_buf, o_hbm.at[rows(t)], sems.at[1])
    store_un = lambda t: pltpu.make_async_copy(un_buf, un_hbm.at[rows(t)], sems.at[2])

    def wait_stores(t):
        store_x(t).wait()
        store_un(t).wait()

    def swiglu_chunk():
        u = u_ref[...]
        a = jnp.dot(u, wa_ref[0], preferred_element_type=F32)
        b = jnp.dot(u, wb_ref[0], preferred_element_type=F32)
        h = (a * jax.nn.sigmoid(a) * b).astype(BF16)
        return jnp.dot(h, wo_ref[0], preferred_element_type=F32)

    def finish():
        x = x_buf[...] + 0.5 * _rms(acc_ref[...], g_ref[gi:gi + 1, :])
        x_buf[...] = x
        un_buf[...] = _rms(x, gn_ref[gni:gni + 1, :]).astype(BF16)

    @pl.when((j == fetch_step) & (i > 0) & (i < nt))
    def _():
        wait_stores(i - 1)

    @pl.when((j == fetch_step) & (i < nt))
    def _():
        fetch_x(i).start()

    @pl.when((j == 0) & (i == 0))
    def _():
        acc_ref[...] = swiglu_chunk()

    @pl.when((j == 0) & (i > 0) & (i < nt))
    def _():
        fetch_x(i - 1).wait()
        finish()
        acc_ref[...] = swiglu_chunk()
        store_x(i - 1).start()
        store_un(i - 1).start()

    @pl.when((j == 0) & (i == nt))
    def _():
        fetch_x(i - 1).wait()
        finish()
        store_x(i - 1).start()
        store_un(i - 1).start()

    @pl.when((j == 1) & (i == nt))
    def _():
        wait_stores(i - 1)

    @pl.when((j > 0) & (i < nt))
    def _():
        acc_ref[...] += swiglu_chunk()


def _ffn(u2, x2, g, gn, w_in, w_out, layer, gi, gni):
    T, D = x2.shape
    F = w_out.shape[1]
    tm = _tile(T, 1024)
    tf = _tile(F, 512)
    nt = T // tm
    nj = F // tf
    assert nj >= 2
    fetch_step = min(nj // 2, nj - 1)
    assert fetch_step >= 1
    chunk = lambda i, j: jnp.where(i < nt, j, nj - 1)
    hbm = pl.BlockSpec(memory_space=pl.ANY)
    return pl.pallas_call(
        functools.partial(_ffn_body, gi=gi, gni=gni, nt=nt, tm=tm, fetch_step=fetch_step),
        out_shape=(jax.ShapeDtypeStruct((T, D), F32), jax.ShapeDtypeStruct((T, D), BF16)),
        grid=(nt + 1, nj),
        in_specs=[
            pl.BlockSpec((tm, D), lambda i, j: (jnp.minimum(i, nt - 1), 0)),
            hbm,
            pl.BlockSpec(g.shape, lambda i, j: (0, 0)),
            pl.BlockSpec(gn.shape, lambda i, j: (0, 0)),
            pl.BlockSpec((1, D, tf), lambda i, j: (layer, 0, chunk(i, j))),
            pl.BlockSpec((1, D, tf), lambda i, j: (layer, 0, chunk(i, j) + nj)),
            pl.BlockSpec((1, tf, D), lambda i, j: (layer, chunk(i, j), 0)),
        ],
        out_specs=(hbm, hbm),
        scratch_shapes=[pltpu.VMEM((tm, D), F32), pltpu.VMEM((tm, D), F32), pltpu.VMEM((tm, D), BF16),
                        pltpu.SemaphoreType.DMA((3,))],
        compiler_params=_params(("arbitrary", "arbitrary")),
        name="ffn",
    )(u2, x2, g, gn, w_in, w_in, w_out)


def _fox_proj_body(u_ref, wq_ref, wk_ref, wv_ref, qt_ref, k_ref, vt_ref, *, heads_per_chunk):
    u = u_ref[0]

    def transposed(t, out_ref):
        for c in range(heads_per_chunk):
            out_ref[0, c, 0:HEAD_DIM, :] = jnp.transpose(t[:, c * HEAD_DIM:(c + 1) * HEAD_DIM]).astype(BF16)

    q = jnp.dot(u, wq_ref[...], preferred_element_type=F32)
    transposed(q * (HEAD_DIM ** -0.5 * LOG2E), qt_ref)
    transposed(jnp.dot(u, wv_ref[...], preferred_element_type=F32), vt_ref)
    for c in range(heads_per_chunk):
        vt_ref[0, c, HEAD_DIM:, :] = jnp.ones((V_ONES_ROWS, u.shape[0]), BF16)
    k_ref[0] = jnp.dot(u, wk_ref[...], preferred_element_type=F32).astype(BF16)


def _fox_proj(u, w_qkv, n_heads):
    B, S, D = u.shape
    fox_w = n_heads * HEAD_DIM
    tm = _tile(S, 1024)
    tn = _tile(fox_w, 512)
    npk = fox_w // tn
    hpc = tn // HEAD_DIM
    t_spec = lambda rows: pl.BlockSpec((1, hpc, rows, tm), lambda b, i, j: (b, j, 0, i))
    t_shape = lambda rows: jax.ShapeDtypeStruct((B, n_heads, rows, S), BF16)
    v_rows = HEAD_DIM + V_ONES_ROWS
    return pl.pallas_call(
        functools.partial(_fox_proj_body, heads_per_chunk=hpc),
        out_shape=(t_shape(HEAD_DIM), jax.ShapeDtypeStruct((B, S, fox_w), BF16), t_shape(v_rows)),
        grid=(B, S // tm, npk),
        in_specs=[pl.BlockSpec((1, tm, D), lambda b, i, j: (b, i, 0)),
                  pl.BlockSpec((D, tn), lambda b, i, j: (0, j)),
                  pl.BlockSpec((D, tn), lambda b, i, j: (0, j + npk)),
                  pl.BlockSpec((D, tn), lambda b, i, j: (0, j + 2 * npk))],
        out_specs=(t_spec(HEAD_DIM), pl.BlockSpec((1, tm, tn), lambda b, i, j: (b, i, j)), t_spec(v_rows)),
        compiler_params=_params(("parallel", "parallel", "arbitrary")),
        name="fox_proj",
    )(u, w_qkv, w_qkv, w_qkv)


def _rope(t, cos, sin):
    return t * cos + pltpu.roll(t, HEAD_DIM // 2, 1) * sin


def _dil_proj_body(u_ref, wq_ref, wk_ref, wv_ref, cos_ref, sin_ref, q_ref, k_ref, v_ref, *scr, d, tm, tn):
    u = u_ref[0]
    heads = tn // HEAD_DIM
    for kind, (w_ref, out_ref) in enumerate(((wq_ref, q_ref), (wk_ref, k_ref), (wv_ref, v_ref))):
        t = jnp.dot(u, w_ref[...], preferred_element_type=F32)
        for c in range(heads):
            cols = slice(c * HEAD_DIM, (c + 1) * HEAD_DIM)
            sl = t[:, cols]
            if kind < 2:
                sl = _rope(sl, cos_ref[...], sin_ref[...])
            if kind == 0:
                sl = sl * HEAD_DIM ** -0.5
            if d == 1:
                out_ref[0, 0, :, cols] = sl.astype(BF16)
            else:
                slot = kind * heads + c
                scr[0][slot] = sl
                for r in range(d):
                    out_ref[0, r, :, cols] = scr[0][slot, pl.ds(r, tm // d, stride=d), :].astype(BF16)


def _dil_proj(u, w_qkv, cos, sin, group, d, n_groups, dil_out_w):
    B, S, D = u.shape
    tm = _tile(S, 1024)
    tn = dil_out_w
    assert tm % (d * 16) == 0
    shape = jax.ShapeDtypeStruct((B, d, S // d, tn), BF16)
    out_spec = pl.BlockSpec((1, d, tm // d, tn), lambda b, i: (b, 0, i, 0))
    w_spec = lambda kind: pl.BlockSpec((D, tn), lambda b, i: (0, kind * n_groups + group))
    return pl.pallas_call(
        functools.partial(_dil_proj_body, d=d, tm=tm, tn=tn),
        out_shape=(shape, shape, shape),
        grid=(B, S // tm),
        in_specs=[pl.BlockSpec((1, tm, D), lambda b, i: (b, i, 0)), w_spec(0), w_spec(1), w_spec(2),
                  pl.BlockSpec((tm, HEAD_DIM), lambda b, i: (i, 0)),
                  pl.BlockSpec((tm, HEAD_DIM), lambda b, i: (i, 0))],
        out_specs=(out_spec, out_spec, out_spec),
        scratch_shapes=[] if d == 1 else [pltpu.VMEM((3 * tn // HEAD_DIM, tm, HEAD_DIM), F32)],
        compiler_params=_params(("parallel", "parallel")),
        name="dil_proj",
    )(u, w_qkv, w_qkv, w_qkv, cos, sin)


def _gate_proj_body(u_ref, w_ref, o_ref):
    acc = jnp.dot(u_ref[0], w_ref[...], preferred_element_type=F32)
    o_ref[0] = jax.nn.sigmoid(acc).astype(BF16)


def _gate_proj(u, w_g):
    B, S, D = u.shape
    N = w_g.shape[1]
    tm = _tile(S, 1024)
    tn = _tile(N, 2048)
    return pl.pallas_call(
        _gate_proj_body,
        out_shape=jax.ShapeDtypeStruct((B, S, N), BF16),
        grid=(B, S // tm, N // tn),
        in_specs=[pl.BlockSpec((1, tm, D), lambda b, i, j: (b, i, 0)),
                  pl.BlockSpec((D, tn), lambda b, i, j: (0, j))],
        out_specs=pl.BlockSpec((1, tm, tn), lambda b, i, j: (b, i, j)),
        compiler_params=_params(("parallel", "parallel", "arbitrary")),
        name="gate_proj",
    )(u, w_g)


def _scan_body(u_ref, wf_ref, b_ref, e_ref, carry_ref, *, chunk, n_heads):
    tm = u_ref.shape[1]

    @pl.when(pl.program_id(1) == 0)
    def _():
        carry_ref[...] = jnp.zeros(carry_ref.shape, F32)

    row = lax.broadcasted_iota(jnp.int32, (chunk, chunk), 0)
    col = lax.broadcasted_iota(jnp.int32, (chunk, chunk), 1)
    tri = jnp.where(col <= row, 1.0, 0.0).astype(BF16)
    pr = lax.broadcasted_iota(jnp.int32, (HEAD_DIM, HEAD_DIM), 0)
    pc = lax.broadcasted_iota(jnp.int32, (HEAD_DIM, HEAD_DIM), 1)
    place = [jnp.where((pc == N_PIECES * pr + k) & (pr < n_heads), 1.0, 0.0).astype(BF16)
             for k in range(N_PIECES)]

    t = jnp.dot(u_ref[0], wf_ref[...], preferred_element_type=F32) + b_ref[...]
    ls = jnp.minimum(t, 0.0) - jnp.log1p(jnp.exp(-jnp.abs(t)))
    carry = carry_ref[...]
    for c in range(tm // chunk):
        cs = carry
        for piece in _split3(ls[c * chunk:(c + 1) * chunk]):
            cs = cs + jnp.dot(tri, piece, preferred_element_type=F32)
        e = jnp.zeros((chunk, HEAD_DIM), F32)
        for piece, pmat in zip(_split3(cs * (-LOG2E)), place):
            e = e + jnp.dot(piece, pmat, preferred_element_type=F32)
        e_ref[0, c * chunk:(c + 1) * chunk, :] = e.astype(BF16)
        carry = cs[chunk - 1:chunk, :]
    carry_ref[...] = carry


def _forget_scan(u, w_f, b_f):
    B, S, D = u.shape
    H = b_f.shape[0]
    assert N_PIECES * H <= HEAD_DIM
    tm = _tile(S, 1024)
    chunk = _tile(tm, 256)
    return pl.pallas_call(
        functools.partial(_scan_body, chunk=chunk, n_heads=H),
        out_shape=jax.ShapeDtypeStruct((B, S, HEAD_DIM), BF16),
        grid=(B, S // tm),
        in_specs=[pl.BlockSpec((1, tm, D), lambda b, i: (b, i, 0)),
                  pl.BlockSpec(w_f.shape, lambda b, i: (0, 0)),
                  pl.BlockSpec((1, HEAD_DIM), lambda b, i: (0, 0))],
        out_specs=pl.BlockSpec((1, tm, HEAD_DIM), lambda b, i: (b, i, 0)),
        scratch_shapes=[pltpu.VMEM((1, HEAD_DIM), F32)],
        compiler_params=_params(("parallel", "arbitrary")),
        name="forget_scan",
    )(u, w_f, jnp.pad(b_f, (0, HEAD_DIM - H)).reshape(1, HEAD_DIM))


def _fox_body(qt_ref, k_ref, e_ref, vt_ref, o_ref, m_ref, acc_ref, sa_ref, sb_ref, *, tq, hg):
    h0 = pl.program_id(1) * hg
    qi = pl.program_id(2)
    r = lax.broadcasted_iota(jnp.int32, (HEAD_DIM, tq), 0)
    qts = []
    for hh in range(hg):
        lo = N_PIECES * (h0 + hh)
        sel = jnp.where((r >= lo) & (r < lo + N_PIECES), 1.0, 0.0).astype(BF16)
        qts.append(jnp.concatenate([qt_ref[0, hh], sel], axis=0))

    m_ref[...] = jnp.full(m_ref.shape, MASK_VALUE, F32)
    acc_ref[...] = jnp.zeros(acc_ref.shape, F32)

    def score(j, dst, hh):
        k0 = pl.multiple_of(j * tq, tq)
        cols = slice(hh * HEAD_DIM, (hh + 1) * HEAD_DIM)
        kk = jnp.concatenate([k_ref[0, pl.ds(k0, tq), cols], e_ref[0, pl.ds(k0, tq), :]], axis=1)
        dst[hh] = jnp.dot(kk, qts[hh], preferred_element_type=F32)

    def scores(j, dst):
        for hh in range(hg):
            score(j, dst, hh)

    def softmax(src, mask):
        ps, alphas = [], []
        for hh in range(hg):
            s = src[hh] if mask is None else jnp.where(mask, src[hh], MASK_VALUE)
            m_prev = m_ref[hh]
            m_new = jnp.maximum(m_prev, jnp.max(s, axis=0, keepdims=True))
            alphas.append(jnp.exp2(m_prev - m_new))
            ps.append(jnp.exp2((s - m_new).astype(BF16)))
            m_ref[hh] = m_new
        return ps, alphas

    def value(j, ps, alphas, hh):
        k0 = pl.multiple_of(j * tq, tq)
        pv = jnp.dot(vt_ref[0, hh, :, pl.ds(k0, tq)], ps[hh], preferred_element_type=F32)
        acc_ref[hh] = alphas[hh] * acc_ref[hh] + pv

    def values(j, ps, alphas):
        for hh in range(hg):
            value(j, ps, alphas, hh)

    def step(j, src, dst):
        ps, alphas = softmax(src, None)
        score(j + 1, dst, 0)
        for hh in range(hg):
            value(j, ps, alphas, hh)
            if hh + 1 < hg:
                score(j + 1, dst, hh + 1)

    def last(src):
        key = lax.broadcasted_iota(jnp.int32, (tq, tq), 0)
        qry = lax.broadcasted_iota(jnp.int32, (tq, tq), 1)
        ps, alphas = softmax(src, key <= qry)
        values(qi, ps, alphas)

    scores(0, sa_ref)

    def two_steps(t, carry):
        step(2 * t, sa_ref, sb_ref)
        step(2 * t + 1, sb_ref, sa_ref)
        return carry

    lax.fori_loop(0, lax.div(qi, 2), two_steps, 0)

    @pl.when(lax.rem(qi, 2) == 0)
    def _():
        last(sa_ref)

    @pl.when(lax.rem(qi, 2) == 1)
    def _():
        step(qi - 1, sa_ref, sb_ref)
        last(sb_ref)

    for hh in range(hg):
        o = acc_ref[hh, 0:HEAD_DIM, :] / acc_ref[hh, HEAD_DIM:HEAD_DIM + 1, :]
        o_ref[0, :, hh * HEAD_DIM:(hh + 1) * HEAD_DIM] = jnp.transpose(o).astype(o_ref.dtype)


def _fox(qt, k, e, vt):
    B, H, _, S = qt.shape
    v_rows = vt.shape[2]
    tq = _tile(S, 512)
    hg = _tile(H, 4)
    return pl.pallas_call(
        functools.partial(_fox_body, tq=tq, hg=hg),
        out_shape=jax.ShapeDtypeStruct((B, S, H * HEAD_DIM), BF16),
        grid=(B, H // hg, S // tq),
        in_specs=[
            pl.BlockSpec((1, hg, HEAD_DIM, tq), lambda b, h, i: (b, h, 0, i)),
            pl.BlockSpec((1, S, hg * HEAD_DIM), lambda b, h, i: (b, 0, h)),
            pl.BlockSpec((1, S, HEAD_DIM), lambda b, h, i: (b, 0, 0)),
            pl.BlockSpec((1, hg, v_rows, S), lambda b, h, i: (b, h, 0, 0)),
        ],
        out_specs=pl.BlockSpec((1, tq, hg * HEAD_DIM), lambda b, h, i: (b, i, h)),
        scratch_shapes=[pltpu.VMEM((hg, 1, tq), F32), pltpu.VMEM((hg, v_rows, tq), F32),
                        pltpu.VMEM((hg, tq, tq), F32), pltpu.VMEM((hg, tq, tq), F32)],
        compiler_params=_params(("parallel", "parallel", "arbitrary")),
        name="fox_attention",
    )(qt, k, e, vt)


def _dil_body(*refs, dils, span):
    ng = len(dils)
    q_refs, k_refs, v_refs = refs[:ng], refs[ng:2 * ng], refs[2 * ng:3 * ng]
    o_ref, o_scr, lse_scr = refs[3 * ng:]
    nk = DIL_BLOCK
    i = pl.program_id(2)
    qi = lax.broadcasted_iota(jnp.int32, (nk, 2 * nk), 0)
    ki = lax.broadcasted_iota(jnp.int32, (nk, 2 * nk), 1)
    band = (ki >= qi) & (ki <= qi + nk)
    ones = jnp.ones((2 * nk, HEAD_DIM), BF16)

    for g, d in enumerate(dils):
        rows = span // d
        nblk = rows // nk
        q_ref, k_ref, v_ref = q_refs[g], k_refs[g], v_refs[g]

        def several(it, carry, g=g, d=d, rows=rows, nblk=nblk, q_ref=q_ref, k_ref=k_ref, v_ref=v_ref):
            where, ss = [], []
            for j in range(DIL_UNROLL):
                idx = it * DIL_UNROLL + j
                r = lax.div(idx, nblk)
                n = lax.rem(idx, nblk)
                row0 = i * rows + n * nk
                prev0 = pl.multiple_of(jnp.maximum(row0 - nk, 0), nk)
                cur0 = pl.multiple_of(row0, nk)
                q = q_ref[0, r, pl.ds(pl.multiple_of(n * nk, nk), nk), :]
                kk = jnp.concatenate([k_ref[0, r, pl.ds(prev0, nk), :], k_ref[0, r, pl.ds(cur0, nk), :]],
                                     axis=0)
                s = lax.dot_general(q, kk, (((1,), (1,)), ((), ())), preferred_element_type=F32)
                ss.append(jnp.where(band & ((ki >= nk) | (row0 > 0)), s, MASK_VALUE))
                where.append((r, n, prev0, cur0))
            ps, ms = [], []
            for s in ss:
                m = jnp.max(s, axis=1, keepdims=True)
                ps.append(jnp.exp((s - m).astype(BF16)))
                ms.append(m)
            for (r, n, prev0, cur0), p, m in zip(where, ps, ms):
                vv = jnp.concatenate([v_ref[0, r, pl.ds(prev0, nk), :], v_ref[0, r, pl.ds(cur0, nk), :]],
                                     axis=0)
                ov = jnp.dot(p, jnp.concatenate([vv, ones], axis=1), preferred_element_type=F32)
                l = ov[:, HEAD_DIM:]
                o = ov[:, :HEAD_DIM] / l
                lse = m + jnp.log(l)
                tok0 = n * (nk * d) + r
                if d == 1:
                    o_scr[g, pl.ds(pl.multiple_of(tok0, nk), nk), :] = o
                    lse_scr[g, pl.ds(pl.multiple_of(tok0, nk), nk), :] = lse
                else:
                    o_scr[g, pl.ds(tok0, nk, stride=d), :] = o
                    lse_scr[g, pl.ds(tok0, nk, stride=d), :] = lse
            return carry

        assert (d * nblk) % DIL_UNROLL == 0
        lax.fori_loop(0, d * nblk // DIL_UNROLL, several, 0)

    lse = [lse_scr[g] for g in range(ng)]
    top = functools.reduce(jnp.maximum, lse)
    w = [jnp.exp(v - top) for v in lse]
    den = functools.reduce(lambda a, b: a + b, w)
    num = functools.reduce(lambda a, b: a + b, [w[g] * o_scr[g] for g in range(ng)])
    o_ref[0] = (num / den).astype(o_ref.dtype)


def _dilated(qs, ks, vs):
    B, d0, L0, W = qs[0].shape
    S = d0 * L0
    dils = tuple(t.shape[1] for t in qs)
    span = _tile(S, DIL_SPAN)
    assert all(span % (d * DIL_BLOCK) == 0 for d in dils)
    q_specs = [pl.BlockSpec((1, d, span // d, HEAD_DIM), lambda b, h, i: (b, 0, i, h)) for d in dils]
    kv_specs = [pl.BlockSpec((1, d, S // d, HEAD_DIM), lambda b, h, i: (b, 0, 0, h)) for d in dils]
    ng = len(dils)
    return pl.pallas_call(
        functools.partial(_dil_body, dils=dils, span=span),
        out_shape=jax.ShapeDtypeStruct((B, S, W), BF16),
        grid=(B, W // HEAD_DIM, S // span),
        in_specs=q_specs + kv_specs + kv_specs,
        out_specs=pl.BlockSpec((1, span, HEAD_DIM), lambda b, h, i: (b, i, h)),
        scratch_shapes=[pltpu.VMEM((ng, span, HEAD_DIM), F32), pltpu.VMEM((ng, span, HEAD_DIM), F32)],
        compiler_params=_params(("parallel", "parallel", "arbitrary")),
        name="dilated_attention",
    )(*qs, *ks, *vs)


def _merge_body(x_ref, g_ref, oa_ref, ob_ref, ga_ref, gb_ref, wa_ref, wb_ref, wo_ref, out_ref, un_ref, *,
                gi, gni):
    y_a = jnp.dot(oa_ref[0], wa_ref[...], preferred_element_type=F32)
    y_b = jnp.dot(ob_ref[0], wb_ref[...], preferred_element_type=F32)
    merged = ga_ref[0].astype(F32) * y_a + gb_ref[0].astype(F32) * y_b
    y = jnp.dot(merged.astype(BF16), wo_ref[...], preferred_element_type=F32)
    x = x_ref[0] + _rms(y, g_ref[gi:gi + 1, :])
    out_ref[0] = x
    un_ref[0] = _rms(x, g_ref[gni:gni + 1, :]).astype(BF16)


def _merge(x, g, o_a, o_b, gates, w_br_a, w_br_b, w_out, gi, gni):
    B, S, D = x.shape
    tm = _tile(S, 512)
    row = lambda w, c=0: pl.BlockSpec((1, tm, w), lambda b, i: (b, i, c))
    full = lambda a: pl.BlockSpec(a.shape, lambda b, i: (0, 0), pipeline_mode=pl.Buffered(1))
    return pl.pallas_call(
        functools.partial(_merge_body, gi=gi, gni=gni),
        out_shape=(jax.ShapeDtypeStruct((B, S, D), F32), jax.ShapeDtypeStruct((B, S, D), BF16)),
        grid=(B, S // tm),
        in_specs=[row(D), full(g), row(o_a.shape[-1]), row(o_b.shape[-1]), row(D, 0), row(D, 1),
                  full(w_br_a), full(w_br_b), full(w_out)],
        out_specs=(row(D), row(D)),
        compiler_params=_params(("parallel", "parallel")),
        name="mixer_merge",
    )(x, g, o_a, o_b, gates, gates, w_br_a, w_br_b, w_out)


def _ple_body(x_ref, u_ref, g_ref, gn_ref, p_ref, wp_ref, wg_ref, out_ref, un_ref, *, gi, gni):
    gate = jax.nn.sigmoid(jnp.dot(u_ref[0], wg_ref[...], preferred_element_type=F32))
    e = jnp.dot(p_ref[0, 0].astype(BF16), wp_ref[...], preferred_element_type=F32)
    x = x_ref[0] + _rms(gate * e, g_ref[gi:gi + 1, :])
    out_ref[0] = x
    un_ref[0] = _rms(x, gn_ref[gni:gni + 1, :]).astype(BF16)


def _ple(x, u, g, gn, p, layer, w_proj, w_gate, gi, gni):
    B, S, D = x.shape
    P = p.shape[-1]
    tm = _tile(S, 512)
    full = lambda a: pl.BlockSpec(a.shape, lambda b, i: (0, 0))
    row = pl.BlockSpec((1, tm, D), lambda b, i: (b, i, 0))
    return pl.pallas_call(
        functools.partial(_ple_body, gi=gi, gni=gni),
        out_shape=(jax.ShapeDtypeStruct((B, S, D), F32), jax.ShapeDtypeStruct((B, S, D), BF16)),
        grid=(B, S // tm),
        in_specs=[row, row, full(g), full(gn),
                  pl.BlockSpec((1, 1, tm, P), lambda b, i: (layer, b, i, 0)),
                  full(w_proj), full(w_gate)],
        out_specs=(row, row),
        compiler_params=_params(("parallel", "parallel")),
        name="ple_gate",
    )(x, u, g, gn, p, w_proj, w_gate)


def _rope_tables(S):
    half = HEAD_DIM // 2
    inv = ROPE_THETA ** (-jnp.arange(half, dtype=F32) * 2.0 / HEAD_DIM)
    ang = jnp.arange(S, dtype=F32)[:, None] * inv[None, :]
    cos, sin = jnp.cos(ang), jnp.sin(ang)
    return jnp.concatenate([cos, cos], axis=1), jnp.concatenate([-sin, sin], axis=1)


def _mixer(x, u, g, w_in, b_f, w_br_a, w_br_b, w_out, cos, sin):
    B, S, D = x.shape
    fox_w = w_br_a.shape[0]
    dil_out_w = w_br_b.shape[0]
    n_groups = len(DIL_PATTERNS)
    dil_w = n_groups * dil_out_w
    n_fox = b_f.shape[0]
    assert fox_w == n_fox * HEAD_DIM
    assert w_in.shape[1] == 3 * fox_w + n_fox + 3 * dil_w + 2 * D
    for window, d in DIL_PATTERNS:
        assert window // d == DIL_BLOCK

    f_lo = 3 * fox_w
    b_lo = f_lo + n_fox
    g_lo = b_lo + 3 * dil_w
    w_fox = w_in[:, :f_lo].astype(BF16)
    w_f = jnp.pad(w_in[:, f_lo:b_lo], ((0, 0), (0, HEAD_DIM - n_fox))).astype(BF16)
    w_dil = w_in[:, b_lo:g_lo].astype(BF16)
    w_gate = w_in[:, g_lo:].astype(BF16)

    qt, k, vt = _fox_proj(u, w_fox, n_fox)
    e = _forget_scan(u, w_f, b_f)
    o_a = _fox(qt, k, e, vt)

    qs, ks, vs = [], [], []
    for gidx, (_, d) in enumerate(DIL_PATTERNS):
        q, kk, v = _dil_proj(u, w_dil, cos, sin, gidx, d, n_groups, dil_out_w)
        qs.append(q), ks.append(kk), vs.append(v)
    o_b = _dilated(qs, ks, vs)

    gates = _gate_proj(u, w_gate)
    return _merge(x, g, o_a, o_b, gates, w_br_a.astype(BF16), w_br_b.astype(BF16), w_out.astype(BF16), 3, 4)


def kernel(x, p, norm_g, ffn1_w_in, ffn1_w_out, mix_w_in, fox_b_f, mix_w_br_a, mix_w_br_b, mix_w_out,
           ffn2_w_in, ffn2_w_out, ple_w_proj, ple_w_gate):
    B, S, D = x.shape
    T = B * S
    depth = norm_g.shape[0]
    cos, sin = _rope_tables(S)
    w1_in, w1_out, w2_in, w2_out = (_to_bf16(w) for w in (ffn1_w_in, ffn1_w_out, ffn2_w_in, ffn2_w_out))
    x = x.reshape(T, D)
    u = _norm(x, norm_g[0], 0)
    for i in range(depth):
        g = norm_g[i]
        gn = norm_g[(i + 1) % depth]
        x, u = _ffn(u, x, g, g, w1_in, w1_out, i, 1, 2)
        x, u = _mixer(x.reshape(B, S, D), u.reshape(B, S, D), g, mix_w_in[i], fox_b_f[i], mix_w_br_a[i],
                      mix_w_br_b[i], mix_w_out[i], cos, sin)
        x, u = _ffn(u.reshape(T, D), x.reshape(T, D), g, g, w2_in, w2_out, i, 5, 6)
        x, u = _ple(x.reshape(B, S, D), u.reshape(B, S, D), g, gn, p, i, ple_w_proj[i].astype(BF16),
                    ple_w_gate[i].astype(BF16), 7, 0)
        x, u = x.reshape(T, D), u.reshape(T, D)
    return x.reshape(B, S, D)
```

```python
import functools
import math

import jax
import jax.numpy as jnp
from jax import lax
from jax.experimental import pallas as pl
from jax.experimental.pallas import tpu as pltpu

HEAD_DIM = 128
DIL_PATTERNS = ((128, 1), (512, 4), (2048, 16))
DIL_BLOCK = 128
DIL_SPAN = 2048
DIL_PAD_ROWS = 8
DIL_UNROLL = 16
ROPE_THETA = 10000.0
RMS_EPS = 1e-6
MASK_VALUE = -1e30
LOG2E = math.log2(math.e)
N_PIECES = 3
CAST_BLOCK_ELEMS = 3 << 20
V_ONES_ROWS = 16
VMEM_LIMIT_V7X = 56 * 1024 * 1024

F32 = jnp.float32
BF16 = jnp.bfloat16


def _rms(x, g):
    return x * lax.rsqrt(jnp.mean(x * x, axis=-1, keepdims=True) + RMS_EPS) * g


def _tile(n, want):
    t = min(n, want)
    assert n % t == 0, (n, t)
    return t


def _params(sem, vmem=VMEM_LIMIT_V7X):
    return pltpu.CompilerParams(dimension_semantics=sem, vmem_limit_bytes=vmem)


def _split3(v):
    hi = v.astype(BF16)
    r = v - hi.astype(F32)
    mid = r.astype(BF16)
    lo = (r - mid.astype(F32)).astype(BF16)
    return hi, mid, lo


def _cast_body(w_ref, o_ref):
    o_ref[...] = w_ref[...].astype(BF16)


def _to_bf16(w):
    L, K, N = w.shape
    tk = min(K, max(16, 1 << int(math.log2(max(1, CAST_BLOCK_ELEMS // N)))))
    while K % tk:
        tk //= 2
    spec = pl.BlockSpec((1, tk, N), lambda l, i: (l, i, 0))
    return pl.pallas_call(
        _cast_body,
        out_shape=jax.ShapeDtypeStruct(w.shape, BF16),
        grid=(L, K // tk),
        in_specs=[spec],
        out_specs=spec,
        compiler_params=_params(("parallel", "parallel")),
        name="weight_cast",
    )(w)


def _norm_body(x_ref, g_ref, u_ref, *, gi):
    u_ref[...] = _rms(x_ref[...], g_ref[gi:gi + 1, :]).astype(BF16)


def _norm(x2, g, gi):
    T, D = x2.shape
    tm = _tile(T, 1024)
    return pl.pallas_call(
        functools.partial(_norm_body, gi=gi),
        out_shape=jax.ShapeDtypeStruct((T, D), BF16),
        grid=(T // tm,),
        in_specs=[pl.BlockSpec((tm, D), lambda i: (i, 0)), pl.BlockSpec(g.shape, lambda i: (0, 0))],
        out_specs=pl.BlockSpec((tm, D), lambda i: (i, 0)),
        compiler_params=_params(("parallel",)),
        name="first_norm",
    )(x2, g)


def _ffn_body(u_ref, x_hbm, g_ref, gn_ref, wa_ref, wb_ref, wo_ref, o_hbm, un_hbm, acc_ref, x_buf, un_buf, sems,
              *, gi, gni, nt, tm, fetch_step):
    i = pl.program_id(0)
    j = pl.program_id(1)

    def rows(t):
        return pl.ds(pl.multiple_of(t * tm, tm), tm)

    fetch_x = lambda t: pltpu.make_async_copy(x_hbm.at[rows(t)], x_buf, sems.at[0])
    store_x = lambda t: pltpu.make_async_copy(x_buf, o_hbm.at[rows(t)], sems.at[1])
    store_un = lambda t: pltpu.make_async_copy(un_buf, un_hbm.at[rows(t)], sems.at[2])

    def wait_stores(t):
        store_x(t).wait()
        store_un(t).wait()

    def swiglu_chunk():
        u = u_ref[...]
        a = jnp.dot(u, wa_ref[0], preferred_element_type=F32)
        b = jnp.dot(u, wb_ref[0], preferred_element_type=F32)
        h = (a * jax.nn.sigmoid(a) * b).astype(BF16)
        return jnp.dot(h, wo_ref[0], preferred_element_type=F32)

    def finish():
        x = x_buf[...] + 0.5 * _rms(acc_ref[...], g_ref[gi:gi + 1, :])
        x_buf[...] = x
        un_buf[...] = _rms(x, gn_ref[gni:gni + 1, :]).astype(BF16)

    @pl.when((j == fetch_step) & (i > 0) & (i < nt))
    def _():
        wait_stores(i - 1)

    @pl.when((j == fetch_step) & (i < nt))
    def _():
        fetch_x(i).start()

    @pl.when((j == 0) & (i == 0))
    def _():
        acc_ref[...] = swiglu_chunk()

    @pl.when((j == 0) & (i > 0) & (i < nt))
    def _():
        fetch_x(i - 1).wait()
        finish()
        acc_ref[...] = swiglu_chunk()
        store_x(i - 1).start()
        store_un(i - 1).start()

    @pl.when((j == 0) & (i == nt))
    def _():
        fetch_x(i - 1).wait()
        finish()
        store_x(i - 1).start()
        store_un(i - 1).start()

    @pl.when((j == 1) & (i == nt))
    def _():
        wait_stores(i - 1)

    @pl.when((j > 0) & (i < nt))
    def _():
        acc_ref[...] += swiglu_chunk()


def _ffn(u2, x2, g, gn, w_in, w_out, layer, gi, gni):
    T, D = x2.shape
    F = w_out.shape[1]
    tm = _tile(T, 1024)
    tf = _tile(F, 512)
    nt = T // tm
    nj = F // tf
    assert nj >= 2
    fetch_step = min(nj // 2, nj - 1)
    assert fetch_step >= 1
    chunk = lambda i, j: jnp.where(i < nt, j, nj - 1)
    hbm = pl.BlockSpec(memory_space=pl.ANY)
    return pl.pallas_call(
        functools.partial(_ffn_body, gi=gi, gni=gni, nt=nt, tm=tm, fetch_step=fetch_step),
        out_shape=(jax.ShapeDtypeStruct((T, D), F32), jax.ShapeDtypeStruct((T, D), BF16)),
        grid=(nt + 1, nj),
        in_specs=[
            pl.BlockSpec((tm, D), lambda i, j: (jnp.minimum(i, nt - 1), 0)),
            hbm,
            pl.BlockSpec(g.shape, lambda i, j: (0, 0)),
            pl.BlockSpec(gn.shape, lambda i, j: (0, 0)),
            pl.BlockSpec((1, D, tf), lambda i, j: (layer, 0, chunk(i, j))),
            pl.BlockSpec((1, D, tf), lambda i, j: (layer, 0, chunk(i, j) + nj)),
            pl.BlockSpec((1, tf, D), lambda i, j: (layer, chunk(i, j), 0)),
        ],
        out_specs=(hbm, hbm),
        scratch_shapes=[pltpu.VMEM((tm, D), F32), pltpu.VMEM((tm, D), F32), pltpu.VMEM((tm, D), BF16),
                        pltpu.SemaphoreType.DMA((3,))],
        compiler_params=_params(("arbitrary", "arbitrary")),
        name="ffn",
    )(u2, x2, g, gn, w_in, w_in, w_out)


def _fox_proj_body(u_ref, wq_ref, wk_ref, wv_ref, qt_ref, k_ref, vt_ref, *, heads_per_chunk):
    u = u_ref[0]

    def transposed(t, out_ref):
        for c in range(heads_per_chunk):
            out_ref[0, c, 0:HEAD_DIM, :] = jnp.transpose(t[:, c * HEAD_DIM:(c + 1) * HEAD_DIM]).astype(BF16)

    q = jnp.dot(u, wq_ref[...], preferred_element_type=F32)
    transposed(q * (HEAD_DIM ** -0.5 * LOG2E), qt_ref)
    transposed(jnp.dot(u, wv_ref[...], preferred_element_type=F32), vt_ref)
    for c in range(heads_per_chunk):
        vt_ref[0, c, HEAD_DIM:, :] = jnp.ones((V_ONES_ROWS, u.shape[0]), BF16)
    k_ref[0] = jnp.dot(u, wk_ref[...], preferred_element_type=F32).astype(BF16)


def _fox_proj(u, w_qkv, n_heads):
    B, S, D = u.shape
    fox_w = n_heads * HEAD_DIM
    tm = _tile(S, 1024)
    tn = _tile(fox_w, 512)
    npk = fox_w // tn
    hpc = tn // HEAD_DIM
    t_spec = lambda rows: pl.BlockSpec((1, hpc, rows, tm), lambda b, i, j: (b, j, 0, i))
    t_shape = lambda rows: jax.ShapeDtypeStruct((B, n_heads, rows, S), BF16)
    v_rows = HEAD_DIM + V_ONES_ROWS
    return pl.pallas_call(
        functools.partial(_fox_proj_body, heads_per_chunk=hpc),
        out_shape=(t_shape(HEAD_DIM), jax.ShapeDtypeStruct((B, S, fox_w), BF16), t_shape(v_rows)),
        grid=(B, S // tm, npk),
        in_specs=[pl.BlockSpec((1, tm, D), lambda b, i, j: (b, i, 0)),
                  pl.BlockSpec((D, tn), lambda b, i, j: (0, j)),
                  pl.BlockSpec((D, tn), lambda b, i, j: (0, j + npk)),
                  pl.BlockSpec((D, tn), lambda b, i, j: (0, j + 2 * npk))],
        out_specs=(t_spec(HEAD_DIM), pl.BlockSpec((1, tm, tn), lambda b, i, j: (b, i, j)), t_spec(v_rows)),
        compiler_params=_params(("parallel", "parallel", "arbitrary")),
        name="fox_proj",
    )(u, w_qkv, w_qkv, w_qkv)


def _rope(t, cos, sin):
    return t * cos + pltpu.roll(t, HEAD_DIM // 2, 1) * sin


def _dil_proj_body(u_ref, wq_ref, wk_ref, wv_ref, cos_ref, sin_ref, q_ref, k_ref, v_ref, *scr, d, tm, tn):
    u = u_ref[0]
    heads = tn // HEAD_DIM
    for kind, (w_ref, out_ref) in enumerate(((wq_ref, q_ref), (wk_ref, k_ref), (wv_ref, v_ref))):
        t = jnp.dot(u, w_ref[...], preferred_element_type=F32)
        for c in range(heads):
            cols = slice(c * HEAD_DIM, (c + 1) * HEAD_DIM)
            sl = t[:, cols]
            if kind < 2:
                sl = _rope(sl, cos_ref[...], sin_ref[...])
            if kind == 0:
                sl = sl * HEAD_DIM ** -0.5
            if d == 1:
                out_ref[0, 0, :, cols] = sl.astype(BF16)
            else:
                slot = kind * heads + c
                pitch = d + DIL_PAD_ROWS if d % 8 == 0 else d
                if pitch == d:
                    scr[0][slot, 0:tm, :] = sl
                else:
                    for a in range(tm // d):
                        scr[0][slot, a * pitch:a * pitch + d, :] = sl[a * d:(a + 1) * d, :]
                for r in range(d):
                    out_ref[0, r, :, cols] = scr[0][slot, pl.ds(r, tm // d, stride=pitch), :].astype(BF16)


def _dil_proj(u, w_qkv, cos, sin, group, d, n_groups, dil_out_w):
    B, S, D = u.shape
    tm = _tile(S, 1024)
    tn = dil_out_w
    assert tm % (d * 16) == 0
    shape = jax.ShapeDtypeStruct((B, d, S // d, tn), BF16)
    out_spec = pl.BlockSpec((1, d, tm // d, tn), lambda b, i: (b, 0, i, 0))
    w_spec = lambda kind: pl.BlockSpec((D, tn), lambda b, i: (0, kind * n_groups + group))
    return pl.pallas_call(
        functools.partial(_dil_proj_body, d=d, tm=tm, tn=tn),
        out_shape=(shape, shape, shape),
        grid=(B, S // tm),
        in_specs=[pl.BlockSpec((1, tm, D), lambda b, i: (b, i, 0)), w_spec(0), w_spec(1), w_spec(2),
                  pl.BlockSpec((tm, HEAD_DIM), lambda b, i: (i, 0)),
                  pl.BlockSpec((tm, HEAD_DIM), lambda b, i: (i, 0))],
        out_specs=(out_spec, out_spec, out_spec),
        scratch_shapes=[] if d == 1 else [
            pltpu.VMEM((3 * tn // HEAD_DIM, tm + (tm // d) * DIL_PAD_ROWS, HEAD_DIM), F32)],
        compiler_params=_params(("parallel", "parallel")),
        name="dil_proj",
    )(u, w_qkv, w_qkv, w_qkv, cos, sin)


def _gate_proj_body(u_ref, w_ref, o_ref):
    acc = jnp.dot(u_ref[0], w_ref[...], preferred_element_type=F32)
    o_ref[0] = jax.nn.sigmoid(acc).astype(BF16)


def _gate_proj(u, w_g):
    B, S, D = u.shape
    N = w_g.shape[1]
    tm = _tile(S, 1024)
    tn = _tile(N, 2048)
    return pl.pallas_call(
        _gate_proj_body,
        out_shape=jax.ShapeDtypeStruct((B, S, N), BF16),
        grid=(B, S // tm, N // tn),
        in_specs=[pl.BlockSpec((1, tm, D), lambda b, i, j: (b, i, 0)),
                  pl.BlockSpec((D, tn), lambda b, i, j: (0, j))],
        out_specs=pl.BlockSpec((1, tm, tn), lambda b, i, j: (b, i, j)),
        compiler_params=_params(("parallel", "parallel", "arbitrary")),
        name="gate_proj",
    )(u, w_g)


def _scan_body(u_ref, wf_ref, b_ref, e_ref, carry_ref, *, chunk, n_heads):
    tm = u_ref.shape[1]

    @pl.when(pl.program_id(1) == 0)
    def _():
        carry_ref[...] = jnp.zeros(carry_ref.shape, F32)

    row = lax.broadcasted_iota(jnp.int32, (chunk, chunk), 0)
    col = lax.broadcasted_iota(jnp.int32, (chunk, chunk), 1)
    tri = jnp.where(col <= row, 1.0, 0.0).astype(BF16)
    pr = lax.broadcasted_iota(jnp.int32, (HEAD_DIM, HEAD_DIM), 0)
    pc = lax.broadcasted_iota(jnp.int32, (HEAD_DIM, HEAD_DIM), 1)
    place = [jnp.where((pc == N_PIECES * pr + k) & (pr < n_heads), 1.0, 0.0).astype(BF16)
             for k in range(N_PIECES)]

    t = jnp.dot(u_ref[0], wf_ref[...], preferred_element_type=F32) + b_ref[...]
    ls = jnp.minimum(t, 0.0) - jnp.log1p(jnp.exp(-jnp.abs(t)))
    carry = carry_ref[...]
    for c in range(tm // chunk):
        cs = carry
        for piece in _split3(ls[c * chunk:(c + 1) * chunk]):
            cs = cs + jnp.dot(tri, piece, preferred_element_type=F32)
        e = jnp.zeros((chunk, HEAD_DIM), F32)
        for piece, pmat in zip(_split3(cs * (-LOG2E)), place):
            e = e + jnp.dot(piece, pmat, preferred_element_type=F32)
        e_ref[0, c * chunk:(c + 1) * chunk, :] = e.astype(BF16)
        carry = cs[chunk - 1:chunk, :]
    carry_ref[...] = carry


def _forget_scan(u, w_f, b_f):
    B, S, D = u.shape
    H = b_f.shape[0]
    assert N_PIECES * H <= HEAD_DIM
    tm = _tile(S, 1024)
    chunk = _tile(tm, 256)
    return pl.pallas_call(
        functools.partial(_scan_body, chunk=chunk, n_heads=H),
        out_shape=jax.ShapeDtypeStruct((B, S, HEAD_DIM), BF16),
        grid=(B, S // tm),
        in_specs=[pl.BlockSpec((1, tm, D), lambda b, i: (b, i, 0)),
                  pl.BlockSpec(w_f.shape, lambda b, i: (0, 0)),
                  pl.BlockSpec((1, HEAD_DIM), lambda b, i: (0, 0))],
        out_specs=pl.BlockSpec((1, tm, HEAD_DIM), lambda b, i: (b, i, 0)),
        scratch_shapes=[pltpu.VMEM((1, HEAD_DIM), F32)],
        compiler_params=_params(("parallel", "arbitrary")),
        name="forget_scan",
    )(u, w_f, jnp.pad(b_f, (0, HEAD_DIM - H)).reshape(1, HEAD_DIM))


def _fox_body(qt_ref, k_ref, e_ref, vt_ref, o_ref, m_ref, acc_ref, sa_ref, sb_ref, *, tq, hg):
    h0 = pl.program_id(1) * hg
    qi = pl.program_id(2)
    r = lax.broadcasted_iota(jnp.int32, (HEAD_DIM, tq), 0)
    qts = []
    for hh in range(hg):
        lo = N_PIECES * (h0 + hh)
        sel = jnp.where((r >= lo) & (r < lo + N_PIECES), 1.0, 0.0).astype(BF16)
        qts.append(jnp.concatenate([qt_ref[0, hh], sel], axis=0))

    m_ref[...] = jnp.full(m_ref.shape, MASK_VALUE, F32)
    acc_ref[...] = jnp.zeros(acc_ref.shape, F32)

    def score(j, dst, hh):
        k0 = pl.multiple_of(j * tq, tq)
        cols = slice(hh * HEAD_DIM, (hh + 1) * HEAD_DIM)
        kk = jnp.concatenate([k_ref[0, pl.ds(k0, tq), cols], e_ref[0, pl.ds(k0, tq), :]], axis=1)
        dst[hh] = jnp.dot(kk, qts[hh], preferred_element_type=F32)

    def scores(j, dst):
        for hh in range(hg):
            score(j, dst, hh)

    def softmax(src, mask):
        ps, alphas = [], []
        for hh in range(hg):
            s = src[hh] if mask is None else jnp.where(mask, src[hh], MASK_VALUE)
            m_prev = m_ref[hh]
            m_new = jnp.maximum(m_prev, jnp.max(s, axis=0, keepdims=True))
            alphas.append(jnp.exp2(m_prev - m_new))
            ps.append(jnp.exp2((s - m_new).astype(BF16)))
            m_ref[hh] = m_new
        return ps, alphas

    def value(j, ps, alphas, hh):
        k0 = pl.multiple_of(j * tq, tq)
        pv = jnp.dot(vt_ref[0, hh, :, pl.ds(k0, tq)], ps[hh], preferred_element_type=F32)
        acc_ref[hh] = alphas[hh] * acc_ref[hh] + pv

    def values(j, ps, alphas):
        for hh in range(hg):
            value(j, ps, alphas, hh)

    def step(j, src, dst):
        ps, alphas = softmax(src, None)
        score(j + 1, dst, 0)
        for hh in range(hg):
            value(j, ps, alphas, hh)
            if hh + 1 < hg:
                score(j + 1, dst, hh + 1)

    def last(src):
        key = lax.broadcasted_iota(jnp.int32, (tq, tq), 0)
        qry = lax.broadcasted_iota(jnp.int32, (tq, tq), 1)
        ps, alphas = softmax(src, key <= qry)
        values(qi, ps, alphas)

    scores(0, sa_ref)

    def two_steps(t, carry):
        step(2 * t, sa_ref, sb_ref)
        step(2 * t + 1, sb_ref, sa_ref)
        return carry

    lax.fori_loop(0, lax.div(qi, 2), two_steps, 0)

    @pl.when(lax.rem(qi, 2) == 0)
    def _():
        last(sa_ref)

    @pl.when(lax.rem(qi, 2) == 1)
    def _():
        step(qi - 1, sa_ref, sb_ref)
        last(sb_ref)

    for hh in range(hg):
        o = acc_ref[hh, 0:HEAD_DIM, :] / acc_ref[hh, HEAD_DIM:HEAD_DIM + 1, :]
        o_ref[0, :, hh * HEAD_DIM:(hh + 1) * HEAD_DIM] = jnp.transpose(o).astype(o_ref.dtype)


def _fox(qt, k, e, vt):
    B, H, _, S = qt.shape
    v_rows = vt.shape[2]
    tq = _tile(S, 512)
    hg = _tile(H, 4)
    return pl.pallas_call(
        functools.partial(_fox_body, tq=tq, hg=hg),
        out_shape=jax.ShapeDtypeStruct((B, S, H * HEAD_DIM), BF16),
        grid=(B, H // hg, S // tq),
        in_specs=[
            pl.BlockSpec((1, hg, HEAD_DIM, tq), lambda b, h, i: (b, h, 0, i)),
            pl.BlockSpec((1, S, hg * HEAD_DIM), lambda b, h, i: (b, 0, h)),
            pl.BlockSpec((1, S, HEAD_DIM), lambda b, h, i: (b, 0, 0)),
            pl.BlockSpec((1, hg, v_rows, S), lambda b, h, i: (b, h, 0, 0)),
        ],
        out_specs=pl.BlockSpec((1, tq, hg * HEAD_DIM), lambda b, h, i: (b, i, h)),
        scratch_shapes=[pltpu.VMEM((hg, 1, tq), F32), pltpu.VMEM((hg, v_rows, tq), F32),
                        pltpu.VMEM((hg, tq, tq), F32), pltpu.VMEM((hg, tq, tq), F32)],
        compiler_params=_params(("parallel", "parallel", "arbitrary")),
        name="fox_attention",
    )(qt, k, e, vt)


def _dil_body(*refs, dils, span):
    ng = len(dils)
    q_refs, k_refs, v_refs = refs[:ng], refs[ng:2 * ng], refs[2 * ng:3 * ng]
    o_ref, o_scr, lse_scr = refs[3 * ng:]
    nk = DIL_BLOCK
    i = pl.program_id(2)
    qi = lax.broadcasted_iota(jnp.int32, (nk, 2 * nk), 0)
    ki = lax.broadcasted_iota(jnp.int32, (nk, 2 * nk), 1)
    band = (ki >= qi) & (ki <= qi + nk)
    ones = jnp.ones((2 * nk, HEAD_DIM), BF16)

    for g, d in enumerate(dils):
        rows = span // d
        nblk = rows // nk
        q_ref, k_ref, v_ref = q_refs[g], k_refs[g], v_refs[g]

        def several(it, carry, g=g, d=d, rows=rows, nblk=nblk, q_ref=q_ref, k_ref=k_ref, v_ref=v_ref):
            where, ss = [], []
            for j in range(DIL_UNROLL):
                idx = it * DIL_UNROLL + j
                r = lax.div(idx, nblk)
                n = lax.rem(idx, nblk)
                row0 = i * rows + n * nk
                prev0 = pl.multiple_of(jnp.maximum(row0 - nk, 0), nk)
                cur0 = pl.multiple_of(row0, nk)
                q = q_ref[0, r, pl.ds(pl.multiple_of(n * nk, nk), nk), :]
                kk = jnp.concatenate([k_ref[0, r, pl.ds(prev0, nk), :], k_ref[0, r, pl.ds(cur0, nk), :]],
                                     axis=0)
                s = lax.dot_general(q, kk, (((1,), (1,)), ((), ())), preferred_element_type=F32)
                ss.append(jnp.where(band & ((ki >= nk) | (row0 > 0)), s, MASK_VALUE))
                where.append((r, n, prev0, cur0))
            ps, ms = [], []
            for s in ss:
                m = jnp.max(s, axis=1, keepdims=True)
                ps.append(jnp.exp((s - m).astype(BF16)))
                ms.append(m)
            for (r, n, prev0, cur0), p, m in zip(where, ps, ms):
                vv = jnp.concatenate([v_ref[0, r, pl.ds(prev0, nk), :], v_ref[0, r, pl.ds(cur0, nk), :]],
                                     axis=0)
                ov = jnp.dot(p, jnp.concatenate([vv, ones], axis=1), preferred_element_type=F32)
                l = ov[:, HEAD_DIM:]
                o = ov[:, :HEAD_DIM] / l
                lse = m + jnp.log(l)
                tok0 = n * (nk * d) + r
                if d == 1:
                    o_scr[g, pl.ds(pl.multiple_of(tok0, nk), nk), :] = o
                    lse_scr[g, pl.ds(pl.multiple_of(tok0, nk), nk), :] = lse
                else:
                    o_scr[g, pl.ds(tok0, nk, stride=d), :] = o
                    lse_scr[g, pl.ds(tok0, nk, stride=d), :] = lse
            return carry

        assert (d * nblk) % DIL_UNROLL == 0
        lax.fori_loop(0, d * nblk // DIL_UNROLL, several, 0)

    lse = [lse_scr[g] for g in range(ng)]
    top = functools.reduce(jnp.maximum, lse)
    w = [jnp.exp(v - top) for v in lse]
    den = functools.reduce(lambda a, b: a + b, w)
    num = functools.reduce(lambda a, b: a + b, [w[g] * o_scr[g] for g in range(ng)])
    o_ref[0] = (num / den).astype(o_ref.dtype)


def _dilated(qs, ks, vs):
    B, d0, L0, W = qs[0].shape
    S = d0 * L0
    dils = tuple(t.shape[1] for t in qs)
    span = _tile(S, DIL_SPAN)
    assert all(span % (d * DIL_BLOCK) == 0 for d in dils)
    q_specs = [pl.BlockSpec((1, d, span // d, HEAD_DIM), lambda b, h, i: (b, 0, i, h)) for d in dils]
    kv_specs = [pl.BlockSpec((1, d, S // d, HEAD_DIM), lambda b, h, i: (b, 0, 0, h)) for d in dils]
    ng = len(dils)
    return pl.pallas_call(
        functools.partial(_dil_body, dils=dils, span=span),
        out_shape=jax.ShapeDtypeStruct((B, S, W), BF16),
        grid=(B, W // HEAD_DIM, S // span),
        in_specs=q_specs + kv_specs + kv_specs,
        out_specs=pl.BlockSpec((1, span, HEAD_DIM), lambda b, h, i: (b, i, h)),
        scratch_shapes=[pltpu.VMEM((ng, span, HEAD_DIM), F32), pltpu.VMEM((ng, span, HEAD_DIM), F32)],
        compiler_params=_params(("parallel", "parallel", "arbitrary")),
        name="dilated_attention",
    )(*qs, *ks, *vs)


def _merge_body(x_ref, g_ref, oa_ref, ob_ref, ga_ref, gb_ref, wa_ref, wb_ref, wo_ref, out_ref, un_ref, *,
                gi, gni):
    y_a = jnp.dot(oa_ref[0], wa_ref[...], preferred_element_type=F32)
    y_b = jnp.dot(ob_ref[0], wb_ref[...], preferred_element_type=F32)
    merged = ga_ref[0].astype(F32) * y_a + gb_ref[0].astype(F32) * y_b
    y = jnp.dot(merged.astype(BF16), wo_ref[...], preferred_element_type=F32)
    x = x_ref[0] + _rms(y, g_ref[gi:gi + 1, :])
    out_ref[0] = x
    un_ref[0] = _rms(x, g_ref[gni:gni + 1, :]).astype(BF16)


def _merge(x, g, o_a, o_b, gates, w_br_a, w_br_b, w_out, gi, gni):
    B, S, D = x.shape
    tm = _tile(S, 512)
    row = lambda w, c=0: pl.BlockSpec((1, tm, w), lambda b, i: (b, i, c))
    full = lambda a: pl.BlockSpec(a.shape, lambda b, i: (0, 0), pipeline_mode=pl.Buffered(1))
    return pl.pallas_call(
        functools.partial(_merge_body, gi=gi, gni=gni),
        out_shape=(jax.ShapeDtypeStruct((B, S, D), F32), jax.ShapeDtypeStruct((B, S, D), BF16)),
        grid=(B, S // tm),
        in_specs=[row(D), full(g), row(o_a.shape[-1]), row(o_b.shape[-1]), row(D, 0), row(D, 1),
                  full(w_br_a), full(w_br_b), full(w_out)],
        out_specs=(row(D), row(D)),
        compiler_params=_params(("parallel", "parallel")),
        name="mixer_merge",
    )(x, g, o_a, o_b, gates, gates, w_br_a, w_br_b, w_out)


def _ple_body(x_ref, u_ref, g_ref, gn_ref, p_ref, wp_ref, wg_ref, out_ref, un_ref, *, gi, gni):
    gate = jax.nn.sigmoid(jnp.dot(u_ref[0], wg_ref[...], preferred_element_type=F32))
    e = jnp.dot(p_ref[0, 0].astype(BF16), wp_ref[...], preferred_element_type=F32)
    x = x_ref[0] + _rms(gate * e, g_ref[gi:gi + 1, :])
    out_ref[0] = x
    un_ref[0] = _rms(x, gn_ref[gni:gni + 1, :]).astype(BF16)


def _ple(x, u, g, gn, p, layer, w_proj, w_gate, gi, gni):
    B, S, D = x.shape
    P = p.shape[-1]
    tm = _tile(S, 512)
    full = lambda a: pl.BlockSpec(a.shape, lambda b, i: (0, 0))
    row = pl.BlockSpec((1, tm, D), lambda b, i: (b, i, 0))
    return pl.pallas_call(
        functools.partial(_ple_body, gi=gi, gni=gni),
        out_shape=(jax.ShapeDtypeStruct((B, S, D), F32), jax.ShapeDtypeStruct((B, S, D), BF16)),
        grid=(B, S // tm),
        in_specs=[row, row, full(g), full(gn),
                  pl.BlockSpec((1, 1, tm, P), lambda b, i: (layer, b, i, 0)),
                  full(w_proj), full(w_gate)],
        out_specs=(row, row),
        compiler_params=_params(("parallel", "parallel")),
        name="ple_gate",
    )(x, u, g, gn, p, w_proj, w_gate)


def _rope_tables(S):
    half = HEAD_DIM // 2
    inv = ROPE_THETA ** (-jnp.arange(half, dtype=F32) * 2.0 / HEAD_DIM)
    ang = jnp.arange(S, dtype=F32)[:, None] * inv[None, :]
    cos, sin = jnp.cos(ang), jnp.sin(ang)
    return jnp.concatenate([cos, cos], axis=1), jnp.concatenate([-sin, sin], axis=1)


def _mixer(x, u, g, w_in, b_f, w_br_a, w_br_b, w_out, cos, sin):
    B, S, D = x.shape
    fox_w = w_br_a.shape[0]
    dil_out_w = w_br_b.shape[0]
    n_groups = len(DIL_PATTERNS)
    dil_w = n_groups * dil_out_w
    n_fox = b_f.shape[0]
    assert fox_w == n_fox * HEAD_DIM
    assert w_in.shape[1] == 3 * fox_w + n_fox + 3 * dil_w + 2 * D
    for window, d in DIL_PATTERNS:
        assert window // d == DIL_BLOCK

    f_lo = 3 * fox_w
    b_lo = f_lo + n_fox
    g_lo = b_lo + 3 * dil_w
    w_fox = w_in[:, :f_lo].astype(BF16)
    w_f = jnp.pad(w_in[:, f_lo:b_lo], ((0, 0), (0, HEAD_DIM - n_fox))).astype(BF16)
    w_dil = w_in[:, b_lo:g_lo].astype(BF16)
    w_gate = w_in[:, g_lo:].astype(BF16)

    qt, k, vt = _fox_proj(u, w_fox, n_fox)
    e = _forget_scan(u, w_f, b_f)
    o_a = _fox(qt, k, e, vt)

    qs, ks, vs = [], [], []
    for gidx, (_, d) in enumerate(DIL_PATTERNS):
        q, kk, v = _dil_proj(u, w_dil, cos, sin, gidx, d, n_groups, dil_out_w)
        qs.append(q), ks.append(kk), vs.append(v)
    o_b = _dilated(qs, ks, vs)

    gates = _gate_proj(u, w_gate)
    return _merge(x, g, o_a, o_b, gates, w_br_a.astype(BF16), w_br_b.astype(BF16), w_out.astype(BF16), 3, 4)


def kernel(x, p, norm_g, ffn1_w_in, ffn1_w_out, mix_w_in, fox_b_f, mix_w_br_a, mix_w_br_b, mix_w_out,
           ffn2_w_in, ffn2_w_out, ple_w_proj, ple_w_gate):
    B, S, D = x.shape
    T = B * S
    depth = norm_g.shape[0]
    cos, sin = _rope_tables(S)
    w1_in, w1_out, w2_in, w2_out = (_to_bf16(w) for w in (ffn1_w_in, ffn1_w_out, ffn2_w_in, ffn2_w_out))
    x = x.reshape(T, D)
    u = _norm(x, norm_g[0], 0)
    for i in range(depth):
        g = norm_g[i]
        gn = norm_g[(i + 1) % depth]
        x, u = _ffn(u, x, g, g, w1_in, w1_out, i, 1, 2)
        x, u = _mixer(x.reshape(B, S, D), u.reshape(B, S, D), g, mix_w_in[i], fox_b_f[i], mix_w_br_a[i],
                      mix_w_br_b[i], mix_w_out[i], cos, sin)
        x, u = _ffn(u.reshape(T, D), x.reshape(T, D), g, g, w2_in, w2_out, i, 5, 6)
        x, u = _ple(x.reshape(B, S, D), u.reshape(B, S, D), g, gn, p, i, ple_w_proj[i].astype(BF16),
                    ple_w_gate[i].astype(BF16), 7, 0)
        x, u = x.reshape(T, D), u.reshape(T, D)
    return x.reshape(B, S, D)
```

```python
import functools
import math

import jax
import jax.numpy as jnp
from jax import lax
from jax.experimental import pallas as pl
from jax.experimental.pallas import tpu as pltpu

HEAD_DIM = 128
DIL_PATTERNS = ((128, 1), (512, 4), (2048, 16))
DIL_BLOCK = 128
DIL_SPAN = 2048
DIL_PAD_ROWS = 8
DIL_UNROLL = 16
ROPE_THETA = 10000.0
RMS_EPS = 1e-6
MASK_VALUE = -1e30
LOG2E = math.log2(math.e)
N_PIECES = 3
CAST_BLOCK_ELEMS = 3 << 20
V_ONES_ROWS = 16
VMEM_LIMIT_V7X = 56 * 1024 * 1024

F32 = jnp.float32
BF16 = jnp.bfloat16


def _rms(x, g):
    return x * lax.rsqrt(jnp.mean(x * x, axis=-1, keepdims=True) + RMS_EPS) * g


def _tile(n, want):
    t = min(n, want)
    assert n % t == 0, (n, t)
    return t


def _params(sem, vmem=VMEM_LIMIT_V7X):
    return pltpu.CompilerParams(dimension_semantics=sem, vmem_limit_bytes=vmem)


def _split3(v):
    hi = v.astype(BF16)
    r = v - hi.astype(F32)
    mid = r.astype(BF16)
    lo = (r - mid.astype(F32)).astype(BF16)
    return hi, mid, lo


def _cast_body(w_ref, o_ref):
    o_ref[...] = w_ref[...].astype(BF16)


def _to_bf16(w):
    L, K, N = w.shape
    tk = min(K, max(16, 1 << int(math.log2(max(1, CAST_BLOCK_ELEMS // N)))))
    while K % tk:
        tk //= 2
    spec = pl.BlockSpec((1, tk, N), lambda l, i: (l, i, 0))
    return pl.pallas_call(
        _cast_body,
        out_shape=jax.ShapeDtypeStruct(w.shape, BF16),
        grid=(L, K // tk),
        in_specs=[spec],
        out_specs=spec,
        compiler_params=_params(("parallel", "parallel")),
        name="weight_cast",
    )(w)


def _norm_body(x_ref, g_ref, u_ref, *, gi):
    u_ref[...] = _rms(x_ref[...], g_ref[gi:gi + 1, :]).astype(BF16)


def _norm(x2, g, gi):
    T, D = x2.shape
    tm = _tile(T, 1024)
    return pl.pallas_call(
        functools.partial(_norm_body, gi=gi),
        out_shape=jax.ShapeDtypeStruct((T, D), BF16),
        grid=(T // tm,),
        in_specs=[pl.BlockSpec((tm, D), lambda i: (i, 0)), pl.BlockSpec(g.shape, lambda i: (0, 0))],
        out_specs=pl.BlockSpec((tm, D), lambda i: (i, 0)),
        compiler_params=_params(("parallel",)),
        name="first_norm",
    )(x2, g)


def _ffn_body(u_ref, x_hbm, g_ref, gn_ref, wa_ref, wb_ref, wo_ref, o_hbm, un_hbm, acc_ref, x_buf, un_buf, sems,
              *, gi, gni, nt, tm, fetch_step):
    i = pl.program_id(0)
    j = pl.program_id(1)

    def rows(t):
        return pl.ds(pl.multiple_of(t * tm, tm), tm)

    fetch_x = lambda t: pltpu.make_async_copy(x_hbm.at[rows(t)], x_buf, sems.at[0])
    store_x = lambda t: pltpu.make_async_copy(x_buf, o_hbm.at[rows(t)], sems.at[1])
    store_un = lambda t: pltpu.make_async_copy(un_buf, un_hbm.at[rows(t)], sems.at[2])

    def wait_stores(t):
        store_x(t).wait()
        store_un(t).wait()

    def swiglu_chunk():
        u = u_ref[...]
        a = jnp.dot(u, wa_ref[0], preferred_element_type=F32)
        b = jnp.dot(u, wb_ref[0], preferred_element_type=F32)
        h = (a * jax.nn.sigmoid(a) * b).astype(BF16)
        return jnp.dot(h, wo_ref[0], preferred_element_type=F32)

    def finish():
        x = x_buf[...] + _rms(acc_ref[...], 0.5 * g_ref[gi:gi + 1, :])
        x_buf[...] = x
        un_buf[...] = _rms(x, gn_ref[gni:gni + 1, :]).astype(BF16)

    @pl.when((j == fetch_step) & (i > 0) & (i < nt))
    def _():
        wait_stores(i - 1)

    @pl.when((j == fetch_step) & (i < nt))
    def _():
        fetch_x(i).start()

    @pl.when((j == 0) & (i == 0))
    def _():
        acc_ref[...] = swiglu_chunk()

    @pl.when((j == 0) & (i > 0) & (i < nt))
    def _():
        fetch_x(i - 1).wait()
        finish()
        acc_ref[...] = swiglu_chunk()
        store_x(i - 1).start()
        store_un(i - 1).start()

    @pl.when((j == 0) & (i == nt))
    def _():
        fetch_x(i - 1).wait()
        finish()
        store_x(i - 1).start()
        store_un(i - 1).start()

    @pl.when((j == 1) & (i == nt))
    def _():
        wait_stores(i - 1)

    @pl.when((j > 0) & (i < nt))
    def _():
        acc_ref[...] += swiglu_chunk()


def _ffn(u2, x2, g, gn, w_in, w_out, layer, gi, gni):
    T, D = x2.shape
    F = w_out.shape[1]
    tm = _tile(T, 1024)
    tf = _tile(F, 512)
    nt = T // tm
    nj = F // tf
    assert nj >= 2
    fetch_step = min(nj // 2, nj - 1)
    assert fetch_step >= 1
    chunk = lambda i, j: jnp.where(i < nt, j, nj - 1)
    hbm = pl.BlockSpec(memory_space=pl.ANY)
    return pl.pallas_call(
        functools.partial(_ffn_body, gi=gi, gni=gni, nt=nt, tm=tm, fetch_step=fetch_step),
        out_shape=(jax.ShapeDtypeStruct((T, D), F32), jax.ShapeDtypeStruct((T, D), BF16)),
        grid=(nt + 1, nj),
        in_specs=[
            pl.BlockSpec((tm, D), lambda i, j: (jnp.minimum(i, nt - 1), 0)),
            hbm,
            pl.BlockSpec(g.shape, lambda i, j: (0, 0)),
            pl.BlockSpec(gn.shape, lambda i, j: (0, 0)),
            pl.BlockSpec((1, D, tf), lambda i, j: (layer, 0, chunk(i, j))),
            pl.BlockSpec((1, D, tf), lambda i, j: (layer, 0, chunk(i, j) + nj)),
            pl.BlockSpec((1, tf, D), lambda i, j: (layer, chunk(i, j), 0)),
        ],
        out_specs=(hbm, hbm),
        scratch_shapes=[pltpu.VMEM((tm, D), F32), pltpu.VMEM((tm, D), F32), pltpu.VMEM((tm, D), BF16),
                        pltpu.SemaphoreType.DMA((3,))],
        compiler_params=_params(("arbitrary", "arbitrary")),
        name="ffn",
    )(u2, x2, g, gn, w_in, w_in, w_out)


def _fox_proj_body(u_ref, wq_ref, wk_ref, wv_ref, qt_ref, k_ref, vt_ref, *, heads_per_chunk):
    u = u_ref[0]

    def transposed(t, out_ref):
        for c in range(heads_per_chunk):
            out_ref[0, c, 0:HEAD_DIM, :] = jnp.transpose(t[:, c * HEAD_DIM:(c + 1) * HEAD_DIM]).astype(BF16)

    q = jnp.dot(u, wq_ref[...], preferred_element_type=F32)
    transposed(q * (HEAD_DIM ** -0.5 * LOG2E), qt_ref)
    transposed(jnp.dot(u, wv_ref[...], preferred_element_type=F32), vt_ref)
    for c in range(heads_per_chunk):
        vt_ref[0, c, HEAD_DIM:, :] = jnp.ones((V_ONES_ROWS, u.shape[0]), BF16)
    k_ref[0] = jnp.dot(u, wk_ref[...], preferred_element_type=F32).astype(BF16)


def _fox_proj(u, w_qkv, n_heads):
    B, S, D = u.shape
    fox_w = n_heads * HEAD_DIM
    tm = _tile(S, 1024)
    tn = _tile(fox_w, 512)
    npk = fox_w // tn
    hpc = tn // HEAD_DIM
    t_spec = lambda rows: pl.BlockSpec((1, hpc, rows, tm), lambda b, i, j: (b, j, 0, i))
    t_shape = lambda rows: jax.ShapeDtypeStruct((B, n_heads, rows, S), BF16)
    v_rows = HEAD_DIM + V_ONES_ROWS
    return pl.pallas_call(
        functools.partial(_fox_proj_body, heads_per_chunk=hpc),
        out_shape=(t_shape(HEAD_DIM), jax.ShapeDtypeStruct((B, S, fox_w), BF16), t_shape(v_rows)),
        grid=(B, S // tm, npk),
        in_specs=[pl.BlockSpec((1, tm, D), lambda b, i, j: (b, i, 0)),
                  pl.BlockSpec((D, tn), lambda b, i, j: (0, j)),
                  pl.BlockSpec((D, tn), lambda b, i, j: (0, j + npk)),
                  pl.BlockSpec((D, tn), lambda b, i, j: (0, j + 2 * npk))],
        out_specs=(t_spec(HEAD_DIM), pl.BlockSpec((1, tm, tn), lambda b, i, j: (b, i, j)), t_spec(v_rows)),
        compiler_params=_params(("parallel", "parallel", "arbitrary")),
        name="fox_proj",
    )(u, w_qkv, w_qkv, w_qkv)


def _rope(t, cos, sin):
    return t * cos + pltpu.roll(t, HEAD_DIM // 2, 1) * sin


def _dil_proj_body(u_ref, wq_ref, wk_ref, wv_ref, cos_ref, sin_ref, q_ref, k_ref, v_ref, *scr, d, tm, tn):
    u = u_ref[0]
    heads = tn // HEAD_DIM
    for kind, (w_ref, out_ref) in enumerate(((wq_ref, q_ref), (wk_ref, k_ref), (wv_ref, v_ref))):
        t = jnp.dot(u, w_ref[...], preferred_element_type=F32)
        for c in range(heads):
            cols = slice(c * HEAD_DIM, (c + 1) * HEAD_DIM)
            sl = t[:, cols]
            if kind < 2:
                sl = _rope(sl, cos_ref[...], sin_ref[...])
            if kind == 0:
                sl = sl * HEAD_DIM ** -0.5
            if d == 1:
                out_ref[0, 0, :, cols] = sl.astype(BF16)
            else:
                slot = kind * heads + c
                pitch = d + DIL_PAD_ROWS if d % 8 == 0 else d
                if pitch == d:
                    scr[0][slot, 0:tm, :] = sl
                else:
                    for a in range(tm // d):
                        scr[0][slot, a * pitch:a * pitch + d, :] = sl[a * d:(a + 1) * d, :]
                for r in range(d):
                    out_ref[0, r, :, cols] = scr[0][slot, pl.ds(r, tm // d, stride=pitch), :].astype(BF16)


def _dil_proj(u, w_qkv, cos, sin, group, d, n_groups, dil_out_w):
    B, S, D = u.shape
    tm = _tile(S, 1024)
    tn = dil_out_w
    assert tm % (d * 16) == 0
    shape = jax.ShapeDtypeStruct((B, d, S // d, tn), BF16)
    out_spec = pl.BlockSpec((1, d, tm // d, tn), lambda b, i: (b, 0, i, 0))
    w_spec = lambda kind: pl.BlockSpec((D, tn), lambda b, i: (0, kind * n_groups + group))
    return pl.pallas_call(
        functools.partial(_dil_proj_body, d=d, tm=tm, tn=tn),
        out_shape=(shape, shape, shape),
        grid=(B, S // tm),
        in_specs=[pl.BlockSpec((1, tm, D), lambda b, i: (b, i, 0)), w_spec(0), w_spec(1), w_spec(2),
                  pl.BlockSpec((tm, HEAD_DIM), lambda b, i: (i, 0)),
                  pl.BlockSpec((tm, HEAD_DIM), lambda b, i: (i, 0))],
        out_specs=(out_spec, out_spec, out_spec),
        scratch_shapes=[] if d == 1 else [
            pltpu.VMEM((3 * tn // HEAD_DIM, tm + (tm // d) * DIL_PAD_ROWS, HEAD_DIM), F32)],
        compiler_params=_params(("parallel", "parallel")),
        name="dil_proj",
    )(u, w_qkv, w_qkv, w_qkv, cos, sin)


def _gate_proj_body(u_ref, w_ref, o_ref):
    acc = jnp.dot(u_ref[0], w_ref[...], preferred_element_type=F32)
    o_ref[0] = jax.nn.sigmoid(acc).astype(BF16)


def _gate_proj(u, w_g):
    B, S, D = u.shape
    N = w_g.shape[1]
    tm = _tile(S, 1024)
    tn = _tile(N, 2048)
    return pl.pallas_call(
        _gate_proj_body,
        out_shape=jax.ShapeDtypeStruct((B, S, N), BF16),
        grid=(B, S // tm, N // tn),
        in_specs=[pl.BlockSpec((1, tm, D), lambda b, i, j: (b, i, 0)),
                  pl.BlockSpec((D, tn), lambda b, i, j: (0, j))],
        out_specs=pl.BlockSpec((1, tm, tn), lambda b, i, j: (b, i, j)),
        compiler_params=_params(("parallel", "parallel", "arbitrary")),
        name="gate_proj",
    )(u, w_g)


def _scan_body(u_ref, wf_ref, b_ref, e_ref, carry_ref, *, chunk, n_heads):
    tm = u_ref.shape[1]

    @pl.when(pl.program_id(1) == 0)
    def _():
        carry_ref[...] = jnp.zeros(carry_ref.shape, F32)

    row = lax.broadcasted_iota(jnp.int32, (chunk, chunk), 0)
    col = lax.broadcasted_iota(jnp.int32, (chunk, chunk), 1)
    tri = jnp.where(col <= row, 1.0, 0.0).astype(BF16)
    pr = lax.broadcasted_iota(jnp.int32, (HEAD_DIM, HEAD_DIM), 0)
    pc = lax.broadcasted_iota(jnp.int32, (HEAD_DIM, HEAD_DIM), 1)
    place = [jnp.where((pc == N_PIECES * pr + k) & (pr < n_heads), 1.0, 0.0).astype(BF16)
             for k in range(N_PIECES)]

    t = jnp.dot(u_ref[0], wf_ref[...], preferred_element_type=F32) + b_ref[...]
    ls = jnp.minimum(t, 0.0) - jnp.log1p(jnp.exp(-jnp.abs(t)))
    carry = carry_ref[...]
    for c in range(tm // chunk):
        cs = carry
        for piece in _split3(ls[c * chunk:(c + 1) * chunk]):
            cs = cs + jnp.dot(tri, piece, preferred_element_type=F32)
        e = jnp.zeros((chunk, HEAD_DIM), F32)
        for piece, pmat in zip(_split3(cs * (-LOG2E)), place):
            e = e + jnp.dot(piece, pmat, preferred_element_type=F32)
        e_ref[0, c * chunk:(c + 1) * chunk, :] = e.astype(BF16)
        carry = cs[chunk - 1:chunk, :]
    carry_ref[...] = carry


def _forget_scan(u, w_f, b_f):
    B, S, D = u.shape
    H = b_f.shape[0]
    assert N_PIECES * H <= HEAD_DIM
    tm = _tile(S, 1024)
    chunk = _tile(tm, 256)
    return pl.pallas_call(
        functools.partial(_scan_body, chunk=chunk, n_heads=H),
        out_shape=jax.ShapeDtypeStruct((B, S, HEAD_DIM), BF16),
        grid=(B, S // tm),
        in_specs=[pl.BlockSpec((1, tm, D), lambda b, i: (b, i, 0)),
                  pl.BlockSpec(w_f.shape, lambda b, i: (0, 0)),
                  pl.BlockSpec((1, HEAD_DIM), lambda b, i: (0, 0))],
        out_specs=pl.BlockSpec((1, tm, HEAD_DIM), lambda b, i: (b, i, 0)),
        scratch_shapes=[pltpu.VMEM((1, HEAD_DIM), F32)],
        compiler_params=_params(("parallel", "arbitrary")),
        name="forget_scan",
    )(u, w_f, jnp.pad(b_f, (0, HEAD_DIM - H)).reshape(1, HEAD_DIM))


def _fox_body(qt_ref, k_ref, e_ref, vt_ref, o_ref, m_ref, acc_ref, sa_ref, sb_ref, *, tq, hg):
    h0 = pl.program_id(1) * hg
    qi = pl.program_id(2)
    r = lax.broadcasted_iota(jnp.int32, (HEAD_DIM, tq), 0)
    qts = []
    for hh in range(hg):
        lo = N_PIECES * (h0 + hh)
        sel = jnp.where((r >= lo) & (r < lo + N_PIECES), 1.0, 0.0).astype(BF16)
        qts.append(jnp.concatenate([qt_ref[0, hh], sel], axis=0))

    m_ref[...] = jnp.full(m_ref.shape, MASK_VALUE, F32)
    acc_ref[...] = jnp.zeros(acc_ref.shape, F32)

    def score(j, dst, hh):
        k0 = pl.multiple_of(j * tq, tq)
        cols = slice(hh * HEAD_DIM, (hh + 1) * HEAD_DIM)
        kk = jnp.concatenate([k_ref[0, pl.ds(k0, tq), cols], e_ref[0, pl.ds(k0, tq), :]], axis=1)
        dst[hh] = jnp.dot(kk, qts[hh], preferred_element_type=F32)

    def scores(j, dst):
        for hh in range(hg):
            score(j, dst, hh)

    def softmax(src, mask):
        ps, alphas = [], []
        for hh in range(hg):
            s = src[hh] if mask is None else jnp.where(mask, src[hh], MASK_VALUE)
            m_prev = m_ref[hh]
            m_new = jnp.maximum(m_prev, jnp.max(s, axis=0, keepdims=True))
            alphas.append(jnp.exp2(m_prev - m_new))
            ps.append(jnp.exp2((s - m_new).astype(BF16)))
            m_ref[hh] = m_new
        return ps, alphas

    def value(j, ps, alphas, hh):
        k0 = pl.multiple_of(j * tq, tq)
        pv = jnp.dot(vt_ref[0, hh, :, pl.ds(k0, tq)], ps[hh], preferred_element_type=F32)
        acc_ref[hh] = alphas[hh] * acc_ref[hh] + pv

    def values(j, ps, alphas):
        for hh in range(hg):
            value(j, ps, alphas, hh)

    def step(j, src, dst):
        ps, alphas = softmax(src, None)
        score(j + 1, dst, 0)
        for hh in range(hg):
            value(j, ps, alphas, hh)
            if hh + 1 < hg:
                score(j + 1, dst, hh + 1)

    def last(src):
        key = lax.broadcasted_iota(jnp.int32, (tq, tq), 0)
        qry = lax.broadcasted_iota(jnp.int32, (tq, tq), 1)
        ps, alphas = softmax(src, key <= qry)
        values(qi, ps, alphas)

    scores(0, sa_ref)

    def two_steps(t, carry):
        step(2 * t, sa_ref, sb_ref)
        step(2 * t + 1, sb_ref, sa_ref)
        return carry

    lax.fori_loop(0, lax.div(qi, 2), two_steps, 0)

    @pl.when(lax.rem(qi, 2) == 0)
    def _():
        last(sa_ref)

    @pl.when(lax.rem(qi, 2) == 1)
    def _():
        step(qi - 1, sa_ref, sb_ref)
        last(sb_ref)

    for hh in range(hg):
        o = acc_ref[hh, 0:HEAD_DIM, :] / acc_ref[hh, HEAD_DIM:HEAD_DIM + 1, :]
        o_ref[0, :, hh * HEAD_DIM:(hh + 1) * HEAD_DIM] = jnp.transpose(o).astype(o_ref.dtype)


def _fox(qt, k, e, vt):
    B, H, _, S = qt.shape
    v_rows = vt.shape[2]
    tq = _tile(S, 512)
    hg = _tile(H, 4)
    return pl.pallas_call(
        functools.partial(_fox_body, tq=tq, hg=hg),
        out_shape=jax.ShapeDtypeStruct((B, S, H * HEAD_DIM), BF16),
        grid=(B, H // hg, S // tq),
        in_specs=[
            pl.BlockSpec((1, hg, HEAD_DIM, tq), lambda b, h, i: (b, h, 0, i)),
            pl.BlockSpec((1, S, hg * HEAD_DIM), lambda b, h, i: (b, 0, h)),
            pl.BlockSpec((1, S, HEAD_DIM), lambda b, h, i: (b, 0, 0)),
            pl.BlockSpec((1, hg, v_rows, S), lambda b, h, i: (b, h, 0, 0)),
        ],
        out_specs=pl.BlockSpec((1, tq, hg * HEAD_DIM), lambda b, h, i: (b, i, h)),
        scratch_shapes=[pltpu.VMEM((hg, 1, tq), F32), pltpu.VMEM((hg, v_rows, tq), F32),
                        pltpu.VMEM((hg, tq, tq), F32), pltpu.VMEM((hg, tq, tq), F32)],
        compiler_params=_params(("parallel", "parallel", "arbitrary")),
        name="fox_attention",
    )(qt, k, e, vt)


def _dil_body(*refs, dils, span):
    ng = len(dils)
    q_refs, k_refs, v_refs = refs[:ng], refs[ng:2 * ng], refs[2 * ng:3 * ng]
    o_ref, o_scr, lse_scr = refs[3 * ng:]
    nk = DIL_BLOCK
    i = pl.program_id(2)
    qi = lax.broadcasted_iota(jnp.int32, (nk, 2 * nk), 0)
    ki = lax.broadcasted_iota(jnp.int32, (nk, 2 * nk), 1)
    band = (ki >= qi) & (ki <= qi + nk)
    ones = jnp.ones((2 * nk, HEAD_DIM), BF16)

    for g, d in enumerate(dils):
        rows = span // d
        nblk = rows // nk
        q_ref, k_ref, v_ref = q_refs[g], k_refs[g], v_refs[g]

        def several(it, carry, g=g, d=d, rows=rows, nblk=nblk, q_ref=q_ref, k_ref=k_ref, v_ref=v_ref):
            where, ss = [], []
            for j in range(DIL_UNROLL):
                idx = it * DIL_UNROLL + j
                r = lax.div(idx, nblk)
                n = lax.rem(idx, nblk)
                row0 = i * rows + n * nk
                prev0 = pl.multiple_of(jnp.maximum(row0 - nk, 0), nk)
                cur0 = pl.multiple_of(row0, nk)
                q = q_ref[0, r, pl.ds(pl.multiple_of(n * nk, nk), nk), :]
                kk = jnp.concatenate([k_ref[0, r, pl.ds(prev0, nk), :], k_ref[0, r, pl.ds(cur0, nk), :]],
                                     axis=0)
                s = lax.dot_general(q, kk, (((1,), (1,)), ((), ())), preferred_element_type=F32)
                ss.append(jnp.where(band & ((ki >= nk) | (row0 > 0)), s, MASK_VALUE))
                where.append((r, n, prev0, cur0))
            ps, ms = [], []
            for s in ss:
                m = jnp.max(s, axis=1, keepdims=True)
                ps.append(jnp.exp((s - m).astype(BF16)))
                ms.append(m)
            for (r, n, prev0, cur0), p, m in zip(where, ps, ms):
                vv = jnp.concatenate([v_ref[0, r, pl.ds(prev0, nk), :], v_ref[0, r, pl.ds(cur0, nk), :]],
                                     axis=0)
                ov = jnp.dot(p, jnp.concatenate([vv, ones], axis=1), preferred_element_type=F32)
                l = ov[:, HEAD_DIM:]
                o = ov[:, :HEAD_DIM] / l
                lse = m + jnp.log(l)
                tok0 = n * (nk * d) + r
                if d == 1:
                    o_scr[g, pl.ds(pl.multiple_of(tok0, nk), nk), :] = o
                    lse_scr[g, pl.ds(pl.multiple_of(tok0, nk), nk), :] = lse
                else:
                    o_scr[g, pl.ds(tok0, nk, stride=d), :] = o
                    lse_scr[g, pl.ds(tok0, nk, stride=d), :] = lse
            return carry

        assert (d * nblk) % DIL_UNROLL == 0
        lax.fori_loop(0, d * nblk // DIL_UNROLL, several, 0)

    lse = [lse_scr[g] for g in range(ng)]
    top = functools.reduce(jnp.maximum, lse)
    w = [jnp.exp(v - top) for v in lse]
    den = functools.reduce(lambda a, b: a + b, w)
    num = functools.reduce(lambda a, b: a + b, [w[g] * o_scr[g] for g in range(ng)])
    o_ref[0] = (num / den).astype(o_ref.dtype)


def _dilated(qs, ks, vs):
    B, d0, L0, W = qs[0].shape
    S = d0 * L0
    dils = tuple(t.shape[1] for t in qs)
    span = _tile(S, DIL_SPAN)
    assert all(span % (d * DIL_BLOCK) == 0 for d in dils)
    q_specs = [pl.BlockSpec((1, d, span // d, HEAD_DIM), lambda b, h, i: (b, 0, i, h)) for d in dils]
    kv_specs = [pl.BlockSpec((1, d, S // d, HEAD_DIM), lambda b, h, i: (b, 0, 0, h)) for d in dils]
    ng = len(dils)
    return pl.pallas_call(
        functools.partial(_dil_body, dils=dils, span=span),
        out_shape=jax.ShapeDtypeStruct((B, S, W), BF16),
        grid=(B, W // HEAD_DIM, S // span),
        in_specs=q_specs + kv_specs + kv_specs,
        out_specs=pl.BlockSpec((1, span, HEAD_DIM), lambda b, h, i: (b, i, h)),
        scratch_shapes=[pltpu.VMEM((ng, span, HEAD_DIM), F32), pltpu.VMEM((ng, span, HEAD_DIM), F32)],
        compiler_params=_params(("parallel", "parallel", "arbitrary")),
        name="dilated_attention",
    )(*qs, *ks, *vs)


def _merge_body(x_ref, g_ref, oa_ref, ob_ref, ga_ref, gb_ref, wa_ref, wb_ref, wo_ref, out_ref, un_ref, *,
                gi, gni):
    y_a = jnp.dot(oa_ref[0], wa_ref[...], preferred_element_type=F32)
    y_b = jnp.dot(ob_ref[0], wb_ref[...], preferred_element_type=F32)
    merged = ga_ref[0].astype(F32) * y_a + gb_ref[0].astype(F32) * y_b
    y = jnp.dot(merged.astype(BF16), wo_ref[...], preferred_element_type=F32)
    x = x_ref[0] + _rms(y, g_ref[gi:gi + 1, :])
    out_ref[0] = x
    un_ref[0] = _rms(x, g_ref[gni:gni + 1, :]).astype(BF16)


def _merge(x, g, o_a, o_b, gates, w_br_a, w_br_b, w_out, gi, gni):
    B, S, D = x.shape
    tm = _tile(S, 512)
    row = lambda w, c=0: pl.BlockSpec((1, tm, w), lambda b, i: (b, i, c))
    full = lambda a: pl.BlockSpec(a.shape, lambda b, i: (0, 0), pipeline_mode=pl.Buffered(1))
    return pl.pallas_call(
        functools.partial(_merge_body, gi=gi, gni=gni),
        out_shape=(jax.ShapeDtypeStruct((B, S, D), F32), jax.ShapeDtypeStruct((B, S, D), BF16)),
        grid=(B, S // tm),
        in_specs=[row(D), full(g), row(o_a.shape[-1]), row(o_b.shape[-1]), row(D, 0), row(D, 1),
                  full(w_br_a), full(w_br_b), full(w_out)],
        out_specs=(row(D), row(D)),
        compiler_params=_params(("parallel", "parallel")),
        name="mixer_merge",
    )(x, g, o_a, o_b, gates, gates, w_br_a, w_br_b, w_out)


def _ple_body(x_ref, u_ref, g_ref, gn_ref, p_ref, wp_ref, wg_ref, out_ref, un_ref, *, gi, gni):
    gate = jax.nn.sigmoid(jnp.dot(u_ref[0], wg_ref[...], preferred_element_type=F32))
    e = jnp.dot(p_ref[0, 0].astype(BF16), wp_ref[...], preferred_element_type=F32)
    x = x_ref[0] + _rms(gate * e, g_ref[gi:gi + 1, :])
    out_ref[0] = x
    un_ref[0] = _rms(x, gn_ref[gni:gni + 1, :]).astype(BF16)


def _ple(x, u, g, gn, p, layer, w_proj, w_gate, gi, gni):
    B, S, D = x.shape
    P = p.shape[-1]
    tm = _tile(S, 512)
    full = lambda a: pl.BlockSpec(a.shape, lambda b, i: (0, 0))
    row = pl.BlockSpec((1, tm, D), lambda b, i: (b, i, 0))
    return pl.pallas_call(
        functools.partial(_ple_body, gi=gi, gni=gni),
        out_shape=(jax.ShapeDtypeStruct((B, S, D), F32), jax.ShapeDtypeStruct((B, S, D), BF16)),
        grid=(B, S // tm),
        in_specs=[row, row, full(g), full(gn),
                  pl.BlockSpec((1, 1, tm, P), lambda b, i: (layer, b, i, 0)),
                  full(w_proj), full(w_gate)],
        out_specs=(row, row),
        compiler_params=_params(("parallel", "parallel")),
        name="ple_gate",
    )(x, u, g, gn, p, w_proj, w_gate)


def _rope_tables(S):
    half = HEAD_DIM // 2
    inv = ROPE_THETA ** (-jnp.arange(half, dtype=F32) * 2.0 / HEAD_DIM)
    ang = jnp.arange(S, dtype=F32)[:, None] * inv[None, :]
    cos, sin = jnp.cos(ang), jnp.sin(ang)
    return jnp.concatenate([cos, cos], axis=1), jnp.concatenate([-sin, sin], axis=1)


def _mixer(x, u, g, w_in, b_f, w_br_a, w_br_b, w_out, cos, sin):
    B, S, D = x.shape
    fox_w = w_br_a.shape[0]
    dil_out_w = w_br_b.shape[0]
    n_groups = len(DIL_PATTERNS)
    dil_w = n_groups * dil_out_w
    n_fox = b_f.shape[0]
    assert fox_w == n_fox * HEAD_DIM
    assert w_in.shape[1] == 3 * fox_w + n_fox + 3 * dil_w + 2 * D
    for window, d in DIL_PATTERNS:
        assert window // d == DIL_BLOCK

    f_lo = 3 * fox_w
    b_lo = f_lo + n_fox
    g_lo = b_lo + 3 * dil_w
    w_fox = w_in[:, :f_lo].astype(BF16)
    w_f = jnp.pad(w_in[:, f_lo:b_lo], ((0, 0), (0, HEAD_DIM - n_fox))).astype(BF16)
    w_dil = w_in[:, b_lo:g_lo].astype(BF16)
    w_gate = w_in[:, g_lo:].astype(BF16)

    qt, k, vt = _fox_proj(u, w_fox, n_fox)
    e = _forget_scan(u, w_f, b_f)
    o_a = _fox(qt, k, e, vt)

    qs, ks, vs = [], [], []
    for gidx, (_, d) in enumerate(DIL_PATTERNS):
        q, kk, v = _dil_proj(u, w_dil, cos, sin, gidx, d, n_groups, dil_out_w)
        qs.append(q), ks.append(kk), vs.append(v)
    o_b = _dilated(qs, ks, vs)

    gates = _gate_proj(u, w_gate)
    return _merge(x, g, o_a, o_b, gates, w_br_a.astype(BF16), w_br_b.astype(BF16), w_out.astype(BF16), 3, 4)


def kernel(x, p, norm_g, ffn1_w_in, ffn1_w_out, mix_w_in, fox_b_f, mix_w_br_a, mix_w_br_b, mix_w_out,
           ffn2_w_in, ffn2_w_out, ple_w_proj, ple_w_gate):
    B, S, D = x.shape
    T = B * S
    depth = norm_g.shape[0]
    cos, sin = _rope_tables(S)
    w1_in, w1_out, w2_in, w2_out = (_to_bf16(w) for w in (ffn1_w_in, ffn1_w_out, ffn2_w_in, ffn2_w_out))
    x = x.reshape(T, D)
    u = _norm(x, norm_g[0], 0)
    for i in range(depth):
        g = norm_g[i]
        gn = norm_g[(i + 1) % depth]
        x, u = _ffn(u, x, g, g, w1_in, w1_out, i, 1, 2)
        x, u = _mixer(x.reshape(B, S, D), u.reshape(B, S, D), g, mix_w_in[i], fox_b_f[i], mix_w_br_a[i],
                      mix_w_br_b[i], mix_w_out[i], cos, sin)
        x, u = _ffn(u.reshape(T, D), x.reshape(T, D), g, g, w2_in, w2_out, i, 5, 6)
        x, u = _ple(x.reshape(B, S, D), u.reshape(B, S, D), g, gn, p, i, ple_w_proj[i].astype(BF16),
                    ple_w_gate[i].astype(BF16), 7, 0)
        x, u = x.reshape(T, D), u.reshape(T, D)
    return x.reshape(B, S, D)
```
